```python
import math
import jax, jax.numpy as jnp
from jax import lax
import numpy as np

D_MODEL = 1024
BATCH = 32
SEQ = 256
DEPTH = 4
DEC_BATCH = 8
DEC_SEQ = 2048
PAST_LEN = 512

F32 = jnp.float32
GRID_W = 64
N_MIXERS = 3
N_A = (DEPTH + 2) // 3
N_B = (DEPTH + 1) // 3
N_C = DEPTH // 3
EPS = 1e-6

S5_WIDTH = D_MODEL
S5_GROUP = 16
S5_GROUPS = S5_WIDTH // S5_GROUP
S5_STATE = 64
S5_DT_MIN = 1e-3
S5_DT_MAX = 1e-1

RET_HEADS = 8
RET_QK = D_MODEL
RET_V = 2 * D_MODEL
RET_DK = RET_QK // RET_HEADS
RET_DV = RET_V // RET_HEADS
RET_CHUNK = 128
ROPE_BASE = 10000.0

HY_WIDTH = 2 * D_MODEL
HY_ORDER = 2
HY_SHORT = 3
HY_BANDS = 16
HY_EMB = 2 * HY_BANDS + 1
HY_FILTER_WIDTH = 64
HY_SHORT_DECAY_PCT = 0.3
HY_LONG_DECAY_PCT = 1.5
HY_DECAY_TARGET = 1e-2

kernel_name = 'hybrid_s5_retnet_hyena_prefix_step'


def rmsnorm(x, g):
    xf = x.astype(F32)
    y = xf * lax.rsqrt(jnp.mean(xf * xf, axis=-1, keepdims=True) + EPS)
    return (y * g.astype(F32)).astype(x.dtype)


def ada_mod(cvec, w, b):
    m = (jax.nn.silu(cvec) @ w + b)[:, None, :]
    return m[..., :D_MODEL], m[..., D_MODEL:2 * D_MODEL], m[..., 2 * D_MODEL:]


def _cmul(ar, ai, br, bi):
    return ar * br - ai * bi, ar * bi + ai * br


def s5_scan_dir(u, lam_re, lam_im, log_step, b_re, b_im, c_re, c_im, h0_re, h0_im):
    lam_re = lam_re.astype(F32)
    lam_im = lam_im.astype(F32)
    dt = jnp.exp(log_step.astype(F32))[:, None]
    mag = jnp.exp(lam_re * dt)
    ab_re, ab_im = mag * jnp.cos(lam_im * dt), mag * jnp.sin(lam_im * dt)
    den = lam_re * lam_re + lam_im * lam_im
    nr, ni = ab_re - 1.0, ab_im
    f_re = (nr * lam_re + ni * lam_im) / den
    f_im = (ni * lam_re - nr * lam_im) / den
    bb_re, bb_im = _cmul(f_re[..., None], f_im[..., None], b_re.astype(F32), b_im.astype(F32))
    bu_re = jnp.einsum('blgh,gph->blgp', u, bb_re)
    bu_im = jnp.einsum('blgh,gph->blgp', u, bb_im)
    i_re, i_im = _cmul(ab_re, ab_im, h0_re.astype(F32), h0_im.astype(F32))
    bu_re = bu_re.at[:, 0].add(i_re)
    bu_im = bu_im.at[:, 0].add(i_im)
    L = u.shape[1]
    a_re = jnp.broadcast_to(ab_re, (1, L) + ab_re.shape)
    a_im = jnp.broadcast_to(ab_im, (1, L) + ab_im.shape)

    def combine(e1, e2):
        a1r, a1i, b1r, b1i = e1
        a2r, a2i, b2r, b2i = e2
        ar, ai = _cmul(a2r, a2i, a1r, a1i)
        br, bi = _cmul(a2r, a2i, b1r, b1i)
        return ar, ai, br + b2r, bi + b2i

    _, _, h_re, h_im = lax.associative_scan(combine, (a_re, a_im, bu_re, bu_im), axis=1)
    y = (jnp.einsum('blgp,ghp->blgh', h_re, c_re.astype(F32))
         - jnp.einsum('blgp,ghp->blgh', h_im, c_im.astype(F32)))
    return y, h_re[:, -1], h_im[:, -1]


def s5_branch(h, h0, w_in, lam_re, lam_im, log_step, b_re, b_im, c_re, c_im, d_skip, w_glu, b_glu, w_out):
    bsz, L, _ = h.shape
    proj = h @ w_in
    u, gate = proj[..., :S5_WIDTH], proj[..., S5_WIDTH:]
    uf = u.astype(F32)
    ug = uf.reshape(bsz, L, S5_GROUPS, S5_GROUP)
    y = d_skip.astype(F32) * uf
    states = []
    for d in range(2):
        ud = ug if d == 0 else ug[:, ::-1]
        yd, hr, hi = s5_scan_dir(ud, lam_re[d], lam_im[d], log_step[d], b_re[d], b_im[d],
                                 c_re[d], c_im[d], h0[:, d, 0], h0[:, d, 1])
        yd = yd if d == 0 else yd[:, ::-1]
        y = y + yd.reshape(bsz, L, S5_WIDTH)
        states.append(jnp.stack([hr, hi], axis=1))
    g = jax.nn.gelu(y)
    y = g * jax.nn.sigmoid(g @ w_glu.astype(F32) + b_glu.astype(F32))
    out = (y * jax.nn.silu(gate.astype(F32))).astype(h.dtype) @ w_out
    return out, jnp.stack(states, axis=1)


def rope_2d(x):
    L = x.shape[1]
    rows = L // GRID_W
    t = jnp.arange(rows * GRID_W)
    row, col = t // GRID_W, t % GRID_W
    half = RET_DK // 2
    nfreq = half // 2
    inv = ROPE_BASE ** (-jnp.arange(nfreq, dtype=F32) / nfreq)

    def rot(xs, pos):
        ang = pos.astype(F32)[:, None] * inv[None, :]
        cos, sin = jnp.cos(ang)[None, :, None, :], jnp.sin(ang)[None, :, None, :]
        x1, x2 = xs[..., :nfreq], xs[..., nfreq:]
        return jnp.concatenate([x1 * cos - x2 * sin, x1 * sin + x2 * cos], axis=-1)

    return jnp.concatenate([rot(x[..., :half], row), rot(x[..., half:], col)], axis=-1)


def retention_chunkwise(q, k, v, log_g, s0, inclusive):
    bsz, L, H, _ = q.shape
    C = RET_CHUNK
    N = L // C
    idx = jnp.arange(C, dtype=F32)
    diff = idx[:, None] - idx[None, :]
    mask = (diff >= 0) if inclusive else (diff > 0)
    d_intra = jnp.where(mask[None], jnp.exp(jnp.maximum(diff, 0.0)[None] * log_g[:, None, None]), 0.0)
    q_dec = jnp.exp((idx[:, None] + 1.0) * log_g[None, :])
    k_dec = jnp.exp((C - 1.0 - idx)[:, None] * log_g[None, :])
    c_dec = jnp.exp(C * log_g)

    def to_chunks(t):
        return jnp.moveaxis(t.reshape(bsz, N, C, H, t.shape[-1]), 1, 0)

    def step(S, inp):
        qc, kc, vc = inp
        att = jnp.einsum('bihd,bjhd->bhij', qc, kc) * d_intra[None]
        o = (jnp.einsum('bhij,bjhe->bihe', att, vc)
             + jnp.einsum('bihd,bhde->bihe', qc, S) * q_dec[None, :, :, None])
        S = S * c_dec[None, :, None, None] + jnp.einsum('bjhd,bjhe->bhde', kc * k_dec[None, :, :, None], vc)
        return S, o

    S, o = lax.scan(step, s0.astype(F32), (to_chunks(q), to_chunks(k), to_chunks(v)))
    return jnp.moveaxis(o, 0, 1).reshape(bsz, L, H, v.shape[-1]), S


def retention_branch(h, s0, grid_pos, w_in, decay_logit, w_out):
    bsz, L, _ = h.shape
    proj = h @ w_in
    q, k, v, gate = jnp.split(proj, [RET_QK, 2 * RET_QK, 2 * RET_QK + RET_V], axis=-1)
    q = q.astype(F32).reshape(bsz, L, RET_HEADS, RET_DK)
    k = k.astype(F32).reshape(bsz, L, RET_HEADS, RET_DK) * (RET_DK ** -0.5)
    v = v.astype(F32).reshape(bsz, L, RET_HEADS, RET_DV)
    if grid_pos:
        q, k = rope_2d(q), rope_2d(k)
    lg = jax.nn.log_sigmoid(decay_logit.astype(F32))
    o_f, s_f = retention_chunkwise(q, k, v, lg[0], s0[:, 0], True)
    o_b, s_b = retention_chunkwise(q[:, ::-1], k[:, ::-1], v[:, ::-1], lg[1], s0[:, 1], False)
    o = o_f + o_b[:, ::-1]
    o = o * lax.rsqrt(jnp.mean(o * o, axis=-1, keepdims=True) + EPS)
    out = (o.reshape(bsz, L, RET_V) * jax.nn.silu(gate.astype(F32))).astype(h.dtype) @ w_out
    return out, jnp.stack([s_f, s_b], axis=1)


def hyena_filters(L, w1, b1, w2, b2, w3):
    t = jnp.linspace(0.0, 1.0, L, dtype=F32)[:, None]
    w = 2.0 * math.pi * jnp.arange(L, dtype=F32)[:, None] / L
    f = jnp.linspace(1e-4, HY_BANDS - 1.0, HY_BANDS, dtype=F32)[None, :]
    feat = jnp.concatenate([t, jnp.cos(f * w), -jnp.sin(f * w)], axis=-1)
    z = jnp.sin(feat @ w1.astype(F32) + b1.astype(F32))
    z = jnp.sin(z @ w2.astype(F32) + b2.astype(F32))
    filt = (z @ w3.astype(F32)).reshape(L, 2, HY_ORDER, HY_WIDTH)
    max_decay = math.log(HY_DECAY_TARGET) / HY_SHORT_DECAY_PCT
    min_decay = math.log(HY_DECAY_TARGET) / HY_LONG_DECAY_PCT
    deltas = jnp.linspace(min_decay, max_decay, HY_WIDTH, dtype=F32)
    filt = filt * jnp.exp(-t * jnp.abs(deltas)[None, :])[:, None, None, :]
    filt = filt / (jnp.sum(jnp.abs(filt), axis=(0, 1), keepdims=True) + EPS)
    full = jnp.concatenate([filt[:, 0], jnp.zeros((1, HY_ORDER, HY_WIDTH), F32), filt[:0:-1, 1]], axis=0)
    return jnp.fft.rfft(full, axis=0)


def hyena_branch(h, w_in, conv_w, conv_b, f_w1, f_b1, f_w2, f_b2, f_w3, skip, w_out):
    bsz, L, _ = h.shape
    proj = h @ w_in
    xs, gate = proj[..., :3 * HY_WIDTH].astype(F32), proj[..., 3 * HY_WIDTH:]
    pad = HY_SHORT // 2
    xp = jnp.pad(xs, ((0, 0), (pad, pad), (0, 0)))
    cw = conv_w.astype(F32)
    xs = conv_b.astype(F32) + xp[:, 0:L] * cw[0]
    for j in range(1, HY_SHORT):
        xs = xs + xp[:, j:j + L] * cw[j]
    v, x1, x2 = jnp.split(xs, 3, axis=-1)
    k_hat = hyena_filters(L, f_w1, f_b1, f_w2, f_b2, f_w3)
    sk = skip.astype(F32)
    z = v
    for o, xg in enumerate((x1, x2)):
        z_hat = jnp.fft.rfft(z, n=2 * L, axis=1)
        zc = jnp.fft.irfft(z_hat * k_hat[None, :, o], n=2 * L, axis=1)[:, :L]
        z = xg * (zc + z * sk[o])
    return (z * jax.nn.silu(gate.astype(F32))).astype(h.dtype) @ w_out


def setup_inputs(seed: int = 0) -> dict:
    key = jax.random.key(seed)
    ks = iter(jax.random.split(key, 48))

    def nrm(shape, s):
        return s * jax.random.normal(next(ks), shape, F32)

    G, P, Hg = S5_GROUPS, S5_STATE, S5_GROUP
    n_idx = jnp.arange(P, dtype=F32)
    gam_logit = jnp.log(2.0 ** (5.0 + jnp.arange(RET_HEADS, dtype=F32)) - 1.0)
    return {
        'x_prompt': nrm((BATCH, SEQ, D_MODEL), 1.0),
        'x_sample': nrm((DEC_BATCH, DEC_SEQ, D_MODEL), 1.0),
        'c': nrm((DEC_BATCH, D_MODEL), 1.0),
        'state_s5': nrm((DEC_BATCH, N_A, 2, 2, G, P), 0.1),
        'state_ret': nrm((DEC_BATCH, N_B, 2, RET_HEADS, RET_DK, RET_DV), 0.5),
        'c_ctx': nrm((D_MODEL,), 1.0),
        'norm_g': 1.0 + nrm((DEPTH, D_MODEL), 0.02),
        'mod_w': nrm((DEPTH, D_MODEL, 3 * D_MODEL), 0.5 * D_MODEL ** -0.5),
        'mod_b': nrm((DEPTH, 3 * D_MODEL), 0.02),
        's5_w_in': nrm((N_A, D_MODEL, 2 * S5_WIDTH), D_MODEL ** -0.5),
        's5_lam_re': -0.5 + nrm((N_A, 2, G, P), 0.01),
        's5_lam_im': math.pi * n_idx + nrm((N_A, 2, G, P), 0.01),
        's5_log_step': jax.random.uniform(next(ks), (N_A, 2, G), F32, math.log(S5_DT_MIN), math.log(S5_DT_MAX)),
        's5_b_re': nrm((N_A, 2, G, P, Hg), (2.0 * Hg) ** -0.5),
        's5_b_im': nrm((N_A, 2, G, P, Hg), (2.0 * Hg) ** -0.5),
        's5_c_re': nrm((N_A, 2, G, Hg, P), P ** -0.5),
        's5_c_im': nrm((N_A, 2, G, Hg, P), P ** -0.5),
        's5_d': nrm((N_A, S5_WIDTH), 1.0),
        's5_w_glu': nrm((N_A, S5_WIDTH, S5_WIDTH), S5_WIDTH ** -0.5),
        's5_b_glu': nrm((N_A, S5_WIDTH), 0.02),
        's5_w_out': nrm((N_A, S5_WIDTH, D_MODEL), S5_WIDTH ** -0.5),
        'ret_w_in': nrm((N_B, D_MODEL, 2 * RET_QK + 2 * RET_V), D_MODEL ** -0.5),
        'ret_decay_logit': gam_logit + nrm((N_B, 2, RET_HEADS), 0.05),
        'ret_w_out': nrm((N_B, RET_V, D_MODEL), RET_V ** -0.5),
        'hy_w_in': nrm((N_C, D_MODEL, 4 * HY_WIDTH), D_MODEL ** -0.5),
        'hy_conv_w': nrm((N_C, HY_SHORT, 3 * HY_WIDTH), HY_SHORT ** -0.5),
        'hy_conv_b': nrm((N_C, 3 * HY_WIDTH), 0.02),
        'hy_f_w1': nrm((N_C, HY_EMB, HY_FILTER_WIDTH), HY_EMB ** -0.5),
        'hy_f_b1': nrm((N_C, HY_FILTER_WIDTH), 0.1),
        'hy_f_w2': nrm((N_C, HY_FILTER_WIDTH, HY_FILTER_WIDTH), HY_FILTER_WIDTH ** -0.5),
        'hy_f_b2': nrm((N_C, HY_FILTER_WIDTH), 0.1),
        'hy_f_w3': nrm((N_C, HY_FILTER_WIDTH, 2 * HY_ORDER * HY_WIDTH), HY_FILTER_WIDTH ** -0.5),
        'hy_skip': nrm((N_C, HY_ORDER, HY_WIDTH), 1.0),
        'hy_w_out': nrm((N_C, HY_WIDTH, D_MODEL), HY_WIDTH ** -0.5),
        'final_g': 1.0 + nrm((D_MODEL,), 0.02),
    }


def reference(x_prompt, x_sample, c, state_s5, state_ret, c_ctx, norm_g, mod_w, mod_b,
              s5_w_in, s5_lam_re, s5_lam_im, s5_log_step, s5_b_re, s5_b_im, s5_c_re, s5_c_im,
              s5_d, s5_w_glu, s5_b_glu, s5_w_out,
              ret_w_in, ret_decay_logit, ret_w_out,
              hy_w_in, hy_conv_w, hy_conv_b, hy_f_w1, hy_f_b1, hy_f_w2, hy_f_b2, hy_f_w3, hy_skip, hy_w_out,
              final_g):
    n_ctx = x_prompt.shape[0]
    xc, xl = x_prompt, x_sample
    s5_new, ret_new = [], []
    for i in range(DEPTH):
        kind, j = i % N_MIXERS, i // N_MIXERS
        sh_c, sc_c, gt_c = ada_mod(c_ctx[None, :], mod_w[i], mod_b[i])
        sh_l, sc_l, gt_l = ada_mod(c, mod_w[i], mod_b[i])
        hc = rmsnorm(xc, norm_g[i]) * (1.0 + sc_c) + sh_c
        hl = rmsnorm(xl, norm_g[i]) * (1.0 + sc_l) + sh_l
        if kind == 0:
            p = (s5_w_in[j], s5_lam_re[j], s5_lam_im[j], s5_log_step[j], s5_b_re[j], s5_b_im[j],
                 s5_c_re[j], s5_c_im[j], s5_d[j], s5_w_glu[j], s5_b_glu[j], s5_w_out[j])
            zero = jnp.zeros((n_ctx, 2, 2, S5_GROUPS, S5_STATE), F32)
            oc, st = s5_branch(hc, zero, *p)
            ol, _ = s5_branch(hl, state_s5[:, j], *p)
            s5_new.append(st)
        elif kind == 1:
            p = (ret_w_in[j], ret_decay_logit[j], ret_w_out[j])
            zero = jnp.zeros((n_ctx, 2, RET_HEADS, RET_DK, RET_DV), F32)
            oc, st = retention_branch(hc, zero, False, *p)
            ol, _ = retention_branch(hl, state_ret[:, j], True, *p)
            ret_new.append(st)
        else:
            p = (hy_w_in[j], hy_conv_w[j], hy_conv_b[j], hy_f_w1[j], hy_f_b1[j], hy_f_w2[j],
                 hy_f_b2[j], hy_f_w3[j], hy_skip[j], hy_w_out[j])
            oc = hyena_branch(hc, *p)
            ol = hyena_branch(hl, *p)
        xc = xc + gt_c * oc
        xl = xl + gt_l * ol
    y_prompt = rmsnorm(xc, final_g)
    y_sample = rmsnorm(xl, final_g)
    new_state_s5 = jnp.stack(s5_new, axis=1)
    new_state_ret = jnp.stack(ret_new, axis=1)
    return (y_prompt, y_sample, new_state_s5, new_state_ret)
```

```python
import functools
import math

import jax
import jax.numpy as jnp
from jax import lax
from jax.experimental import pallas as pl
from jax.experimental.pallas import tpu as pltpu

F32 = jnp.float32
BF16 = jnp.bfloat16

D_MODEL = 1024
DEPTH = 4
N_MIXERS = 3
EPS = 1e-6
GRID_W = 64

S5_GROUP = 16
S5_GROUPS = D_MODEL // S5_GROUP
S5_STATE = 64
S5_CHUNK = 16
S5_TILE = S5_CHUNK * S5_GROUP

RET_HEADS = 8
RET_QK = D_MODEL
RET_V = 2 * D_MODEL
RET_DK = RET_QK // RET_HEADS
RET_DV = RET_V // RET_HEADS
RET_BLOCK = 256
ROPE_BASE = 10000.0

HY_WIDTH = 2 * D_MODEL
HY_ORDER = 2
HY_SHORT = 3
HY_BANDS = 16
HY_SHORT_DECAY_PCT = 0.3
HY_LONG_DECAY_PCT = 1.5
HY_DECAY_TARGET = 1e-2
HY_ROW_BLOCK = 512

VMEM_LIMIT_BYTES = 56 * 1024 * 1024
OUT_ROWS = 512


def _params(*sem, vmem=None):
    return pltpu.CompilerParams(dimension_semantics=sem, vmem_limit_bytes=vmem)


def _silu(x):
    return x * jax.nn.sigmoid(x)


def _mod_body(c_ref, w_ref, b_ref, o_ref):
    a = _silu(c_ref[...]).astype(BF16)
    o_ref[0] = jnp.dot(a, w_ref[0].astype(BF16), preferred_element_type=F32) + b_ref[0]


def ada_mod_all(cvecs, mod_w, mod_b):
    r = cvecs.shape[0]
    tn = D_MODEL
    return pl.pallas_call(
        _mod_body,
        grid=(DEPTH, 3 * D_MODEL // tn),
        in_specs=[pl.BlockSpec((r, D_MODEL), lambda i, j: (0, 0)),
                  pl.BlockSpec((1, D_MODEL, tn), lambda i, j: (i, 0, j)),
                  pl.BlockSpec((1, 1, tn), lambda i, j: (i, 0, j))],
        out_specs=pl.BlockSpec((1, r, tn), lambda i, j: (i, 0, j)),
        out_shape=jax.ShapeDtypeStruct((DEPTH, r, 3 * D_MODEL), F32),
        compiler_params=_params("parallel", "parallel"),
        name="ada_mod",
    )(cvecs, mod_w, mod_b.reshape(DEPTH, 1, 3 * D_MODEL))


def _inproj_body(x_ref, g_ref, mod_ref, w_ref, o_ref, h_scr):
    @pl.when(pl.program_id(1) == 0)
    def _():
        x = x_ref[...]
        y = x * lax.rsqrt(jnp.mean(x * x, axis=-1, keepdims=True) + EPS) * g_ref[...]
        h_scr[...] = (y * (1.0 + mod_ref[1:2, :]) + mod_ref[0:1, :]).astype(BF16)

    o_ref[...] = jnp.dot(h_scr[...], w_ref[...], preferred_element_type=F32).astype(o_ref.dtype)


def in_proj(x, g, mod, w, rows_per_mod, tm, tn):
    n, nout = x.shape[0], w.shape[1]
    return pl.pallas_call(
        _inproj_body,
        grid=(n // tm, nout // tn),
        in_specs=[pl.BlockSpec((tm, D_MODEL), lambda i, j: (i, 0)),
                  pl.BlockSpec((1, D_MODEL), lambda i, j: (0, 0)),
                  pl.BlockSpec((None, 3, D_MODEL), lambda i, j: ((i * tm) // rows_per_mod, 0, 0)),
                  pl.BlockSpec((D_MODEL, tn), lambda i, j: (0, j))],
        out_specs=pl.BlockSpec((tm, tn), lambda i, j: (i, j)),
        out_shape=jax.ShapeDtypeStruct((n, nout), BF16),
        scratch_shapes=[pltpu.VMEM((tm, D_MODEL), BF16)],
        compiler_params=_params("parallel", "arbitrary", vmem=VMEM_LIMIT_BYTES),
        name="in_proj",
    )(x, g.reshape(1, D_MODEL), mod, w)


def _outproj_body(y_ref, w_ref, x_ref, mod_ref, o_ref):
    out = jnp.dot(y_ref[...], w_ref[...], preferred_element_type=F32)
    o_ref[...] = x_ref[...] + mod_ref[2:3, :] * out


def out_proj(y, w, x, mod, rows_per_mod, tm):
    n, width = y.shape
    return pl.pallas_call(
        _outproj_body,
        grid=(n // tm,),
        in_specs=[pl.BlockSpec((tm, width), lambda i: (i, 0)),
                  pl.BlockSpec((width, D_MODEL), lambda i: (0, 0)),
                  pl.BlockSpec((tm, D_MODEL), lambda i: (i, 0)),
                  pl.BlockSpec((None, 3, D_MODEL), lambda i: ((i * tm) // rows_per_mod, 0, 0))],
        out_specs=pl.BlockSpec((tm, D_MODEL), lambda i: (i, 0)),
        out_shape=jax.ShapeDtypeStruct((n, D_MODEL), F32),
        compiler_params=_params("parallel", vmem=VMEM_LIMIT_BYTES),
        name="out_proj",
    )(y, w, x, mod)


def _s5_out_body(y_ref, gate_ref, wg_ref, bg_ref, wo_ref, x_ref, mod_ref, o_ref):
    g = jax.nn.gelu(y_ref[...])
    t = jnp.dot(g.astype(BF16), wg_ref[...], preferred_element_type=F32) + bg_ref[...]
    z = g * jax.nn.sigmoid(t) * _silu(gate_ref[...].astype(F32))
    out = jnp.dot(z.astype(BF16), wo_ref[...], preferred_element_type=F32)
    o_ref[...] = x_ref[...] + mod_ref[2:3, :] * out


def s5_out(y, proj, w_glu, b_glu, w_out, x, mod, rows_per_mod, tm):
    n = y.shape[0]
    return pl.pallas_call(
        _s5_out_body,
        grid=(n // tm,),
        in_specs=[pl.BlockSpec((tm, D_MODEL), lambda i: (i, 0)),
                  pl.BlockSpec((tm, D_MODEL), lambda i: (i, 1)),
                  pl.BlockSpec((D_MODEL, D_MODEL), lambda i: (0, 0)),
                  pl.BlockSpec((1, D_MODEL), lambda i: (0, 0)),
                  pl.BlockSpec((D_MODEL, D_MODEL), lambda i: (0, 0)),
                  pl.BlockSpec((tm, D_MODEL), lambda i: (i, 0)),
                  pl.BlockSpec((None, 3, D_MODEL), lambda i: ((i * tm) // rows_per_mod, 0, 0))],
        out_specs=pl.BlockSpec((tm, D_MODEL), lambda i: (i, 0)),
        out_shape=jax.ShapeDtypeStruct((n, D_MODEL), F32),
        compiler_params=_params("parallel", vmem=VMEM_LIMIT_BYTES),
        name="s5_out",
    )(y, proj, w_glu, b_glu.reshape(1, D_MODEL), w_out, x, mod)


def _final_norm_body(x_ref, g_ref, o_ref):
    x = x_ref[...]
    o_ref[...] = x * lax.rsqrt(jnp.mean(x * x, axis=-1, keepdims=True) + EPS) * g_ref[...]


def final_norm(x, g, tm):
    n = x.shape[0]
    return pl.pallas_call(
        _final_norm_body,
        grid=(n // tm,),
        in_specs=[pl.BlockSpec((tm, D_MODEL), lambda i: (i, 0)),
                  pl.BlockSpec((1, D_MODEL), lambda i: (0, 0))],
        out_specs=pl.BlockSpec((tm, D_MODEL), lambda i: (i, 0)),
        out_shape=jax.ShapeDtypeStruct((n, D_MODEL), F32),
        compiler_params=_params("parallel"),
        name="final_norm",
    )(x, g.reshape(1, D_MODEL))


def s5_tables(lam_re, lam_im, log_step, b_re, b_im, c_re, c_im, d_skip):
    t_len = S5_CHUNK
    hp = lax.Precision.HIGHEST
    dt = jnp.exp(log_step)[..., None]
    ab_re = jnp.exp(lam_re * dt) * jnp.cos(lam_im * dt)
    ab_im = jnp.exp(lam_re * dt) * jnp.sin(lam_im * dt)
    den = lam_re * lam_re + lam_im * lam_im
    nr, ni = ab_re - 1.0, ab_im
    f_re = (nr * lam_re + ni * lam_im) / den
    f_im = (ni * lam_re - nr * lam_im) / den
    bb_re = f_re[..., None] * b_re - f_im[..., None] * b_im
    bb_im = f_re[..., None] * b_im + f_im[..., None] * b_re
    ks = jnp.arange(t_len + 1, dtype=F32)[:, None, None, None]
    pw_mag = jnp.exp(ks * (lam_re * dt)[None])
    pw_re = pw_mag * jnp.cos(ks * (lam_im * dt)[None])
    pw_im = pw_mag * jnp.sin(ks * (lam_im * dt)[None])

    ca_re = c_re[None] * pw_re[:, :, :, None, :] - c_im[None] * pw_im[:, :, :, None, :]
    ca_im = c_re[None] * pw_im[:, :, :, None, :] + c_im[None] * pw_re[:, :, :, None, :]
    kk = (jnp.einsum('kdgop,dgph->kdgoh', ca_re[:t_len], bb_re, precision=hp)
          - jnp.einsum('kdgop,dgph->kdgoh', ca_im[:t_len], bb_im, precision=hp))
    sig = jnp.arange(t_len)[:, None]
    tau = jnp.arange(t_len)[None, :]
    diff = tau - sig
    kf = jnp.where((diff >= 0)[:, :, None, None, None], kk[jnp.clip(diff, 0, t_len - 1), 0], 0.0)
    kb = jnp.where((diff <= 0)[:, :, None, None, None], kk[jnp.clip(-diff, 0, t_len - 1), 1], 0.0)
    m = kf + kb
    eye_t = jnp.eye(t_len, dtype=F32)[:, :, None, None, None]
    eye_h = jnp.eye(S5_GROUP, dtype=F32)[None, None, None]
    m = m + eye_t * eye_h * d_skip.reshape(S5_GROUPS, S5_GROUP)[None, None, :, :, None]
    m = jnp.transpose(m, (2, 0, 4, 1, 3)).reshape(S5_GROUPS, S5_TILE, S5_TILE)

    sidx = jnp.arange(t_len)
    def _f(pw_r, pw_i, d, expo):
        ar, ai = pw_r[expo, d], pw_i[expo, d]
        fr = ar[..., None] * bb_re[d][None] - ai[..., None] * bb_im[d][None]
        fi = ar[..., None] * bb_im[d][None] + ai[..., None] * bb_re[d][None]
        to = lambda a: jnp.transpose(a, (1, 0, 3, 2)).reshape(S5_GROUPS, S5_TILE, S5_STATE)
        return to(fr), to(fi)
    ff_re, ff_im = _f(pw_re, pw_im, 0, t_len - 1 - sidx)
    fb_re, fb_im = _f(pw_re, pw_im, 1, sidx)
    f = jnp.concatenate([ff_re, fb_re, ff_im, fb_im], axis=-1)

    def _e(d, expo):
        er = ca_re[expo, d]
        ei = ca_im[expo, d]
        to = lambda a: jnp.transpose(a, (1, 3, 0, 2)).reshape(S5_GROUPS, S5_STATE, S5_TILE)
        return to(er), to(-ei)
    ef_re, ef_im = _e(0, sidx + 1)
    eb_re, eb_im = _e(1, t_len - sidx)
    e = jnp.concatenate([ef_re, eb_re, ef_im, eb_im], axis=1)

    a_re = jnp.concatenate([pw_re[t_len, 0], pw_re[t_len, 1]], axis=-1)[:, None, :]
    a_im = jnp.concatenate([pw_im[t_len, 0], pw_im[t_len, 1]], axis=-1)[:, None, :]
    return m.astype(BF16), f.astype(BF16), e.astype(BF16), a_re, a_im


def _s5_scan_body(u_ref, m_ref, f_ref, e_ref, ar_ref, ai_ref, h0_ref, y_ref, hf_ref,
                  xr_s, xi_s, sar_s, sai_s, sbr_s, sbi_s, *, nchunk, nseq):
    rows = nchunk * nseq
    half = 2 * S5_STATE
    u = u_ref[...]
    x = jnp.dot(u, f_ref[...], preferred_element_type=F32)
    xr_s[...] = x[:, :half].reshape(nchunk, nseq, half)
    xi_s[...] = x[:, half:].reshape(nchunk, nseq, half)
    is_fwd = lax.broadcasted_iota(jnp.int32, (nseq, half), 1) < S5_STATE
    ar = ar_ref[...]
    ai = ai_ref[...]

    def step(i, carry):
        sr, si = carry
        r = nchunk - 1 - i
        xr = jnp.where(is_fwd, xr_s[i], xr_s[r])
        xi = jnp.where(is_fwd, xi_s[i], xi_s[r])
        sar_s[i] = sr
        sai_s[i] = si
        sbr_s[r] = sr
        sbi_s[r] = si
        return ar * sr - ai * si + xr, ar * si + ai * sr + xi

    sr, si = lax.fori_loop(0, nchunk, step, (h0_ref[0], h0_ref[1]))
    hf_ref[0] = sr
    hf_ref[1] = si
    is_fwd3 = lax.broadcasted_iota(jnp.int32, (nchunk, nseq, half), 2) < S5_STATE
    hr = jnp.where(is_fwd3, sar_s[...], sbr_s[...]).reshape(rows, half)
    hi = jnp.where(is_fwd3, sai_s[...], sbi_s[...]).reshape(rows, half)
    h = jnp.concatenate([hr, hi], axis=1).astype(BF16)
    y_ref[...] = (jnp.dot(u, m_ref[...], preferred_element_type=F32)
                  + jnp.dot(h, e_ref[...], preferred_element_type=F32))


def s5_scan(u_t, m, f, e, a_re, a_im, h0, nchunk, nseq):
    rows = nchunk * nseq
    half = 2 * S5_STATE
    body = functools.partial(_s5_scan_body, nchunk=nchunk, nseq=nseq)
    wspec = pl.BlockSpec((None, S5_TILE, S5_TILE), lambda g: (g, 0, 0))
    aspec = pl.BlockSpec((None, 1, half), lambda g: (g, 0, 0))
    hspec = pl.BlockSpec((None, 2, nseq, half), lambda g: (g, 0, 0, 0))
    state = pltpu.VMEM((nchunk, nseq, half), F32)
    return pl.pallas_call(
        body,
        grid=(S5_GROUPS,),
        in_specs=[pl.BlockSpec((None, rows, S5_TILE), lambda g: (g, 0, 0)),
                  wspec, wspec, wspec, aspec, aspec, hspec],
        out_specs=[pl.BlockSpec((None, rows, S5_TILE), lambda g: (g, 0, 0)), hspec],
        out_shape=[jax.ShapeDtypeStruct((S5_GROUPS, rows, S5_TILE), F32),
                   jax.ShapeDtypeStruct((S5_GROUPS, 2, nseq, half), F32)],
        scratch_shapes=[state] * 6,
        compiler_params=_params("parallel"),
        name="s5_scan",
    )(u_t, m, f, e, a_re, a_im, h0)


def s5_layer(x, g, mod, rows_per_mod, nseq, seq_len, h0, w_in, tables, w_glu, b_glu, w_out, tm):
    nchunk = seq_len // S5_CHUNK
    proj = in_proj(x, g, mod, w_in, rows_per_mod, tm, D_MODEL)
    u = proj.reshape(nseq, nchunk, S5_CHUNK, 2 * S5_GROUPS, S5_GROUP)[:, :, :, :S5_GROUPS]
    u_t = jnp.transpose(u, (3, 1, 0, 2, 4)).reshape(S5_GROUPS, nchunk * nseq, S5_TILE)
    if h0 is None:
        h0_t = jnp.zeros((S5_GROUPS, 2, nseq, 2 * S5_STATE), F32)
    else:
        h0_t = jnp.transpose(h0, (3, 2, 0, 1, 4)).reshape(S5_GROUPS, 2, nseq, 2 * S5_STATE)
    m, f, e, a_re, a_im = tables
    y_t, hf = s5_scan(u_t, m, f, e, a_re, a_im, h0_t, nchunk, nseq)
    y = jnp.transpose(y_t.reshape(S5_GROUPS, nchunk, nseq, S5_CHUNK, S5_GROUP), (2, 1, 3, 0, 4))
    y = y.reshape(nseq * seq_len, D_MODEL)
    x_new = s5_out(y, proj, w_glu, b_glu, w_out, x, mod, rows_per_mod, min(tm, OUT_ROWS))
    states = jnp.transpose(hf.reshape(S5_GROUPS, 2, nseq, 2, S5_STATE), (2, 3, 1, 0, 4))
    return x_new, states


def _ret_body(*refs, seq_len, rope, has_init, want_state):
    refs = list(refs)
    q_ref, k_ref, v_ref, gate_ref, lg_ref = refs[:5]
    pos = 5
    if rope:
        cos_ref, sin_ref = refs[pos:pos + 2]
        pos += 2
    if has_init:
        s0_ref = refs[pos]
        pos += 1
    o_ref = refs[pos]
    pos += 1
    if want_state:
        sfin_ref = refs[pos]
        pos += 1
    sf_scr = refs[pos]

    c_len = RET_BLOCK
    nblk = seq_len // c_len
    lgf = lg_ref[0:1, 0:1]
    lgb = lg_ref[1:2, 0:1]
    ii = lax.broadcasted_iota(jnp.int32, (c_len, c_len), 0)
    jj = lax.broadcasted_iota(jnp.int32, (c_len, c_len), 1)
    diff = (ii - jj).astype(F32)
    decay = jnp.exp(jnp.abs(diff) * jnp.where(diff >= 0, lgf, lgb))
    ic = lax.broadcasted_iota(jnp.int32, (c_len, 1), 0).astype(F32)
    q_dec_f = jnp.exp((ic + 1.0) * lgf)
    q_dec_b = jnp.exp((c_len - ic) * lgb)
    k_dec_f = jnp.exp((c_len - 1.0 - ic) * lgf)
    k_dec_b = jnp.exp(ic * lgb)
    c_dec_f = jnp.exp(c_len * lgf)
    c_dec_b = jnp.exp(c_len * lgb)

    q = q_ref[...].astype(F32)
    k = k_ref[...].astype(F32)
    if rope:
        lane = lax.broadcasted_iota(jnp.int32, (seq_len, RET_DK), 1)
        first = (lane % (RET_DK // 2)) < (RET_DK // 4)
        cos = cos_ref[...]
        sin = sin_ref[...]

        def rot(x):
            swapped = jnp.where(first, pltpu.roll(x, RET_DK - RET_DK // 4, 1), pltpu.roll(x, RET_DK // 4, 1))
            return x * cos + swapped * sin

        q = rot(q)
        k = rot(k)

    def blk(a, c):
        return a[c * c_len:(c + 1) * c_len]

    def kv_state(kd, c):
        kt = jnp.transpose(blk(k, c) * kd).astype(BF16)
        return jnp.dot(kt, v_ref[c * c_len:(c + 1) * c_len, :], preferred_element_type=F32)

    use_state = has_init or nblk > 1
    s_f = s0_ref[0] if has_init else jnp.zeros((RET_DK, RET_DV), F32)
    for c in range(nblk):
        if use_state:
            sf_scr[c] = s_f
        if want_state or c < nblk - 1:
            s_f = c_dec_f * s_f + kv_state(k_dec_f, c)
    if want_state:
        sfin_ref[0] = s_f

    s_b = s0_ref[1] if has_init else jnp.zeros((RET_DK, RET_DV), F32)
    for c in range(nblk - 1, -1, -1):
        qc = blk(q, c)
        kc = blk(k, c)
        vc = v_ref[c * c_len:(c + 1) * c_len, :]
        s = lax.dot_general(qc.astype(BF16), kc.astype(BF16), (((1,), (1,)), ((), ())),
                            preferred_element_type=F32)
        o = jnp.dot((s * decay).astype(BF16), vc, preferred_element_type=F32)
        if use_state:
            o = o + jnp.dot((qc * q_dec_f).astype(BF16), sf_scr[c].astype(BF16), preferred_element_type=F32)
            o = o + jnp.dot((qc * q_dec_b).astype(BF16), s_b.astype(BF16), preferred_element_type=F32)
        o = o * lax.rsqrt(jnp.mean(o * o, axis=-1, keepdims=True) + EPS)
        gate = gate_ref[c * c_len:(c + 1) * c_len, :].astype(F32)
        o_ref[c * c_len:(c + 1) * c_len, :] = (o * _silu(gate)).astype(o_ref.dtype)
        if want_state or c > 0:
            s_b = c_dec_b * s_b + kv_state(k_dec_b, c)
    if want_state:
        sfin_ref[1] = s_b


def retention_core(proj3, lg, rope_tabs, s0, want_state):
    nseq, seq_len, _ = proj3.shape
    nblk = seq_len // RET_BLOCK
    rope = rope_tabs is not None
    has_init = s0 is not None
    kq = RET_QK // RET_DK
    in_specs = [pl.BlockSpec((None, seq_len, RET_DK), lambda b, h: (b, 0, h)),
                pl.BlockSpec((None, seq_len, RET_DK), lambda b, h: (b, 0, kq + h)),
                pl.BlockSpec((None, seq_len, RET_DV), lambda b, h: (b, 0, 2 * RET_QK // RET_DV + h)),
                pl.BlockSpec((None, seq_len, RET_DV), lambda b, h: (b, 0, (2 * RET_QK + RET_V) // RET_DV + h)),
                pl.BlockSpec((None, 2, 128), lambda b, h: (h, 0, 0))]
    args = [proj3, proj3, proj3, proj3, lg]
    if rope:
        tab = pl.BlockSpec((seq_len, RET_DK), lambda b, h: (0, 0))
        in_specs += [tab, tab]
        args += list(rope_tabs)
    sspec = pl.BlockSpec((None, 2, None, RET_DK, RET_DV), lambda b, h: (b, 0, h, 0, 0))
    if has_init:
        in_specs.append(sspec)
        args.append(s0)
    out_specs = [pl.BlockSpec((None, seq_len, RET_DV), lambda b, h: (b, 0, h))]
    out_shape = [jax.ShapeDtypeStruct((nseq, seq_len, RET_V), BF16)]
    if want_state:
        out_specs.append(sspec)
        out_shape.append(jax.ShapeDtypeStruct((nseq, 2, RET_HEADS, RET_DK, RET_DV), F32))
    body = functools.partial(_ret_body, seq_len=seq_len, rope=rope, has_init=has_init, want_state=want_state)
    res = pl.pallas_call(
        body,
        grid=(nseq, RET_HEADS),
        in_specs=in_specs,
        out_specs=out_specs,
        out_shape=out_shape,
        scratch_shapes=[pltpu.VMEM((nblk, RET_DK, RET_DV), F32)],
        compiler_params=_params("parallel", "parallel", vmem=VMEM_LIMIT_BYTES),
        name="retention",
    )(*args)
    return res[0], (res[1] if want_state else None)


def rope_tables(seq_len):
    t = jnp.arange(seq_len)
    half = RET_DK // 2
    nfreq = half // 2
    inv = ROPE_BASE ** (-jnp.arange(nfreq, dtype=F32) / nfreq)
    parts_c, parts_s = [], []
    for p in (t // GRID_W, t % GRID_W):
        ang = p.astype(F32)[:, None] * inv[None, :]
        parts_c += [jnp.cos(ang), jnp.cos(ang)]
        parts_s += [-jnp.sin(ang), jnp.sin(ang)]
    return jnp.concatenate(parts_c, axis=-1), jnp.concatenate(parts_s, axis=-1)


def retention_layer(x, g, mod, rows_per_mod, nseq, seq_len, s0, grid_pos, w_in, lg, w_out, tm):
    proj = in_proj(x, g, mod, w_in, rows_per_mod, tm, D_MODEL)
    proj3 = proj.reshape(nseq, seq_len, 2 * RET_QK + 2 * RET_V)
    tabs = rope_tables(seq_len) if grid_pos else None
    y, states = retention_core(proj3, lg, tabs, s0, want_state=s0 is None)
    x_new = out_proj(y.reshape(nseq * seq_len, RET_V), w_out, x, mod, rows_per_mod, min(tm, OUT_ROWS))
    return x_new, states


def _mm_body(a_ref, b_ref, o_ref):
    o_ref[...] = jnp.dot(a_ref[...], b_ref[...], preferred_element_type=F32)


def matmul_f32(a, b, tm, tn):
    m, kd = a.shape
    n = b.shape[1]
    return pl.pallas_call(
        _mm_body,
        grid=(m // tm, n // tn),
        in_specs=[pl.BlockSpec((tm, kd), lambda i, j: (i, 0)),
                  pl.BlockSpec((kd, tn), lambda i, j: (0, j))],
        out_specs=pl.BlockSpec((tm, tn), lambda i, j: (i, j)),
        out_shape=jax.ShapeDtypeStruct((m, n), F32),
        compiler_params=_params("parallel", "parallel", vmem=VMEM_LIMIT_BYTES),
        name="dft_matmul",
    )(a, b)


def dft_matrices(seq_len):
    idx = jnp.arange(seq_len, dtype=jnp.int32)
    prod = (idx[:, None] * idx[None, :]) % (2 * seq_len)
    ang = prod.astype(F32) * (math.pi / seq_len)
    return jnp.cos(ang).astype(BF16), (-jnp.sin(ang)).astype(BF16)


def hyena_filter_spectrum(seq_len, cm, sm, w1, b1, w2, b2, w3):
    hp = lax.Precision.HIGHEST
    t = jnp.linspace(0.0, 1.0, seq_len, dtype=F32)[:, None]
    w = 2.0 * math.pi * jnp.arange(seq_len, dtype=F32)[:, None] / seq_len
    f = jnp.linspace(1e-4, HY_BANDS - 1.0, HY_BANDS, dtype=F32)[None, :]
    feat = jnp.concatenate([t, jnp.cos(f * w), -jnp.sin(f * w)], axis=-1)
    z = jnp.sin(jnp.dot(feat, w1, precision=hp) + b1)
    z = jnp.sin(jnp.dot(z, w2, precision=hp) + b2)
    filt = jnp.dot(z, w3, precision=hp).reshape(seq_len, 2, HY_ORDER, HY_WIDTH)
    max_decay = math.log(HY_DECAY_TARGET) / HY_SHORT_DECAY_PCT
    min_decay = math.log(HY_DECAY_TARGET) / HY_LONG_DECAY_PCT
    deltas = jnp.linspace(min_decay, max_decay, HY_WIDTH, dtype=F32)
    filt = filt * jnp.exp(-t * jnp.abs(deltas)[None, :])[:, None, None, :]
    filt = filt / (jnp.sum(jnp.abs(filt), axis=(0, 1), keepdims=True) + EPS)
    fwd = filt[:, 0].reshape(seq_len, HY_ORDER * HY_WIDTH)
    bwd = filt[:, 1].reshape(seq_len, HY_ORDER * HY_WIDTH)
    bwd = jnp.where(jnp.arange(seq_len)[:, None] == 0, 0.0, bwd)
    fsum, fdif = fwd + bwd, fwd - bwd
    tile = min(seq_len, 512)
    kr = matmul_f32(cm, fsum.astype(BF16), tile, 512)
    ki = matmul_f32(sm, fdif.astype(BF16), tile, 512)
    sign = jnp.where(jnp.arange(seq_len)[:, None] % 2 == 0, 1.0, -1.0)
    kn = jnp.sum(sign * fsum, axis=0, keepdims=True)
    wf = jnp.where(jnp.arange(seq_len)[:, None] == 0, 1.0, 2.0) / (2.0 * seq_len)
    return kr * wf, ki * wf, kn / (2.0 * seq_len)


def _hyena_body(v_ref, x1_ref, x2_ref, gate_ref, cw_ref, cb_ref, kr0_ref, ki0_ref, kr1_ref, ki1_ref,
                kn_ref, sk_ref, cm_ref, sm_ref, o_ref, z_scr, xg_scr, pre_scr, pim_scr, *, seq_len):
    row = lax.broadcasted_iota(jnp.int32, (seq_len, 1), 0)
    sign = jnp.where(row % 2 == 0, 1.0, -1.0).astype(F32)

    def short_conv(x_ref, which):
        x = x_ref[...].astype(F32)
        prev = jnp.where(row == 0, 0.0, pltpu.roll(x, 1, 0))
        nxt = jnp.where(row == seq_len - 1, 0.0, pltpu.roll(x, seq_len - 1, 0))
        return (cb_ref[which:which + 1, :] + prev * cw_ref[0, which:which + 1, :]
                + x * cw_ref[1, which:which + 1, :] + nxt * cw_ref[2, which:which + 1, :])

    rb = min(seq_len, HY_ROW_BLOCK)
    z_scr[...] = short_conv(v_ref, 0)
    for o, (xg_ref, kr_ref, ki_ref) in enumerate(((x1_ref, kr0_ref, ki0_ref), (x2_ref, kr1_ref, ki1_ref))):
        z = z_scr[...]
        zb = z.astype(BF16)
        ny = jnp.sum(z * sign, axis=0, keepdims=True) * kn_ref[o:o + 1, :]
        for r in range(0, seq_len, rb):
            z_re = jnp.dot(cm_ref[r:r + rb, :], zb, preferred_element_type=F32)
            z_im = jnp.dot(sm_ref[r:r + rb, :], zb, preferred_element_type=F32)
            kr = kr_ref[r:r + rb, :]
            ki = ki_ref[r:r + rb, :]
            pre_scr[r:r + rb, :] = (z_re * kr - z_im * ki).astype(BF16)
            pim_scr[r:r + rb, :] = (z_re * ki + z_im * kr).astype(BF16)
        xg_scr[...] = short_conv(xg_ref, o + 1)
        for r in range(0, seq_len, rb):
            zc = (jnp.dot(cm_ref[r:r + rb, :], pre_scr[...], preferred_element_type=F32)
                  + jnp.dot(sm_ref[r:r + rb, :], pim_scr[...], preferred_element_type=F32)
                  + sign[r:r + rb] * ny)
            z_scr[r:r + rb, :] = xg_scr[r:r + rb, :] * (zc + z_scr[r:r + rb, :] * sk_ref[o:o + 1, :])
    o_ref[...] = (z_scr[...] * _silu(gate_ref[...].astype(F32))).astype(o_ref.dtype)


def hyena_core(proj3, conv_w, conv_b, kr, ki, kn, skip, cm, sm, cb):
    nseq, seq_len, _ = proj3.shape
    nj = HY_WIDTH // cb
    once = pl.Buffered(1)

    def act(k):
        return pl.BlockSpec((None, seq_len, cb), lambda j, b: (b, 0, k * nj + j))

    def spec(k):
        return pl.BlockSpec((seq_len, cb), lambda j, b: (0, k * nj + j), pipeline_mode=once)

    dft = pl.BlockSpec((seq_len, seq_len), lambda j, b: (0, 0), pipeline_mode=once)
    body = functools.partial(_hyena_body, seq_len=seq_len)
    return pl.pallas_call(
        body,
        grid=(nj, nseq),
        in_specs=[act(0), act(1), act(2), act(3),
                  pl.BlockSpec((HY_SHORT, 3, cb), lambda j, b: (0, 0, j)),
                  pl.BlockSpec((3, cb), lambda j, b: (0, j)),
                  spec(0), spec(0), spec(1), spec(1),
                  pl.BlockSpec((HY_ORDER, cb), lambda j, b: (0, j)),
                  pl.BlockSpec((HY_ORDER, cb), lambda j, b: (0, j)),
                  dft, dft],
        out_specs=pl.BlockSpec((None, seq_len, cb), lambda j, b: (b, 0, j)),
        out_shape=jax.ShapeDtypeStruct((nseq, seq_len, HY_WIDTH), BF16),
        scratch_shapes=[pltpu.VMEM((seq_len, cb), F32), pltpu.VMEM((seq_len, cb), F32),
                        pltpu.VMEM((seq_len, cb), BF16), pltpu.VMEM((seq_len, cb), BF16)],
        compiler_params=_params("parallel", "arbitrary", vmem=VMEM_LIMIT_BYTES),
        name="hyena",
    )(proj3, proj3, proj3, proj3,
      conv_w.reshape(HY_SHORT, 3, HY_WIDTH), conv_b.reshape(3, HY_WIDTH),
      kr, ki, kr, ki, kn.reshape(HY_ORDER, HY_WIDTH), skip, cm, sm)


def hyena_layer(x, g, mod, rows_per_mod, nseq, seq_len, w_in, conv_w, conv_b, filt_w, skip, w_out, tm, cb):
    proj = in_proj(x, g, mod, w_in, rows_per_mod, tm, D_MODEL)
    proj3 = proj.reshape(nseq, seq_len, 4 * HY_WIDTH)
    cm, sm = dft_matrices(seq_len)
    kr, ki, kn = hyena_filter_spectrum(seq_len, cm, sm, *filt_w)
    y = hyena_core(proj3, conv_w, conv_b, kr, ki, kn, skip, cm, sm, cb)
    return out_proj(y.reshape(nseq * seq_len, HY_WIDTH), w_out, x, mod, rows_per_mod, min(tm, OUT_ROWS))


def kernel(x_prompt, x_sample, c, state_s5, state_ret, c_ctx, norm_g, mod_w, mod_b, s5_w_in, s5_lam_re, s5_lam_im, s5_log_step, s5_b_re, s5_b_im, s5_c_re, s5_c_im, s5_d, s5_w_glu, s5_b_glu, s5_w_out, ret_w_in, ret_decay_logit, ret_w_out, hy_w_in, hy_conv_w, hy_conv_b, hy_f_w1, hy_f_b1, hy_f_w2, hy_f_b2, hy_f_w3, hy_skip, hy_w_out, final_g):
    n_ctx, l_ctx, _ = x_prompt.shape
    n_dec, l_dec, _ = x_sample.shape
    xc = x_prompt.reshape(n_ctx * l_ctx, D_MODEL)
    xl = x_sample.reshape(n_dec * l_dec, D_MODEL)

    pad = (-(1 + n_dec)) % 8
    cvecs = jnp.concatenate([c_ctx[None, :], c, jnp.zeros((pad, D_MODEL), F32)], axis=0)
    mods = ada_mod_all(cvecs, mod_w, mod_b).reshape(DEPTH, -1, 3, D_MODEL)

    tm = 1024
    s5_new, ret_new = [], []
    for i in range(DEPTH):
        kind, j = i % N_MIXERS, i // N_MIXERS
        mod_c = mods[i, 0:1]
        mod_l = mods[i, 1:1 + n_dec]
        g = norm_g[i]
        if kind == 0:
            tables = s5_tables(s5_lam_re[j], s5_lam_im[j], s5_log_step[j], s5_b_re[j], s5_b_im[j],
                               s5_c_re[j], s5_c_im[j], s5_d[j])
            w = (s5_w_in[j].astype(BF16), tables, s5_w_glu[j].astype(BF16), s5_b_glu[j], s5_w_out[j].astype(BF16), tm)
            xc, st = s5_layer(xc, g, mod_c, n_ctx * l_ctx, n_ctx, l_ctx, None, *w)
            xl, _ = s5_layer(xl, g, mod_l, l_dec, n_dec, l_dec, state_s5[:, j], *w)
            s5_new.append(st)
        elif kind == 1:
            w_in = ret_w_in[j]
            kscale = jnp.concatenate([jnp.ones((RET_QK,), F32), jnp.full((RET_QK,), RET_DK ** -0.5, F32),
                                      jnp.ones((2 * RET_V,), F32)])
            w_in = (w_in * kscale[None, :]).astype(BF16)
            lg = jax.nn.log_sigmoid(ret_decay_logit[j])
            lg = jnp.broadcast_to(jnp.transpose(lg)[:, :, None], (RET_HEADS, 2, 128))
            w = (w_in, lg, ret_w_out[j].astype(BF16), tm)
            xc, st = retention_layer(xc, g, mod_c, n_ctx * l_ctx, n_ctx, l_ctx, None, False, *w)
            xl, _ = retention_layer(xl, g, mod_l, l_dec, n_dec, l_dec, state_ret[:, j], True, *w)
            ret_new.append(st)
        else:
            filt_w = (hy_f_w1[j], hy_f_b1[j], hy_f_w2[j], hy_f_b2[j], hy_f_w3[j])
            w = (hy_w_in[j].astype(BF16), hy_conv_w[j], hy_conv_b[j], filt_w, hy_skip[j], hy_w_out[j].astype(BF16), tm)
            xc = hyena_layer(xc, g, mod_c, n_ctx * l_ctx, n_ctx, l_ctx, *w, cb=1024)
            xl = hyena_layer(xl, g, mod_l, l_dec, n_dec, l_dec, *w, cb=256)

    y_prompt = final_norm(xc, final_g, tm).reshape(n_ctx, l_ctx, D_MODEL)
    y_sample = final_norm(xl, final_g, tm).reshape(n_dec, l_dec, D_MODEL)
    new_state_s5 = jnp.stack(s5_new, axis=1)
    new_state_ret = jnp.stack(ret_new, axis=1)
    return (y_prompt, y_sample, new_state_s5, new_state_ret)
```

```python
import functools
import math

import jax
import jax.numpy as jnp
from jax import lax
from jax.experimental import pallas as pl
from jax.experimental.pallas import tpu as pltpu

F32 = jnp.float32
BF16 = jnp.bfloat16

D_MODEL = 1024
DEPTH = 4
N_MIXERS = 3
EPS = 1e-6
GRID_W = 64

S5_GROUP = 16
S5_GROUPS = D_MODEL // S5_GROUP
S5_STATE = 64
S5_CHUNK = 16
S5_TILE = S5_CHUNK * S5_GROUP
S5_SLAB = 128 // S5_GROUP
S5_SEQS_PER_STEP = 4

RET_HEADS = 8
RET_QK = D_MODEL
RET_V = 2 * D_MODEL
RET_DK = RET_QK // RET_HEADS
RET_DV = RET_V // RET_HEADS
RET_BLOCK = 256
ROPE_BASE = 10000.0

HY_WIDTH = 2 * D_MODEL
HY_ORDER = 2
HY_SHORT = 3
HY_BANDS = 16
HY_SHORT_DECAY_PCT = 0.3
HY_LONG_DECAY_PCT = 1.5
HY_DECAY_TARGET = 1e-2
HY_ROW_BLOCK = 512

VMEM_LIMIT_BYTES = 56 * 1024 * 1024
OUT_ROWS = 512


def _params(*sem, vmem=None):
    return pltpu.CompilerParams(dimension_semantics=sem, vmem_limit_bytes=vmem)


def _silu(x):
    return x * jax.nn.sigmoid(x)


def _mod_body(c_ref, w_ref, b_ref, o_ref):
    a = _silu(c_ref[...]).astype(BF16)
    o_ref[0] = jnp.dot(a, w_ref[0].astype(BF16), preferred_element_type=F32) + b_ref[0]


def ada_mod_all(cvecs, mod_w, mod_b):
    r = cvecs.shape[0]
    tn = D_MODEL
    return pl.pallas_call(
        _mod_body,
        grid=(DEPTH, 3 * D_MODEL // tn),
        in_specs=[pl.BlockSpec((r, D_MODEL), lambda i, j: (0, 0)),
                  pl.BlockSpec((1, D_MODEL, tn), lambda i, j: (i, 0, j)),
                  pl.BlockSpec((1, 1, tn), lambda i, j: (i, 0, j))],
        out_specs=pl.BlockSpec((1, r, tn), lambda i, j: (i, 0, j)),
        out_shape=jax.ShapeDtypeStruct((DEPTH, r, 3 * D_MODEL), F32),
        compiler_params=_params("parallel", "parallel"),
        name="ada_mod",
    )(cvecs, mod_w, mod_b.reshape(DEPTH, 1, 3 * D_MODEL))


def _inproj_body(x_ref, g_ref, mod_ref, w_ref, o_ref, h_scr):
    @pl.when(pl.program_id(1) == 0)
    def _():
        x = x_ref[...]
        y = x * lax.rsqrt(jnp.mean(x * x, axis=-1, keepdims=True) + EPS) * g_ref[...]
        h_scr[...] = (y * (1.0 + mod_ref[1:2, :]) + mod_ref[0:1, :]).astype(BF16)

    o_ref[...] = jnp.dot(h_scr[...], w_ref[...], preferred_element_type=F32).astype(o_ref.dtype)


def in_proj(x, g, mod, w, rows_per_mod, tm, tn):
    n, nout = x.shape[0], w.shape[1]
    return pl.pallas_call(
        _inproj_body,
        grid=(n // tm, nout // tn),
        in_specs=[pl.BlockSpec((tm, D_MODEL), lambda i, j: (i, 0)),
                  pl.BlockSpec((1, D_MODEL), lambda i, j: (0, 0)),
                  pl.BlockSpec((None, 3, D_MODEL), lambda i, j: ((i * tm) // rows_per_mod, 0, 0)),
                  pl.BlockSpec((D_MODEL, tn), lambda i, j: (0, j))],
        out_specs=pl.BlockSpec((tm, tn), lambda i, j: (i, j)),
        out_shape=jax.ShapeDtypeStruct((n, nout), BF16),
        scratch_shapes=[pltpu.VMEM((tm, D_MODEL), BF16)],
        compiler_params=_params("parallel", "arbitrary", vmem=VMEM_LIMIT_BYTES),
        name="in_proj",
    )(x, g.reshape(1, D_MODEL), mod, w)


def _outproj_body(y_ref, w_ref, x_ref, mod_ref, o_ref):
    out = jnp.dot(y_ref[...], w_ref[...], preferred_element_type=F32)
    o_ref[...] = x_ref[...] + mod_ref[2:3, :] * out


def out_proj(y, w, x, mod, rows_per_mod, tm):
    n, width = y.shape
    return pl.pallas_call(
        _outproj_body,
        grid=(n // tm,),
        in_specs=[pl.BlockSpec((tm, width), lambda i: (i, 0)),
                  pl.BlockSpec((width, D_MODEL), lambda i: (0, 0)),
                  pl.BlockSpec((tm, D_MODEL), lambda i: (i, 0)),
                  pl.BlockSpec((None, 3, D_MODEL), lambda i: ((i * tm) // rows_per_mod, 0, 0))],
        out_specs=pl.BlockSpec((tm, D_MODEL), lambda i: (i, 0)),
        out_shape=jax.ShapeDtypeStruct((n, D_MODEL), F32),
        compiler_params=_params("parallel", vmem=VMEM_LIMIT_BYTES),
        name="out_proj",
    )(y, w, x, mod)


def _final_norm_body(x_ref, g_ref, o_ref):
    x = x_ref[...]
    o_ref[...] = x * lax.rsqrt(jnp.mean(x * x, axis=-1, keepdims=True) + EPS) * g_ref[...]


def final_norm(x, g, tm):
    n = x.shape[0]
    return pl.pallas_call(
        _final_norm_body,
        grid=(n // tm,),
        in_specs=[pl.BlockSpec((tm, D_MODEL), lambda i: (i, 0)),
                  pl.BlockSpec((1, D_MODEL), lambda i: (0, 0))],
        out_specs=pl.BlockSpec((tm, D_MODEL), lambda i: (i, 0)),
        out_shape=jax.ShapeDtypeStruct((n, D_MODEL), F32),
        compiler_params=_params("parallel"),
        name="final_norm",
    )(x, g.reshape(1, D_MODEL))


def s5_tables(lam_re, lam_im, log_step, b_re, b_im, c_re, c_im, d_skip):
    t_len = S5_CHUNK
    hp = lax.Precision.HIGHEST
    dt = jnp.exp(log_step)[..., None]
    ab_re = jnp.exp(lam_re * dt) * jnp.cos(lam_im * dt)
    ab_im = jnp.exp(lam_re * dt) * jnp.sin(lam_im * dt)
    den = lam_re * lam_re + lam_im * lam_im
    nr, ni = ab_re - 1.0, ab_im
    f_re = (nr * lam_re + ni * lam_im) / den
    f_im = (ni * lam_re - nr * lam_im) / den
    bb_re = f_re[..., None] * b_re - f_im[..., None] * b_im
    bb_im = f_re[..., None] * b_im + f_im[..., None] * b_re
    ks = jnp.arange(t_len + 1, dtype=F32)[:, None, None, None]
    pw_mag = jnp.exp(ks * (lam_re * dt)[None])
    pw_re = pw_mag * jnp.cos(ks * (lam_im * dt)[None])
    pw_im = pw_mag * jnp.sin(ks * (lam_im * dt)[None])

    ca_re = c_re[None] * pw_re[:, :, :, None, :] - c_im[None] * pw_im[:, :, :, None, :]
    ca_im = c_re[None] * pw_im[:, :, :, None, :] + c_im[None] * pw_re[:, :, :, None, :]
    kk = (jnp.einsum('kdgop,dgph->kdgoh', ca_re[:t_len], bb_re, precision=hp)
          - jnp.einsum('kdgop,dgph->kdgoh', ca_im[:t_len], bb_im, precision=hp))
    lag0 = kk[0, 0] + kk[0, 1] + jnp.eye(S5_GROUP, dtype=F32) * d_skip.reshape(S5_GROUPS, S5_GROUP, 1)
    v = jnp.concatenate([kk[:0:-1, 1], lag0[None], kk[1:, 0]], axis=0)
    v = jnp.transpose(v, (1, 3, 0, 2)).astype(BF16)
    w = jnp.concatenate([v, jnp.zeros((S5_GROUPS, S5_GROUP, 1, S5_GROUP), BF16)], axis=2)
    w = jnp.tile(w, (1, 1, t_len, 1))[:, :, :t_len * (2 * t_len - 1)]
    w = w.reshape(S5_GROUPS, S5_GROUP, t_len, 2 * t_len - 1, S5_GROUP)[:, :, :, t_len - 1:]
    m = jnp.transpose(w, (0, 2, 1, 3, 4)).reshape(S5_GROUPS, S5_TILE, S5_TILE)

    npair = S5_GROUPS // 2
    pair_eye = jnp.eye(2, dtype=F32)
    sidx = jnp.arange(t_len)

    def _f(d, expo):
        ar, ai = pw_re[expo, d], pw_im[expo, d]
        fr = ar[..., None] * bb_re[d][None] - ai[..., None] * bb_im[d][None]
        fi = ar[..., None] * bb_im[d][None] + ai[..., None] * bb_re[d][None]
        to = lambda a: jnp.transpose(a, (1, 0, 3, 2)).reshape(S5_GROUPS, S5_TILE, S5_STATE)
        return to(fr), to(fi)
    ff_re, ff_im = _f(0, t_len - 1 - sidx)
    fb_re, fb_im = _f(1, sidx)
    fq = jnp.stack([ff_re, ff_im, fb_re, fb_im], axis=2)
    fq = fq.reshape(npair, 2, S5_TILE, 4, S5_STATE)
    f = jnp.einsum('kgrqp,gj->kgrqjp', fq, pair_eye).reshape(npair, 2 * S5_TILE, 8 * S5_STATE)

    def _e(d, expo):
        er = ca_re[expo, d]
        ei = ca_im[expo, d]
        to = lambda a: jnp.transpose(a, (1, 3, 0, 2)).reshape(S5_GROUPS, S5_STATE, S5_TILE)
        return to(er), to(-ei)
    ef_re, ef_im = _e(0, sidx + 1)
    eb_re, eb_im = _e(1, t_len - sidx)
    eq = jnp.stack([ef_re, ef_im, eb_re, eb_im], axis=1)
    eq = eq.reshape(npair, 2, 4, S5_STATE, S5_TILE)
    e = jnp.einsum('kgqpc,gj->kqjpgc', eq, pair_eye).reshape(npair, 8 * S5_STATE, 2 * S5_TILE)

    dec = jnp.stack([pw_re[t_len, 0], pw_im[t_len, 0], pw_re[t_len, 1], pw_im[t_len, 1]], axis=0)
    dec = jnp.transpose(dec.reshape(4, S5_GROUPS // S5_SLAB, S5_SLAB * S5_STATE), (1, 0, 2))
    return m.astype(BF16), f.astype(BF16), e.astype(BF16), dec


def s5_lane_permutation():
    width = S5_CHUNK * S5_SLAB * S5_GROUP
    i = jnp.arange(width)
    step, grp, ch = i // (S5_SLAB * S5_GROUP), (i // S5_GROUP) % S5_SLAB, i % S5_GROUP
    dst = grp * S5_TILE + step * S5_GROUP + ch
    return (dst[:, None] == jnp.arange(width)[None, :]).astype(BF16)


def _s5_scan_body(u_ref, q_ref, m_ref, f_ref, e_ref, dec_ref, h0_ref, y_ref, hf_ref,
                  uy_s, st_s, *, nchunk, nseq):
    npair = S5_SLAB // 2
    pair_w = 2 * S5_TILE
    lane_w = 2 * S5_STATE
    nt_dims = (((1,), (1,)), ((), ()))
    xcat = jnp.concatenate([u_ref[s] for s in range(S5_CHUNK)], axis=1)
    for k in range(npair):
        cols = slice(k * pair_w, (k + 1) * pair_w)
        up = jnp.dot(xcat, q_ref[:, cols], preferred_element_type=F32).astype(BF16)
        uy_s[:, cols] = up
        x = jnp.dot(up, f_ref[k], preferred_element_type=F32)
        for q in range(4):
            st_s[q * npair + k] = x[:, q * lane_w:(q + 1) * lane_w]

    def lanes_of(ref, q, k):
        return ref[q, :, k * lane_w:(k + 1) * lane_w]

    dec = [[dec_ref[q:q + 1, k * lane_w:(k + 1) * lane_w] for k in range(npair)] for q in range(4)]

    def step(i, carry):
        rows_f = pl.ds(i, nseq, stride=nchunk)
        rows_b = pl.ds(nchunk - 1 - i, nseq, stride=nchunk)
        new = []
        for k in range(npair):
            fr, fi, br, bi = (carry[q * npair + k] for q in range(4))
            xr, xi = st_s[0 * npair + k, rows_f, :], st_s[1 * npair + k, rows_f, :]
            yr, yi = st_s[2 * npair + k, rows_b, :], st_s[3 * npair + k, rows_b, :]
            st_s[0 * npair + k, rows_f, :] = fr
            st_s[1 * npair + k, rows_f, :] = fi
            st_s[2 * npair + k, rows_b, :] = br
            st_s[3 * npair + k, rows_b, :] = bi
            new.append((dec[0][k] * fr - dec[1][k] * fi + xr, dec[0][k] * fi + dec[1][k] * fr + xi,
                        dec[2][k] * br - dec[3][k] * bi + yr, dec[2][k] * bi + dec[3][k] * br + yi))
        return tuple(new[k][q] for q in range(4) for k in range(npair))

    init = tuple(lanes_of(h0_ref, q, k) for q in range(4) for k in range(npair))
    final = lax.fori_loop(0, nchunk, step, init)
    for q in range(4):
        for k in range(npair):
            hf_ref[q, :, k * lane_w:(k + 1) * lane_w] = final[q * npair + k]

    for k in range(npair):
        h = jnp.concatenate([st_s[q * npair + k] for q in range(4)], axis=1)
        ye = jnp.dot(h.astype(BF16), e_ref[k], preferred_element_type=F32)
        for g in range(2):
            cols = slice(k * pair_w + g * S5_TILE, k * pair_w + (g + 1) * S5_TILE)
            y = jnp.dot(uy_s[:, cols], m_ref[2 * k + g], preferred_element_type=F32)
            uy_s[:, cols] = (y + ye[:, g * S5_TILE:(g + 1) * S5_TILE]).astype(BF16)
    slab_w = S5_SLAB * S5_GROUP
    for t in range(S5_CHUNK):
        y_ref[t] = lax.dot_general(uy_s[...], q_ref[t * slab_w:(t + 1) * slab_w, :], nt_dims,
                                   preferred_element_type=F32).astype(y_ref.dtype)


def s5_scan(u_s, perm, m, f, e, dec, h0, nchunk, nseq, nb):
    rows = nchunk * nb
    nslab = S5_GROUPS // S5_SLAB
    slab_w = S5_SLAB * S5_GROUP
    state_w = S5_SLAB * S5_STATE
    width = S5_CHUNK * slab_w
    body = functools.partial(_s5_scan_body, nchunk=nchunk, nseq=nb)
    hspec = pl.BlockSpec((None, None, 4, nb, state_w), lambda s, b: (s, b, 0, 0, 0))
    state = pltpu.VMEM((2 * S5_SLAB, rows, 2 * S5_STATE), F32)
    return pl.pallas_call(
        body,
        grid=(nslab, nseq // nb),
        in_specs=[pl.BlockSpec((S5_CHUNK, rows, slab_w), lambda s, b: (0, b, s)),
                  pl.BlockSpec((width, width), lambda s, b: (0, 0), pipeline_mode=pl.Buffered(1)),
                  pl.BlockSpec((S5_SLAB, S5_TILE, S5_TILE), lambda s, b: (s, 0, 0)),
                  pl.BlockSpec((S5_SLAB // 2, 2 * S5_TILE, 2 * S5_TILE), lambda s, b: (s, 0, 0)),
                  pl.BlockSpec((S5_SLAB // 2, 2 * S5_TILE, 2 * S5_TILE), lambda s, b: (s, 0, 0)),
                  pl.BlockSpec((None, 4, state_w), lambda s, b: (s, 0, 0)),
                  hspec],
        out_specs=[pl.BlockSpec((S5_CHUNK, rows, slab_w), lambda s, b: (0, b, s)), hspec],
        out_shape=[jax.ShapeDtypeStruct((S5_CHUNK, nseq * nchunk, D_MODEL), BF16),
                   jax.ShapeDtypeStruct((nslab, nseq // nb, 4, nb, state_w), F32)],
        scratch_shapes=[pltpu.VMEM((rows, width), BF16), state],
        compiler_params=_params("parallel", "parallel", vmem=VMEM_LIMIT_BYTES),
        name="s5_scan",
    )(u_s, perm, m, f, e, dec, h0)


def _inproj_steps_body(x_ref, g_ref, mod_ref, w_ref, o_ref):
    x = x_ref[...]
    y = x * lax.rsqrt(jnp.mean(x * x, axis=-1, keepdims=True) + EPS) * g_ref[...]
    h = y * (1.0 + mod_ref[:, 1:2, :]) + mod_ref[:, 0:1, :]
    h = h.reshape(x.shape[0] * x.shape[1], D_MODEL).astype(BF16)
    o_ref[...] = jnp.dot(h, w_ref[...], preferred_element_type=F32).astype(o_ref.dtype)


def _mod_spec(mod, nb):
    if mod.shape[0] == 1:
        return pl.BlockSpec((1, 3, D_MODEL), lambda b, s: (0, 0, 0))
    return pl.BlockSpec((nb, 3, D_MODEL), lambda b, s: (b, 0, 0))


def in_proj_steps(x4, g, mod, w, nb):
    nseq, nchunk, _ = x4.shape
    nout = w.shape[1]
    return pl.pallas_call(
        _inproj_steps_body,
        grid=(nseq // nb, S5_CHUNK),
        in_specs=[pl.BlockSpec((nb, nchunk, D_MODEL), lambda b, s: (b, 0, s)),
                  pl.BlockSpec((1, D_MODEL), lambda b, s: (0, 0)),
                  _mod_spec(mod, nb),
                  pl.BlockSpec((D_MODEL, nout), lambda b, s: (0, 0))],
        out_specs=pl.BlockSpec((None, nb * nchunk, nout), lambda b, s: (s, b, 0)),
        out_shape=jax.ShapeDtypeStruct((S5_CHUNK, nseq * nchunk, nout), BF16),
        compiler_params=_params("parallel", "parallel", vmem=VMEM_LIMIT_BYTES),
        name="in_proj_steps",
    )(x4, g.reshape(1, D_MODEL), mod, w)


def _s5_out_body(y_ref, gate_ref, wg_ref, bg_ref, wo_ref, x_ref, mod_ref, o_ref):
    g = jax.nn.gelu(y_ref[...].astype(F32))
    t = jnp.dot(g.astype(BF16), wg_ref[...], preferred_element_type=F32) + bg_ref[...]
    z = g * jax.nn.sigmoid(t) * _silu(gate_ref[...].astype(F32))
    out = jnp.dot(z.astype(BF16), wo_ref[...], preferred_element_type=F32)
    o_ref[...] = x_ref[...] + mod_ref[:, 2:3, :] * out.reshape(x_ref.shape)


def s5_out(y_s, u_s, w_glu, b_glu, w_out, x4, mod, nb):
    nseq, nchunk, _ = x4.shape
    rows = nb * nchunk
    xspec = pl.BlockSpec((nb, nchunk, D_MODEL), lambda b, s: (b, 0, s))
    wspec = pl.BlockSpec((D_MODEL, D_MODEL), lambda b, s: (0, 0))
    return pl.pallas_call(
        _s5_out_body,
        grid=(nseq // nb, S5_CHUNK),
        in_specs=[pl.BlockSpec((None, rows, D_MODEL), lambda b, s: (s, b, 0)),
                  pl.BlockSpec((None, rows, D_MODEL), lambda b, s: (s, b, 1)),
                  wspec,
                  pl.BlockSpec((1, D_MODEL), lambda b, s: (0, 0)),
                  wspec,
                  xspec,
                  _mod_spec(mod, nb)],
        out_specs=xspec,
        out_shape=jax.ShapeDtypeStruct(x4.shape, F32),
        compiler_params=_params("parallel", "parallel", vmem=VMEM_LIMIT_BYTES),
        name="s5_out",
    )(y_s, u_s, w_glu, b_glu.reshape(1, D_MODEL), w_out, x4, mod)


def s5_layer(x, g, mod, nseq, seq_len, h0, w_in, tables, perm, w_glu, b_glu, w_out, nb):
    nchunk = seq_len // S5_CHUNK
    nslab = S5_GROUPS // S5_SLAB
    state_w = S5_SLAB * S5_STATE
    x4 = x.reshape(nseq, nchunk, S5_CHUNK * D_MODEL)
    u_s = in_proj_steps(x4, g, mod, w_in, nb)
    if h0 is None:
        h0_p = jnp.zeros((nslab, nseq // nb, 4, nb, state_w), F32)
    else:
        h0_p = jnp.transpose(h0.reshape(nseq // nb, nb, 4, nslab, state_w), (3, 0, 2, 1, 4))
    m, f, e, dec = tables
    y_s, hf = s5_scan(u_s, perm, m, f, e, dec, h0_p, nchunk, nseq, nb)
    x_new = s5_out(y_s, u_s, w_glu, b_glu, w_out, x4, mod, nb).reshape(nseq * seq_len, D_MODEL)
    states = jnp.transpose(hf, (1, 3, 2, 0, 4)).reshape(nseq, 2, 2, S5_GROUPS, S5_STATE)
    return x_new, states


def _ret_body(*refs, seq_len, rope, has_init, want_state):
    refs = list(refs)
    q_ref, k_ref, v_ref, gate_ref, lg_ref = refs[:5]
    pos = 5
    if rope:
        cos_ref, sin_ref = refs[pos:pos + 2]
        pos += 2
    if has_init:
        s0_ref = refs[pos]
        pos += 1
    o_ref = refs[pos]
    pos += 1
    if want_state:
        sfin_ref = refs[pos]
        pos += 1
    sf_scr = refs[pos]

    c_len = RET_BLOCK
    nblk = seq_len // c_len
    lgf = lg_ref[0:1, 0:1]
    lgb = lg_ref[1:2, 0:1]
    ii = lax.broadcasted_iota(jnp.int32, (c_len, c_len), 0)
    jj = lax.broadcasted_iota(jnp.int32, (c_len, c_len), 1)
    diff = (ii - jj).astype(F32)
    decay = jnp.exp(jnp.abs(diff) * jnp.where(diff >= 0, lgf, lgb))
    ic = lax.broadcasted_iota(jnp.int32, (c_len, 1), 0).astype(F32)
    q_dec_f = jnp.exp((ic + 1.0) * lgf)
    q_dec_b = jnp.exp((c_len - ic) * lgb)
    k_dec_f = jnp.exp((c_len - 1.0 - ic) * lgf)
    k_dec_b = jnp.exp(ic * lgb)
    c_dec_f = jnp.exp(c_len * lgf)
    c_dec_b = jnp.exp(c_len * lgb)

    q = q_ref[...].astype(F32)
    k = k_ref[...].astype(F32)
    if rope:
        lane = lax.broadcasted_iota(jnp.int32, (seq_len, RET_DK), 1)
        first = (lane % (RET_DK // 2)) < (RET_DK // 4)
        cos = cos_ref[...]
        sin = sin_ref[...]

        def rot(x):
            swapped = jnp.where(first, pltpu.roll(x, RET_DK - RET_DK // 4, 1), pltpu.roll(x, RET_DK // 4, 1))
            return x * cos + swapped * sin

        q = rot(q)
        k = rot(k)

    def blk(a, c):
        return a[c * c_len:(c + 1) * c_len]

    def kv_state(kd, c):
        kt = jnp.transpose(blk(k, c) * kd).astype(BF16)
        return jnp.dot(kt, v_ref[c * c_len:(c + 1) * c_len, :], preferred_element_type=F32)

    use_state = has_init or nblk > 1
    s_f = s0_ref[0] if has_init else jnp.zeros((RET_DK, RET_DV), F32)
    for c in range(nblk):
        if use_state:
            sf_scr[c] = s_f
        if want_state or c < nblk - 1:
            s_f = c_dec_f * s_f + kv_state(k_dec_f, c)
    if want_state:
        sfin_ref[0] = s_f

    s_b = s0_ref[1] if has_init else jnp.zeros((RET_DK, RET_DV), F32)
    for c in range(nblk - 1, -1, -1):
        qc = blk(q, c)
        kc = blk(k, c)
        vc = v_ref[c * c_len:(c + 1) * c_len, :]
        s = lax.dot_general(qc.astype(BF16), kc.astype(BF16), (((1,), (1,)), ((), ())),
                            preferred_element_type=F32)
        o = jnp.dot((s * decay).astype(BF16), vc, preferred_element_type=F32)
        if use_state:
            o = o + jnp.dot((qc * q_dec_f).astype(BF16), sf_scr[c].astype(BF16), preferred_element_type=F32)
            o = o + jnp.dot((qc * q_dec_b).astype(BF16), s_b.astype(BF16), preferred_element_type=F32)
        o = o * lax.rsqrt(jnp.mean(o * o, axis=-1, keepdims=True) + EPS)
        gate = gate_ref[c * c_len:(c + 1) * c_len, :].astype(F32)
        o_ref[c * c_len:(c + 1) * c_len, :] = (o * _silu(gate)).astype(o_ref.dtype)
        if want_state or c > 0:
            s_b = c_dec_b * s_b + kv_state(k_dec_b, c)
    if want_state:
        sfin_ref[1] = s_b


def retention_core(proj3, lg, rope_tabs, s0, want_state):
    nseq, seq_len, _ = proj3.shape
    nblk = seq_len // RET_BLOCK
    rope = rope_tabs is not None
    has_init = s0 is not None
    kq = RET_QK // RET_DK
    in_specs = [pl.BlockSpec((None, seq_len, RET_DK), lambda b, h: (b, 0, h)),
                pl.BlockSpec((None, seq_len, RET_DK), lambda b, h: (b, 0, kq + h)),
                pl.BlockSpec((None, seq_len, RET_DV), lambda b, h: (b, 0, 2 * RET_QK // RET_DV + h)),
                pl.BlockSpec((None, seq_len, RET_DV), lambda b, h: (b, 0, (2 * RET_QK + RET_V) // RET_DV + h)),
                pl.BlockSpec((None, 2, 128), lambda b, h: (h, 0, 0))]
    args = [proj3, proj3, proj3, proj3, lg]
    if rope:
        tab = pl.BlockSpec((seq_len, RET_DK), lambda b, h: (0, 0))
        in_specs += [tab, tab]
        args += list(rope_tabs)
    sspec = pl.BlockSpec((None, 2, None, RET_DK, RET_DV), lambda b, h: (b, 0, h, 0, 0))
    if has_init:
        in_specs.append(sspec)
        args.append(s0)
    out_specs = [pl.BlockSpec((None, seq_len, RET_DV), lambda b, h: (b, 0, h))]
    out_shape = [jax.ShapeDtypeStruct((nseq, seq_len, RET_V), BF16)]
    if want_state:
        out_specs.append(sspec)
        out_shape.append(jax.ShapeDtypeStruct((nseq, 2, RET_HEADS, RET_DK, RET_DV), F32))
    body = functools.partial(_ret_body, seq_len=seq_len, rope=rope, has_init=has_init, want_state=want_state)
    res = pl.pallas_call(
        body,
        grid=(nseq, RET_HEADS),
        in_specs=in_specs,
        out_specs=out_specs,
        out_shape=out_shape,
        scratch_shapes=[pltpu.VMEM((nblk, RET_DK, RET_DV), F32)],
        compiler_params=_params("parallel", "parallel", vmem=VMEM_LIMIT_BYTES),
        name="retention",
    )(*args)
    return res[0], (res[1] if want_state else None)


def rope_tables(seq_len):
    t = jnp.arange(seq_len)
    half = RET_DK // 2
    nfreq = half // 2
    inv = ROPE_BASE ** (-jnp.arange(nfreq, dtype=F32) / nfreq)
    parts_c, parts_s = [], []
    for p in (t // GRID_W, t % GRID_W):
        ang = p.astype(F32)[:, None] * inv[None, :]
        parts_c += [jnp.cos(ang), jnp.cos(ang)]
        parts_s += [-jnp.sin(ang), jnp.sin(ang)]
    return jnp.concatenate(parts_c, axis=-1), jnp.concatenate(parts_s, axis=-1)


def retention_layer(x, g, mod, rows_per_mod, nseq, seq_len, s0, grid_pos, w_in, lg, w_out, tm):
    proj = in_proj(x, g, mod, w_in, rows_per_mod, tm, D_MODEL)
    proj3 = proj.reshape(nseq, seq_len, 2 * RET_QK + 2 * RET_V)
    tabs = rope_tables(seq_len) if grid_pos else None
    y, states = retention_core(proj3, lg, tabs, s0, want_state=s0 is None)
    x_new = out_proj(y.reshape(nseq * seq_len, RET_V), w_out, x, mod, rows_per_mod, min(tm, OUT_ROWS))
    return x_new, states


def _mm_body(a_ref, b_ref, o_ref):
    o_ref[...] = jnp.dot(a_ref[...], b_ref[...], preferred_element_type=F32)


def matmul_f32(a, b, tm, tn):
    m, kd = a.shape
    n = b.shape[1]
    return pl.pallas_call(
        _mm_body,
        grid=(m // tm, n // tn),
        in_specs=[pl.BlockSpec((tm, kd), lambda i, j: (i, 0)),
                  pl.BlockSpec((kd, tn), lambda i, j: (0, j))],
        out_specs=pl.BlockSpec((tm, tn), lambda i, j: (i, j)),
        out_shape=jax.ShapeDtypeStruct((m, n), F32),
        compiler_params=_params("parallel", "parallel", vmem=VMEM_LIMIT_BYTES),
        name="dft_matmul",
    )(a, b)


def dft_matrices(seq_len):
    idx = jnp.arange(seq_len, dtype=jnp.int32)
    prod = (idx[:, None] * idx[None, :]) % (2 * seq_len)
    ang = prod.astype(F32) * (math.pi / seq_len)
    return jnp.cos(ang).astype(BF16), (-jnp.sin(ang)).astype(BF16)


def hyena_filter_spectrum(seq_len, cm, sm, w1, b1, w2, b2, w3):
    hp = lax.Precision.HIGHEST
    t = jnp.linspace(0.0, 1.0, seq_len, dtype=F32)[:, None]
    w = 2.0 * math.pi * jnp.arange(seq_len, dtype=F32)[:, None] / seq_len
    f = jnp.linspace(1e-4, HY_BANDS - 1.0, HY_BANDS, dtype=F32)[None, :]
    feat = jnp.concatenate([t, jnp.cos(f * w), -jnp.sin(f * w)], axis=-1)
    z = jnp.sin(jnp.dot(feat, w1, precision=hp) + b1)
    z = jnp.sin(jnp.dot(z, w2, precision=hp) + b2)
    filt = jnp.dot(z, w3, precision=hp).reshape(seq_len, 2, HY_ORDER, HY_WIDTH)
    max_decay = math.log(HY_DECAY_TARGET) / HY_SHORT_DECAY_PCT
    min_decay = math.log(HY_DECAY_TARGET) / HY_LONG_DECAY_PCT
    deltas = jnp.linspace(min_decay, max_decay, HY_WIDTH, dtype=F32)
    filt = filt * jnp.exp(-t * jnp.abs(deltas)[None, :])[:, None, None, :]
    filt = filt / (jnp.sum(jnp.abs(filt), axis=(0, 1), keepdims=True) + EPS)
    fwd = filt[:, 0].reshape(seq_len, HY_ORDER * HY_WIDTH)
    bwd = filt[:, 1].reshape(seq_len, HY_ORDER * HY_WIDTH)
    bwd = jnp.where(jnp.arange(seq_len)[:, None] == 0, 0.0, bwd)
    fsum, fdif = fwd + bwd, fwd - bwd
    tile = min(seq_len, 512)
    kr = matmul_f32(cm, fsum.astype(BF16), tile, 512)
    ki = matmul_f32(sm, fdif.astype(BF16), tile, 512)
    sign = jnp.where(jnp.arange(seq_len)[:, None] % 2 == 0, 1.0, -1.0)
    kn = jnp.sum(sign * fsum, axis=0, keepdims=True)
    wf = jnp.where(jnp.arange(seq_len)[:, None] == 0, 1.0, 2.0) / (2.0 * seq_len)
    return kr * wf, ki * wf, kn / (2.0 * seq_len)


def _hyena_body(v_ref, x1_ref, x2_ref, gate_ref, cw_ref, cb_ref, kr0_ref, ki0_ref, kr1_ref, ki1_ref,
                kn_ref, sk_ref, cm_ref, sm_ref, o_ref, z_scr, xg_scr, pre_scr, pim_scr, *, seq_len):
    row = lax.broadcasted_iota(jnp.int32, (seq_len, 1), 0)
    sign = jnp.where(row % 2 == 0, 1.0, -1.0).astype(F32)

    def short_conv(x_ref, which):
        x = x_ref[...].astype(F32)
        prev = jnp.where(row == 0, 0.0, pltpu.roll(x, 1, 0))
        nxt = jnp.where(row == seq_len - 1, 0.0, pltpu.roll(x, seq_len - 1, 0))
        return (cb_ref[which:which + 1, :] + prev * cw_ref[0, which:which + 1, :]
                + x * cw_ref[1, which:which + 1, :] + nxt * cw_ref[2, which:which + 1, :])

    rb = min(seq_len, HY_ROW_BLOCK)
    z_scr[...] = short_conv(v_ref, 0)
    for o, (xg_ref, kr_ref, ki_ref) in enumerate(((x1_ref, kr0_ref, ki0_ref), (x2_ref, kr1_ref, ki1_ref))):
        z = z_scr[...]
        zb = z.astype(BF16)
        ny = jnp.sum(z * sign, axis=0, keepdims=True) * kn_ref[o:o + 1, :]
        for r in range(0, seq_len, rb):
            z_re = jnp.dot(cm_ref[r:r + rb, :], zb, preferred_element_type=F32)
            z_im = jnp.dot(sm_ref[r:r + rb, :], zb, preferred_element_type=F32)
            kr = kr_ref[r:r + rb, :]
            ki = ki_ref[r:r + rb, :]
            pre_scr[r:r + rb, :] = (z_re * kr - z_im * ki).astype(BF16)
            pim_scr[r:r + rb, :] = (z_re * ki + z_im * kr).astype(BF16)
        xg_scr[...] = short_conv(xg_ref, o + 1)
        for r in range(0, seq_len, rb):
            zc = (jnp.dot(cm_ref[r:r + rb, :], pre_scr[...], preferred_element_type=F32)
                  + jnp.dot(sm_ref[r:r + rb, :], pim_scr[...], preferred_element_type=F32)
                  + sign[r:r + rb] * ny)
            z_scr[r:r + rb, :] = xg_scr[r:r + rb, :] * (zc + z_scr[r:r + rb, :] * sk_ref[o:o + 1, :])
    o_ref[...] = (z_scr[...] * _silu(gate_ref[...].astype(F32))).astype(o_ref.dtype)


def hyena_core(proj3, conv_w, conv_b, kr, ki, kn, skip, cm, sm, cb):
    nseq, seq_len, _ = proj3.shape
    nj = HY_WIDTH // cb
    once = pl.Buffered(1)

    def act(k):
        return pl.BlockSpec((None, seq_len, cb), lambda j, b: (b, 0, k * nj + j))

    def spec(k):
        return pl.BlockSpec((seq_len, cb), lambda j, b: (0, k * nj + j), pipeline_mode=once)

    dft = pl.BlockSpec((seq_len, seq_len), lambda j, b: (0, 0), pipeline_mode=once)
    body = functools.partial(_hyena_body, seq_len=seq_len)
    return pl.pallas_call(
        body,
        grid=(nj, nseq),
        in_specs=[act(0), act(1), act(2), act(3),
                  pl.BlockSpec((HY_SHORT, 3, cb), lambda j, b: (0, 0, j)),
                  pl.BlockSpec((3, cb), lambda j, b: (0, j)),
                  spec(0), spec(0), spec(1), spec(1),
                  pl.BlockSpec((HY_ORDER, cb), lambda j, b: (0, j)),
                  pl.BlockSpec((HY_ORDER, cb), lambda j, b: (0, j)),
                  dft, dft],
        out_specs=pl.BlockSpec((None, seq_len, cb), lambda j, b: (b, 0, j)),
        out_shape=jax.ShapeDtypeStruct((nseq, seq_len, HY_WIDTH), BF16),
        scratch_shapes=[pltpu.VMEM((seq_len, cb), F32), pltpu.VMEM((seq_len, cb), F32),
                        pltpu.VMEM((seq_len, cb), BF16), pltpu.VMEM((seq_len, cb), BF16)],
        compiler_params=_params("parallel", "arbitrary", vmem=VMEM_LIMIT_BYTES),
        name="hyena",
    )(proj3, proj3, proj3, proj3,
      conv_w.reshape(HY_SHORT, 3, HY_WIDTH), conv_b.reshape(3, HY_WIDTH),
      kr, ki, kr, ki, kn.reshape(HY_ORDER, HY_WIDTH), skip, cm, sm)


def hyena_layer(x, g, mod, rows_per_mod, nseq, seq_len, w_in, conv_w, conv_b, filt_w, skip, w_out, tm, cb):
    proj = in_proj(x, g, mod, w_in, rows_per_mod, tm, D_MODEL)
    proj3 = proj.reshape(nseq, seq_len, 4 * HY_WIDTH)
    cm, sm = dft_matrices(seq_len)
    kr, ki, kn = hyena_filter_spectrum(seq_len, cm, sm, *filt_w)
    y = hyena_core(proj3, conv_w, conv_b, kr, ki, kn, skip, cm, sm, cb)
    return out_proj(y.reshape(nseq * seq_len, HY_WIDTH), w_out, x, mod, rows_per_mod, min(tm, OUT_ROWS))


def kernel(x_prompt, x_sample, c, state_s5, state_ret, c_ctx, norm_g, mod_w, mod_b, s5_w_in, s5_lam_re, s5_lam_im, s5_log_step, s5_b_re, s5_b_im, s5_c_re, s5_c_im, s5_d, s5_w_glu, s5_b_glu, s5_w_out, ret_w_in, ret_decay_logit, ret_w_out, hy_w_in, hy_conv_w, hy_conv_b, hy_f_w1, hy_f_b1, hy_f_w2, hy_f_b2, hy_f_w3, hy_skip, hy_w_out, final_g):
    n_ctx, l_ctx, _ = x_prompt.shape
    n_dec, l_dec, _ = x_sample.shape
    xc = x_prompt.reshape(n_ctx * l_ctx, D_MODEL)
    xl = x_sample.reshape(n_dec * l_dec, D_MODEL)

    pad = (-(1 + n_dec)) % 8
    cvecs = jnp.concatenate([c_ctx[None, :], c, jnp.zeros((pad, D_MODEL), F32)], axis=0)
    mods = ada_mod_all(cvecs, mod_w, mod_b).reshape(DEPTH, -1, 3, D_MODEL)

    tm = 1024
    perm = s5_lane_permutation()
    s5_new, ret_new = [], []
    for i in range(DEPTH):
        kind, j = i % N_MIXERS, i // N_MIXERS
        mod_c = mods[i, 0:1]
        mod_l = mods[i, 1:1 + n_dec]
        g = norm_g[i]
        if kind == 0:
            tables = s5_tables(s5_lam_re[j], s5_lam_im[j], s5_log_step[j], s5_b_re[j], s5_b_im[j],
                               s5_c_re[j], s5_c_im[j], s5_d[j])
            w = (s5_w_in[j].astype(BF16), tables, perm, s5_w_glu[j].astype(BF16), s5_b_glu[j],
                 s5_w_out[j].astype(BF16))
            xc, st = s5_layer(xc, g, mod_c, n_ctx, l_ctx, None, *w, nb=n_ctx)
            xl, _ = s5_layer(xl, g, mod_l, n_dec, l_dec, state_s5[:, j], *w, nb=S5_SEQS_PER_STEP)
            s5_new.append(st)
        elif kind == 1:
            w_in = ret_w_in[j]
            kscale = jnp.concatenate([jnp.ones((RET_QK,), F32), jnp.full((RET_QK,), RET_DK ** -0.5, F32),
                                      jnp.ones((2 * RET_V,), F32)])
            w_in = (w_in * kscale[None, :]).astype(BF16)
            lg = jax.nn.log_sigmoid(ret_decay_logit[j])
            lg = jnp.broadcast_to(jnp.transpose(lg)[:, :, None], (RET_HEADS, 2, 128))
            w = (w_in, lg, ret_w_out[j].astype(BF16), tm)
            xc, st = retention_layer(xc, g, mod_c, n_ctx * l_ctx, n_ctx, l_ctx, None, False, *w)
            xl, _ = retention_layer(xl, g, mod_l, l_dec, n_dec, l_dec, state_ret[:, j], True, *w)
            ret_new.append(st)
        else:
            filt_w = (hy_f_w1[j], hy_f_b1[j], hy_f_w2[j], hy_f_b2[j], hy_f_w3[j])
            w = (hy_w_in[j].astype(BF16), hy_conv_w[j], hy_conv_b[j], filt_w, hy_skip[j], hy_w_out[j].astype(BF16), tm)
            xc = hyena_layer(xc, g, mod_c, n_ctx * l_ctx, n_ctx, l_ctx, *w, cb=1024)
            xl = hyena_layer(xl, g, mod_l, l_dec, n_dec, l_dec, *w, cb=256)

    y_prompt = final_norm(xc, final_g, tm).reshape(n_ctx, l_ctx, D_MODEL)
    y_sample = final_norm(xl, final_g, tm).reshape(n_dec, l_dec, D_MODEL)
    new_state_s5 = jnp.stack(s5_new, axis=1)
    new_state_ret = jnp.stack(ret_new, axis=1)
    return (y_prompt, y_sample, new_state_s5, new_state_ret)
```

```python
import functools
import math

import jax
import jax.numpy as jnp
from jax import lax
from jax.experimental import pallas as pl
from jax.experimental.pallas import tpu as pltpu

F32 = jnp.float32
BF16 = jnp.bfloat16

D_MODEL = 1024
DEPTH = 4
N_MIXERS = 3
EPS = 1e-6
GRID_W = 64

S5_GROUP = 16
S5_GROUPS = D_MODEL // S5_GROUP
S5_STATE = 64
S5_CHUNK = 16
S5_TILE = S5_CHUNK * S5_GROUP
S5_SLAB = 128 // S5_GROUP
S5_SEQS_PER_STEP = 4

RET_HEADS = 8
RET_QK = D_MODEL
RET_V = 2 * D_MODEL
RET_DK = RET_QK // RET_HEADS
RET_DV = RET_V // RET_HEADS
RET_BLOCK = 256
ROPE_BASE = 10000.0

HY_WIDTH = 2 * D_MODEL
HY_ORDER = 2
HY_SHORT = 3
HY_BANDS = 16
HY_SHORT_DECAY_PCT = 0.3
HY_LONG_DECAY_PCT = 1.5
HY_DECAY_TARGET = 1e-2
HY_LANES = 128
HY_ROW_BLOCK = 512

VMEM_LIMIT_BYTES = 56 * 1024 * 1024
OUT_ROWS = 512


def _params(*sem, vmem=None):
    return pltpu.CompilerParams(dimension_semantics=sem, vmem_limit_bytes=vmem)


def _silu(x):
    return x * jax.nn.sigmoid(x)


def _mod_body(c_ref, w_ref, b_ref, o_ref):
    a = _silu(c_ref[...]).astype(BF16)
    o_ref[0] = jnp.dot(a, w_ref[0].astype(BF16), preferred_element_type=F32) + b_ref[0]


def ada_mod_all(cvecs, mod_w, mod_b):
    r = cvecs.shape[0]
    tn = D_MODEL
    return pl.pallas_call(
        _mod_body,
        grid=(DEPTH, 3 * D_MODEL // tn),
        in_specs=[pl.BlockSpec((r, D_MODEL), lambda i, j: (0, 0)),
                  pl.BlockSpec((1, D_MODEL, tn), lambda i, j: (i, 0, j)),
                  pl.BlockSpec((1, 1, tn), lambda i, j: (i, 0, j))],
        out_specs=pl.BlockSpec((1, r, tn), lambda i, j: (i, 0, j)),
        out_shape=jax.ShapeDtypeStruct((DEPTH, r, 3 * D_MODEL), F32),
        compiler_params=_params("parallel", "parallel"),
        name="ada_mod",
    )(cvecs, mod_w, mod_b.reshape(DEPTH, 1, 3 * D_MODEL))


def _inproj_body(x_ref, g_ref, mod_ref, w_ref, o_ref, h_scr):
    @pl.when(pl.program_id(1) == 0)
    def _():
        x = x_ref[...]
        y = x * lax.rsqrt(jnp.mean(x * x, axis=-1, keepdims=True) + EPS) * g_ref[...]
        h_scr[...] = (y * (1.0 + mod_ref[1:2, :]) + mod_ref[0:1, :]).astype(BF16)

    o_ref[...] = jnp.dot(h_scr[...], w_ref[...], preferred_element_type=F32).astype(o_ref.dtype)


def in_proj(x, g, mod, w, rows_per_mod, tm, tn):
    n, nout = x.shape[0], w.shape[1]
    return pl.pallas_call(
        _inproj_body,
        grid=(n // tm, nout // tn),
        in_specs=[pl.BlockSpec((tm, D_MODEL), lambda i, j: (i, 0)),
                  pl.BlockSpec((1, D_MODEL), lambda i, j: (0, 0)),
                  pl.BlockSpec((None, 3, D_MODEL), lambda i, j: ((i * tm) // rows_per_mod, 0, 0)),
                  pl.BlockSpec((D_MODEL, tn), lambda i, j: (0, j))],
        out_specs=pl.BlockSpec((tm, tn), lambda i, j: (i, j)),
        out_shape=jax.ShapeDtypeStruct((n, nout), BF16),
        scratch_shapes=[pltpu.VMEM((tm, D_MODEL), BF16)],
        compiler_params=_params("parallel", "arbitrary", vmem=VMEM_LIMIT_BYTES),
        name="in_proj",
    )(x, g.reshape(1, D_MODEL), mod, w)


def _outproj_body(y_ref, w_ref, x_ref, mod_ref, o_ref):
    out = jnp.dot(y_ref[...], w_ref[...], preferred_element_type=F32)
    o_ref[...] = x_ref[...] + mod_ref[2:3, :] * out


def out_proj(y, w, x, mod, rows_per_mod, tm):
    n, width = y.shape
    return pl.pallas_call(
        _outproj_body,
        grid=(n // tm,),
        in_specs=[pl.BlockSpec((tm, width), lambda i: (i, 0)),
                  pl.BlockSpec((width, D_MODEL), lambda i: (0, 0)),
                  pl.BlockSpec((tm, D_MODEL), lambda i: (i, 0)),
                  pl.BlockSpec((None, 3, D_MODEL), lambda i: ((i * tm) // rows_per_mod, 0, 0))],
        out_specs=pl.BlockSpec((tm, D_MODEL), lambda i: (i, 0)),
        out_shape=jax.ShapeDtypeStruct((n, D_MODEL), F32),
        compiler_params=_params("parallel", vmem=VMEM_LIMIT_BYTES),
        name="out_proj",
    )(y, w, x, mod)


def _final_norm_body(x_ref, g_ref, o_ref):
    x = x_ref[...]
    o_ref[...] = x * lax.rsqrt(jnp.mean(x * x, axis=-1, keepdims=True) + EPS) * g_ref[...]


def final_norm(x, g, tm):
    n = x.shape[0]
    return pl.pallas_call(
        _final_norm_body,
        grid=(n // tm,),
        in_specs=[pl.BlockSpec((tm, D_MODEL), lambda i: (i, 0)),
                  pl.BlockSpec((1, D_MODEL), lambda i: (0, 0))],
        out_specs=pl.BlockSpec((tm, D_MODEL), lambda i: (i, 0)),
        out_shape=jax.ShapeDtypeStruct((n, D_MODEL), F32),
        compiler_params=_params("parallel"),
        name="final_norm",
    )(x, g.reshape(1, D_MODEL))


def s5_tables(lam_re, lam_im, log_step, b_re, b_im, c_re, c_im, d_skip):
    t_len = S5_CHUNK
    dt = jnp.exp(log_step)[..., None]
    ab_re = jnp.exp(lam_re * dt) * jnp.cos(lam_im * dt)
    ab_im = jnp.exp(lam_re * dt) * jnp.sin(lam_im * dt)
    den = lam_re * lam_re + lam_im * lam_im
    nr, ni = ab_re - 1.0, ab_im
    f_re = (nr * lam_re + ni * lam_im) / den
    f_im = (ni * lam_re - nr * lam_im) / den
    bb_re = f_re[..., None] * b_re - f_im[..., None] * b_im
    bb_im = f_re[..., None] * b_im + f_im[..., None] * b_re
    ks = jnp.arange(t_len + 1, dtype=F32)[:, None, None, None]
    pw_mag = jnp.exp(ks * (lam_re * dt)[None])
    pw_re = pw_mag * jnp.cos(ks * (lam_im * dt)[None])
    pw_im = pw_mag * jnp.sin(ks * (lam_im * dt)[None])

    npair = S5_GROUPS // 2
    pair_eye = jnp.eye(2, dtype=F32)

    def pair_pack(a):
        a = a.reshape(2, npair, 2, S5_GROUP, S5_STATE)
        a = a[:, :, :, :, None, :] * pair_eye[None, None, :, None, :, None]
        return jnp.transpose(a.reshape(2, npair, 2, S5_GROUP, 2 * S5_STATE), (1, 0, 2, 3, 4))

    bbt = jnp.stack([pair_pack(jnp.swapaxes(bb_re, 2, 3)), pair_pack(jnp.swapaxes(bb_im, 2, 3))], axis=2)
    cpk = jnp.stack([pair_pack(c_re), pair_pack(c_im)], axis=2)
    pw = jnp.stack([pw_re, pw_im], axis=0).reshape(2, t_len + 1, 2, npair, 2 * S5_STATE)
    pw = jnp.transpose(pw, (3, 2, 0, 1, 4))
    dmat = jnp.eye(S5_GROUP, dtype=F32) * d_skip.reshape(S5_GROUPS, 1, S5_GROUP)
    dmat = jnp.pad(dmat, ((0, 0), (0, 0), (0, S5_TILE - S5_GROUP))).reshape(npair, 2, S5_GROUP, S5_TILE)
    m, f, et = s5_table_kernel(bbt, cpk, pw, dmat)

    dec = jnp.stack([pw_re[t_len, 0], pw_im[t_len, 0], pw_re[t_len, 1], pw_im[t_len, 1]], axis=0)
    dec = jnp.transpose(dec.reshape(4, S5_GROUPS // S5_SLAB, S5_SLAB * S5_STATE), (1, 0, 2))
    return m.reshape(S5_GROUPS, S5_TILE, S5_TILE), f, et, dec


def _s5_table_body(bb_ref, c_ref, pw_ref, d_ref, m_ref, f_ref, et_ref):
    t_len = S5_CHUNK
    nt_dims = (((1,), (1,)), ((), ()))
    hp = lax.Precision.HIGHEST

    def times_power(ref, d, g, k):
        pr, pi = pw_ref[d, 0, k:k + 1, :], pw_ref[d, 1, k:k + 1, :]
        ar, ai = ref[d, 0, g], ref[d, 1, g]
        return ar * pr - ai * pi, ar * pi + ai * pr

    def stack(parts):
        return jnp.concatenate(parts, axis=0)

    for d in range(2):
        f_pow = [t_len - 1 - s for s in range(t_len)] if d == 0 else list(range(t_len))
        e_pow = [s + 1 for s in range(t_len)] if d == 0 else [t_len - s for s in range(t_len)]
        fr, fi, er, ei = [], [], [], []
        for g in range(2):
            for s in range(t_len):
                a, b = times_power(bb_ref, d, g, f_pow[s])
                fr.append(a)
                fi.append(b)
                a, b = times_power(c_ref, d, g, e_pow[s])
                er.append(a)
                ei.append(-b)
        lanes = 2 * S5_STATE
        f_ref[:, (2 * d) * lanes:(2 * d + 1) * lanes] = stack(fr).astype(f_ref.dtype)
        f_ref[:, (2 * d + 1) * lanes:(2 * d + 2) * lanes] = stack(fi).astype(f_ref.dtype)
        et_ref[:, (2 * d) * lanes:(2 * d + 1) * lanes] = stack(er).astype(et_ref.dtype)
        et_ref[:, (2 * d + 1) * lanes:(2 * d + 2) * lanes] = stack(ei).astype(et_ref.dtype)

    zeros = jnp.zeros((S5_GROUP, S5_TILE), F32)
    for g in range(2):
        def lag_kernels(d, powers):
            car, cai = zip(*(times_power(c_ref, d, g, k) for k in powers))
            return (lax.dot_general(bb_ref[d, 0, g], stack(car), nt_dims, precision=hp, preferred_element_type=F32)
                    - lax.dot_general(bb_ref[d, 1, g], stack(cai), nt_dims, precision=hp,
                                      preferred_element_type=F32))
        v_f = jnp.concatenate([zeros, lag_kernels(0, range(t_len)) + d_ref[g]], axis=1)
        v_b = jnp.concatenate([lag_kernels(1, range(t_len - 1, -1, -1)), zeros], axis=1)
        width = 2 * S5_TILE
        for s in range(t_len):
            blk_f = pltpu.roll(v_f, (width - (t_len - s) * S5_GROUP) % width, 1)[:, :S5_TILE]
            blk_b = pltpu.roll(v_b, (width - (t_len - 1 - s) * S5_GROUP) % width, 1)[:, :S5_TILE]
            m_ref[g, s * S5_GROUP:(s + 1) * S5_GROUP, :] = (blk_f + blk_b).astype(m_ref.dtype)


def s5_table_kernel(bbt, cpk, pw, dmat):
    npair = S5_GROUPS // 2
    pspec = pl.BlockSpec((None, 2, 2, 2, S5_GROUP, 2 * S5_STATE), lambda k: (k, 0, 0, 0, 0, 0))
    wide = pl.BlockSpec((None, 2 * S5_TILE, 2 * S5_TILE), lambda k: (k, 0, 0))
    return pl.pallas_call(
        _s5_table_body,
        grid=(npair,),
        in_specs=[pspec, pspec,
                  pl.BlockSpec((None, 2, 2, S5_CHUNK + 1, 2 * S5_STATE), lambda k: (k, 0, 0, 0, 0)),
                  pl.BlockSpec((None, 2, S5_GROUP, S5_TILE), lambda k: (k, 0, 0, 0))],
        out_specs=[pl.BlockSpec((None, 2, S5_TILE, S5_TILE), lambda k: (k, 0, 0, 0)), wide, wide],
        out_shape=[jax.ShapeDtypeStruct((npair, 2, S5_TILE, S5_TILE), BF16),
                   jax.ShapeDtypeStruct((npair, 2 * S5_TILE, 2 * S5_TILE), BF16),
                   jax.ShapeDtypeStruct((npair, 2 * S5_TILE, 2 * S5_TILE), BF16)],
        compiler_params=_params("parallel"),
        name="s5_tables",
    )(bbt, cpk, pw, dmat)


def s5_lane_permutation():
    width = S5_CHUNK * S5_SLAB * S5_GROUP
    i = jnp.arange(width)
    step, grp, ch = i // (S5_SLAB * S5_GROUP), (i // S5_GROUP) % S5_SLAB, i % S5_GROUP
    dst = grp * S5_TILE + step * S5_GROUP + ch
    return (dst[:, None] == jnp.arange(width)[None, :]).astype(BF16)


def _s5_scan_body(u_ref, q_ref, m_ref, f_ref, et_ref, dec_ref, h0_ref, y_ref, hf_ref,
                  uy_s, st_s, *, nchunk, nseq):
    npair = S5_SLAB // 2
    pair_w = 2 * S5_TILE
    lane_w = 2 * S5_STATE
    nt_dims = (((1,), (1,)), ((), ()))
    xcat = jnp.concatenate([u_ref[s] for s in range(S5_CHUNK)], axis=1)
    for k in range(npair):
        cols = slice(k * pair_w, (k + 1) * pair_w)
        up = jnp.dot(xcat, q_ref[:, cols], preferred_element_type=F32).astype(BF16)
        uy_s[:, cols] = up
        x = jnp.dot(up, f_ref[k], preferred_element_type=F32)
        for q in range(4):
            st_s[q * npair + k] = x[:, q * lane_w:(q + 1) * lane_w]

    def lanes_of(ref, q, k):
        return ref[q, :, k * lane_w:(k + 1) * lane_w]

    dec = [[dec_ref[q:q + 1, k * lane_w:(k + 1) * lane_w] for k in range(npair)] for q in range(4)]

    def step(i, carry):
        rows_f = pl.ds(i, nseq, stride=nchunk)
        rows_b = pl.ds(nchunk - 1 - i, nseq, stride=nchunk)
        new = []
        for k in range(npair):
            fr, fi, br, bi = (carry[q * npair + k] for q in range(4))
            xr, xi = st_s[0 * npair + k, rows_f, :], st_s[1 * npair + k, rows_f, :]
            yr, yi = st_s[2 * npair + k, rows_b, :], st_s[3 * npair + k, rows_b, :]
            st_s[0 * npair + k, rows_f, :] = fr
            st_s[1 * npair + k, rows_f, :] = fi
            st_s[2 * npair + k, rows_b, :] = br
            st_s[3 * npair + k, rows_b, :] = bi
            new.append((dec[0][k] * fr - dec[1][k] * fi + xr, dec[0][k] * fi + dec[1][k] * fr + xi,
                        dec[2][k] * br - dec[3][k] * bi + yr, dec[2][k] * bi + dec[3][k] * br + yi))
        return tuple(new[k][q] for q in range(4) for k in range(npair))

    init = tuple(lanes_of(h0_ref, q, k) for q in range(4) for k in range(npair))
    final = lax.fori_loop(0, nchunk, step, init)
    for q in range(4):
        for k in range(npair):
            hf_ref[q, :, k * lane_w:(k + 1) * lane_w] = final[q * npair + k]

    for k in range(npair):
        h = jnp.concatenate([st_s[q * npair + k] for q in range(4)], axis=1)
        ye = lax.dot_general(h.astype(BF16), et_ref[k], nt_dims, preferred_element_type=F32)
        for g in range(2):
            cols = slice(k * pair_w + g * S5_TILE, k * pair_w + (g + 1) * S5_TILE)
            y = jnp.dot(uy_s[:, cols], m_ref[2 * k + g], preferred_element_type=F32)
            uy_s[:, cols] = (y + ye[:, g * S5_TILE:(g + 1) * S5_TILE]).astype(BF16)
    slab_w = S5_SLAB * S5_GROUP
    steps = pair_w // slab_w
    for t0 in range(0, S5_CHUNK, steps):
        yp = lax.dot_general(uy_s[...], q_ref[t0 * slab_w:(t0 + steps) * slab_w, :], nt_dims,
                             preferred_element_type=F32).astype(y_ref.dtype)
        for t in range(steps):
            y_ref[t0 + t] = yp[:, t * slab_w:(t + 1) * slab_w]


def s5_scan(u_s, perm, m, f, e, dec, h0, nchunk, nseq, nb):
    rows = nchunk * nb
    nslab = S5_GROUPS // S5_SLAB
    slab_w = S5_SLAB * S5_GROUP
    state_w = S5_SLAB * S5_STATE
    width = S5_CHUNK * slab_w
    body = functools.partial(_s5_scan_body, nchunk=nchunk, nseq=nb)
    hspec = pl.BlockSpec((None, None, 4, nb, state_w), lambda s, b: (s, b, 0, 0, 0))
    state = pltpu.VMEM((2 * S5_SLAB, rows, 2 * S5_STATE), F32)
    return pl.pallas_call(
        body,
        grid=(nslab, nseq // nb),
        in_specs=[pl.BlockSpec((S5_CHUNK, rows, slab_w), lambda s, b: (0, b, s)),
                  pl.BlockSpec((width, width), lambda s, b: (0, 0), pipeline_mode=pl.Buffered(1)),
                  pl.BlockSpec((S5_SLAB, S5_TILE, S5_TILE), lambda s, b: (s, 0, 0)),
                  pl.BlockSpec((S5_SLAB // 2, 2 * S5_TILE, 2 * S5_TILE), lambda s, b: (s, 0, 0)),
                  pl.BlockSpec((S5_SLAB // 2, 2 * S5_TILE, 2 * S5_TILE), lambda s, b: (s, 0, 0)),
                  pl.BlockSpec((None, 4, state_w), lambda s, b: (s, 0, 0)),
                  hspec],
        out_specs=[pl.BlockSpec((S5_CHUNK, rows, slab_w), lambda s, b: (0, b, s)), hspec],
        out_shape=[jax.ShapeDtypeStruct((S5_CHUNK, nseq * nchunk, D_MODEL), BF16),
                   jax.ShapeDtypeStruct((nslab, nseq // nb, 4, nb, state_w), F32)],
        scratch_shapes=[pltpu.VMEM((rows, width), BF16), state],
        compiler_params=_params("parallel", "parallel", vmem=VMEM_LIMIT_BYTES),
        name="s5_scan",
    )(u_s, perm, m, f, e, dec, h0)


def _inproj_steps_body(x_ref, g_ref, mod_ref, w_ref, o_ref):
    x = x_ref[...]
    y = x * lax.rsqrt(jnp.mean(x * x, axis=-1, keepdims=True) + EPS) * g_ref[...]
    h = y * (1.0 + mod_ref[:, 1:2, :]) + mod_ref[:, 0:1, :]
    h = h.reshape(x.shape[0] * x.shape[1], D_MODEL).astype(BF16)
    o_ref[...] = jnp.dot(h, w_ref[...], preferred_element_type=F32).astype(o_ref.dtype)


def _mod_spec(mod, nb):
    if mod.shape[0] == 1:
        return pl.BlockSpec((1, 3, D_MODEL), lambda b, s: (0, 0, 0))
    return pl.BlockSpec((nb, 3, D_MODEL), lambda b, s: (b, 0, 0))


def in_proj_steps(x4, g, mod, w, nb):
    nseq, nchunk, _ = x4.shape
    nout = w.shape[1]
    return pl.pallas_call(
        _inproj_steps_body,
        grid=(nseq // nb, S5_CHUNK),
        in_specs=[pl.BlockSpec((nb, nchunk, D_MODEL), lambda b, s: (b, 0, s)),
                  pl.BlockSpec((1, D_MODEL), lambda b, s: (0, 0)),
                  _mod_spec(mod, nb),
                  pl.BlockSpec((D_MODEL, nout), lambda b, s: (0, 0))],
        out_specs=pl.BlockSpec((None, nb * nchunk, nout), lambda b, s: (s, b, 0)),
        out_shape=jax.ShapeDtypeStruct((S5_CHUNK, nseq * nchunk, nout), BF16),
        compiler_params=_params("parallel", "parallel", vmem=VMEM_LIMIT_BYTES),
        name="in_proj_steps",
    )(x4, g.reshape(1, D_MODEL), mod, w)


def _s5_out_body(y_ref, gate_ref, wg_ref, bg_ref, wo_ref, x_ref, mod_ref, *rest):
    o_ref = rest[-1]
    g = jax.nn.gelu(y_ref[...].astype(F32))
    t = jnp.dot(g.astype(BF16), wg_ref[...], preferred_element_type=F32) + bg_ref[...]
    z = g * jax.nn.sigmoid(t) * _silu(gate_ref[...].astype(F32))
    out = jnp.dot(z.astype(BF16), wo_ref[...], preferred_element_type=F32)
    x = x_ref[...] + mod_ref[:, 2:3, :] * out.reshape(x_ref.shape)
    if len(rest) == 2:
        x = x * lax.rsqrt(jnp.mean(x * x, axis=-1, keepdims=True) + EPS) * rest[0][...]
    o_ref[...] = x


def s5_out(y_s, u_s, w_glu, b_glu, w_out, x4, mod, nb, final_g=None):
    nseq, nchunk, _ = x4.shape
    rows = nb * nchunk
    xspec = pl.BlockSpec((nb, nchunk, D_MODEL), lambda b, s: (b, 0, s))
    wspec = pl.BlockSpec((D_MODEL, D_MODEL), lambda b, s: (0, 0))
    vspec = pl.BlockSpec((1, D_MODEL), lambda b, s: (0, 0))
    in_specs = [pl.BlockSpec((None, rows, D_MODEL), lambda b, s: (s, b, 0)),
                pl.BlockSpec((None, rows, D_MODEL), lambda b, s: (s, b, 1)),
                wspec, vspec, wspec, xspec, _mod_spec(mod, nb)]
    args = [y_s, u_s, w_glu, b_glu.reshape(1, D_MODEL), w_out, x4, mod]
    if final_g is not None:
        in_specs.append(vspec)
        args.append(final_g.reshape(1, D_MODEL))
    return pl.pallas_call(
        _s5_out_body,
        grid=(nseq // nb, S5_CHUNK),
        in_specs=in_specs,
        out_specs=xspec,
        out_shape=jax.ShapeDtypeStruct(x4.shape, F32),
        compiler_params=_params("parallel", "parallel", vmem=VMEM_LIMIT_BYTES),
        name="s5_out",
    )(*args)


def s5_layer(x, g, mod, nseq, seq_len, h0, w_in, tables, perm, w_glu, b_glu, w_out, nb, final_g=None):
    nchunk = seq_len // S5_CHUNK
    nslab = S5_GROUPS // S5_SLAB
    state_w = S5_SLAB * S5_STATE
    x4 = x.reshape(nseq, nchunk, S5_CHUNK * D_MODEL)
    u_s = in_proj_steps(x4, g, mod, w_in, nb)
    if h0 is None:
        h0_p = jnp.zeros((nslab, nseq // nb, 4, nb, state_w), F32)
    else:
        h0_p = jnp.transpose(h0.reshape(nseq // nb, nb, 4, nslab, state_w), (3, 0, 2, 1, 4))
    m, f, e, dec = tables
    y_s, hf = s5_scan(u_s, perm, m, f, e, dec, h0_p, nchunk, nseq, nb)
    x_new = s5_out(y_s, u_s, w_glu, b_glu, w_out, x4, mod, nb, final_g).reshape(nseq * seq_len, D_MODEL)
    states = jnp.transpose(hf, (1, 3, 2, 0, 4)).reshape(nseq, 2, 2, S5_GROUPS, S5_STATE)
    return x_new, states


def _ret_body(*refs, seq_len, rope, has_init, want_state):
    refs = list(refs)
    q_ref, k_ref, v_ref, gate_ref, lg_ref = refs[:5]
    pos = 5
    if rope:
        cos_ref, sin_ref = refs[pos:pos + 2]
        pos += 2
    if has_init:
        s0_ref = refs[pos]
        pos += 1
    o_ref = refs[pos]
    pos += 1
    if want_state:
        sfin_ref = refs[pos]
        pos += 1
    sf_scr = refs[pos]

    c_len = RET_BLOCK
    nblk = seq_len // c_len
    lgf = lg_ref[0:1, 0:1]
    lgb = lg_ref[1:2, 0:1]
    ii = lax.broadcasted_iota(jnp.int32, (c_len, c_len), 0)
    jj = lax.broadcasted_iota(jnp.int32, (c_len, c_len), 1)
    diff = (ii - jj).astype(F32)
    decay = jnp.exp(jnp.abs(diff) * jnp.where(diff >= 0, lgf, lgb))
    ic = lax.broadcasted_iota(jnp.int32, (c_len, 1), 0).astype(F32)
    q_dec_f = jnp.exp((ic + 1.0) * lgf)
    q_dec_b = jnp.exp((c_len - ic) * lgb)
    k_dec_f = jnp.exp((c_len - 1.0 - ic) * lgf)
    k_dec_b = jnp.exp(ic * lgb)
    c_dec_f = jnp.exp(c_len * lgf)
    c_dec_b = jnp.exp(c_len * lgb)

    q = q_ref[...].astype(F32)
    k = k_ref[...].astype(F32)
    if rope:
        lane = lax.broadcasted_iota(jnp.int32, (seq_len, RET_DK), 1)
        first = (lane % (RET_DK // 2)) < (RET_DK // 4)
        cos = cos_ref[...]
        sin = sin_ref[...]

        def rot(x):
            swapped = jnp.where(first, pltpu.roll(x, RET_DK - RET_DK // 4, 1), pltpu.roll(x, RET_DK // 4, 1))
            return x * cos + swapped * sin

        q = rot(q)
        k = rot(k)

    def blk(a, c):
        return a[c * c_len:(c + 1) * c_len]

    def kv_state(kd, c):
        kt = jnp.transpose(blk(k, c) * kd).astype(BF16)
        return jnp.dot(kt, v_ref[c * c_len:(c + 1) * c_len, :], preferred_element_type=F32)

    use_state = has_init or nblk > 1
    s_f = s0_ref[0] if has_init else jnp.zeros((RET_DK, RET_DV), F32)
    for c in range(nblk):
        if use_state:
            sf_scr[c] = s_f
        if want_state or c < nblk - 1:
            s_f = c_dec_f * s_f + kv_state(k_dec_f, c)
    if want_state:
        sfin_ref[0] = s_f

    s_b = s0_ref[1] if has_init else jnp.zeros((RET_DK, RET_DV), F32)
    for c in range(nblk - 1, -1, -1):
        qc = blk(q, c)
        kc = blk(k, c)
        vc = v_ref[c * c_len:(c + 1) * c_len, :]
        s = lax.dot_general(qc.astype(BF16), kc.astype(BF16), (((1,), (1,)), ((), ())),
                            preferred_element_type=F32)
        o = jnp.dot((s * decay).astype(BF16), vc, preferred_element_type=F32)
        if use_state:
            o = o + jnp.dot((qc * q_dec_f).astype(BF16), sf_scr[c].astype(BF16), preferred_element_type=F32)
            o = o + jnp.dot((qc * q_dec_b).astype(BF16), s_b.astype(BF16), preferred_element_type=F32)
        o = o * lax.rsqrt(jnp.mean(o * o, axis=-1, keepdims=True) + EPS)
        gate = gate_ref[c * c_len:(c + 1) * c_len, :].astype(F32)
        o_ref[c * c_len:(c + 1) * c_len, :] = (o * _silu(gate)).astype(o_ref.dtype)
        if want_state or c > 0:
            s_b = c_dec_b * s_b + kv_state(k_dec_b, c)
    if want_state:
        sfin_ref[1] = s_b


def retention_core(proj3, lg, rope_tabs, s0, want_state):
    nseq, seq_len, _ = proj3.shape
    nblk = seq_len // RET_BLOCK
    rope = rope_tabs is not None
    has_init = s0 is not None
    kq = RET_QK // RET_DK
    in_specs = [pl.BlockSpec((None, seq_len, RET_DK), lambda b, h: (b, 0, h)),
                pl.BlockSpec((None, seq_len, RET_DK), lambda b, h: (b, 0, kq + h)),
                pl.BlockSpec((None, seq_len, RET_DV), lambda b, h: (b, 0, 2 * RET_QK // RET_DV + h)),
                pl.BlockSpec((None, seq_len, RET_DV), lambda b, h: (b, 0, (2 * RET_QK + RET_V) // RET_DV + h)),
                pl.BlockSpec((None, 2, 128), lambda b, h: (h, 0, 0))]
    args = [proj3, proj3, proj3, proj3, lg]
    if rope:
        tab = pl.BlockSpec((seq_len, RET_DK), lambda b, h: (0, 0))
        in_specs += [tab, tab]
        args += list(rope_tabs)
    sspec = pl.BlockSpec((None, 2, None, RET_DK, RET_DV), lambda b, h: (b, 0, h, 0, 0))
    if has_init:
        in_specs.append(sspec)
        args.append(s0)
    out_specs = [pl.BlockSpec((None, seq_len, RET_DV), lambda b, h: (b, 0, h))]
    out_shape = [jax.ShapeDtypeStruct((nseq, seq_len, RET_V), BF16)]
    if want_state:
        out_specs.append(sspec)
        out_shape.append(jax.ShapeDtypeStruct((nseq, 2, RET_HEADS, RET_DK, RET_DV), F32))
    body = functools.partial(_ret_body, seq_len=seq_len, rope=rope, has_init=has_init, want_state=want_state)
    res = pl.pallas_call(
        body,
        grid=(nseq, RET_HEADS),
        in_specs=in_specs,
        out_specs=out_specs,
        out_shape=out_shape,
        scratch_shapes=[pltpu.VMEM((nblk, RET_DK, RET_DV), F32)],
        compiler_params=_params("parallel", "parallel", vmem=VMEM_LIMIT_BYTES),
        name="retention",
    )(*args)
    return res[0], (res[1] if want_state else None)


def rope_tables(seq_len):
    t = jnp.arange(seq_len)
    half = RET_DK // 2
    nfreq = half // 2
    inv = ROPE_BASE ** (-jnp.arange(nfreq, dtype=F32) / nfreq)
    parts_c, parts_s = [], []
    for p in (t // GRID_W, t % GRID_W):
        ang = p.astype(F32)[:, None] * inv[None, :]
        parts_c += [jnp.cos(ang), jnp.cos(ang)]
        parts_s += [-jnp.sin(ang), jnp.sin(ang)]
    return jnp.concatenate(parts_c, axis=-1), jnp.concatenate(parts_s, axis=-1)


def retention_layer(x, g, mod, rows_per_mod, nseq, seq_len, s0, grid_pos, w_in, lg, w_out, tm):
    proj = in_proj(x, g, mod, w_in, rows_per_mod, tm, D_MODEL)
    proj3 = proj.reshape(nseq, seq_len, 2 * RET_QK + 2 * RET_V)
    tabs = rope_tables(seq_len) if grid_pos else None
    y, states = retention_core(proj3, lg, tabs, s0, want_state=s0 is None)
    x_new = out_proj(y.reshape(nseq * seq_len, RET_V), w_out, x, mod, rows_per_mod, min(tm, OUT_ROWS))
    return x_new, states


def _filter_dft_body(a_ref, b_ref, norm_ref, o_ref, *, seq_len, tm):
    freq = pl.program_id(0) * tm + lax.broadcasted_iota(jnp.int32, (tm, 1), 0)
    wf = jnp.where(freq == 0, 1.0, 2.0) * (1.0 / (2.0 * seq_len))
    acc = jnp.dot(a_ref[...], b_ref[...], preferred_element_type=F32)
    o_ref[...] = acc * wf / (norm_ref[...] + EPS)


def filter_dft(a, b, norm, tm, tn):
    m, kd = a.shape
    n = b.shape[1]
    body = functools.partial(_filter_dft_body, seq_len=m, tm=tm)
    return pl.pallas_call(
        body,
        grid=(m // tm, n // tn),
        in_specs=[pl.BlockSpec((tm, kd), lambda i, j: (i, 0)),
                  pl.BlockSpec((kd, tn), lambda i, j: (0, j)),
                  pl.BlockSpec((1, tn), lambda i, j: (0, j))],
        out_specs=pl.BlockSpec((tm, tn), lambda i, j: (i, j)),
        out_shape=jax.ShapeDtypeStruct((m, n), F32),
        compiler_params=_params("parallel", "parallel", vmem=VMEM_LIMIT_BYTES),
        name="filter_dft",
    )(a, b, norm)


def _filter_gen_body(feat_ref, w1_ref, b1_ref, w2_ref, b2_ref, w3f_ref, w3b_ref, dl_ref,
                     fs_ref, fd_ref, norm_ref, ny_ref, *, seq_len, tr):
    r = pl.program_id(1)
    hp = lax.Precision.HIGHEST
    z = jnp.sin(jnp.dot(feat_ref[...], w1_ref[...], precision=hp, preferred_element_type=F32) + b1_ref[...])
    z = jnp.sin(jnp.dot(z, w2_ref[...], precision=hp, preferred_element_type=F32) + b2_ref[...]).astype(BF16)
    lag = r * tr + lax.broadcasted_iota(jnp.int32, (tr, 1), 0)
    t = lag.astype(F32) * (1.0 / (seq_len - 1))
    win = jnp.exp(-t * jnp.abs(dl_ref[...]))
    ff = jnp.dot(z, w3f_ref[...], preferred_element_type=F32) * win
    fb = jnp.dot(z, w3b_ref[...], preferred_element_type=F32) * win
    part = jnp.sum(jnp.abs(ff) + jnp.abs(fb), axis=0, keepdims=True)
    fb = jnp.where(lag == 0, 0.0, fb)
    fs = ff + fb
    sign = jnp.where(lag % 2 == 0, 1.0, -1.0)

    @pl.when(r == 0)
    def _():
        norm_ref[...] = jnp.zeros_like(norm_ref)
        ny_ref[...] = jnp.zeros_like(ny_ref)

    norm_ref[...] += part
    ny_ref[...] += jnp.sum(sign * fs, axis=0, keepdims=True)
    fs_ref[...] = fs.astype(fs_ref.dtype)
    fd_ref[...] = (ff - fb).astype(fd_ref.dtype)


def filter_gen(feat, w1, b1, w2, b2, w3, deltas, tr, tn):
    seq_len, kf = feat.shape
    hid = w1.shape[1]
    ncol = HY_ORDER * HY_WIDTH
    nj = ncol // tn
    body = functools.partial(_filter_gen_body, seq_len=seq_len, tr=tr)
    full = lambda shape: pl.BlockSpec(shape, lambda j, r: (0, 0))
    row = pl.BlockSpec((1, tn), lambda j, r: (0, j))
    return pl.pallas_call(
        body,
        grid=(nj, seq_len // tr),
        in_specs=[pl.BlockSpec((tr, kf), lambda j, r: (r, 0)),
                  full((kf, hid)), full((1, hid)), full((hid, hid)), full((1, hid)),
                  pl.BlockSpec((hid, tn), lambda j, r: (0, j)),
                  pl.BlockSpec((hid, tn), lambda j, r: (0, nj + j)),
                  pl.BlockSpec((1, tn), lambda j, r: (0, j % (HY_WIDTH // tn)))],
        out_specs=[pl.BlockSpec((tr, tn), lambda j, r: (r, j)),
                   pl.BlockSpec((tr, tn), lambda j, r: (r, j)), row, row],
        out_shape=[jax.ShapeDtypeStruct((seq_len, ncol), BF16), jax.ShapeDtypeStruct((seq_len, ncol), BF16),
                   jax.ShapeDtypeStruct((1, ncol), F32), jax.ShapeDtypeStruct((1, ncol), F32)],
        compiler_params=_params("parallel", "arbitrary", vmem=VMEM_LIMIT_BYTES),
        name="filter_gen",
    )(feat, w1, b1, w2, b2, w3, w3, deltas)


def dft_matrices(seq_len):
    idx = jnp.arange(seq_len, dtype=jnp.int32)
    prod = (idx[:, None] * idx[None, :]) % (2 * seq_len)
    ang = prod.astype(F32) * (math.pi / seq_len)
    return jnp.cos(ang).astype(BF16), (-jnp.sin(ang)).astype(BF16)


def hyena_filter_spectrum(seq_len, cm, sm, w1, b1, w2, b2, w3):
    t = jnp.linspace(0.0, 1.0, seq_len, dtype=F32)[:, None]
    w = 2.0 * math.pi * jnp.arange(seq_len, dtype=F32)[:, None] / seq_len
    f = jnp.linspace(1e-4, HY_BANDS - 1.0, HY_BANDS, dtype=F32)[None, :]
    feat = jnp.concatenate([t, jnp.cos(f * w), -jnp.sin(f * w)], axis=-1)
    emb, hid = w1.shape
    pe, ph = HY_LANES - emb, HY_LANES - hid
    feat = jnp.pad(feat, ((0, 0), (0, pe)))
    w1p = jnp.pad(w1, ((0, pe), (0, ph)))
    w2p = jnp.pad(w2, ((0, ph), (0, ph)))
    w3p = jnp.pad(w3, ((0, ph), (0, 0))).astype(BF16)
    b1p = jnp.pad(b1, (0, ph)).reshape(1, HY_LANES)
    b2p = jnp.pad(b2, (0, ph)).reshape(1, HY_LANES)
    max_decay = math.log(HY_DECAY_TARGET) / HY_SHORT_DECAY_PCT
    min_decay = math.log(HY_DECAY_TARGET) / HY_LONG_DECAY_PCT
    deltas = jnp.linspace(min_decay, max_decay, HY_WIDTH, dtype=F32).reshape(1, HY_WIDTH)
    tile = min(seq_len, 512)
    fs, fd, norm, ny = filter_gen(feat, w1p, b1p, w2p, b2p, w3p, deltas, tile, 1024)
    kr = filter_dft(cm, fs, norm, tile, 512)
    ki = filter_dft(sm, fd, norm, tile, 512)
    kn = ny / (norm + EPS) / (2.0 * seq_len)
    return kr, ki, kn


def _hyena_body(v_ref, x1_ref, x2_ref, gate_ref, cw_ref, cb_ref, kr0_ref, ki0_ref, kr1_ref, ki1_ref,
                kn_ref, sk_ref, cm_ref, sm_ref, o_ref, z_scr, xg_scr, pre_scr, pim_scr, *, seq_len):
    row = lax.broadcasted_iota(jnp.int32, (seq_len, 1), 0)
    sign = jnp.where(row % 2 == 0, 1.0, -1.0).astype(F32)

    def short_conv(x_ref, which):
        x = x_ref[...].astype(F32)
        prev = jnp.where(row == 0, 0.0, pltpu.roll(x, 1, 0))
        nxt = jnp.where(row == seq_len - 1, 0.0, pltpu.roll(x, seq_len - 1, 0))
        return (cb_ref[which:which + 1, :] + prev * cw_ref[0, which:which + 1, :]
                + x * cw_ref[1, which:which + 1, :] + nxt * cw_ref[2, which:which + 1, :])

    rb = min(seq_len, HY_ROW_BLOCK)
    z_scr[...] = short_conv(v_ref, 0)
    for o, (xg_ref, kr_ref, ki_ref) in enumerate(((x1_ref, kr0_ref, ki0_ref), (x2_ref, kr1_ref, ki1_ref))):
        z = z_scr[...]
        zb = z.astype(BF16)
        ny = jnp.sum(z * sign, axis=0, keepdims=True) * kn_ref[o:o + 1, :]
        for r in range(0, seq_len, rb):
            z_re = jnp.dot(cm_ref[r:r + rb, :], zb, preferred_element_type=F32)
            z_im = jnp.dot(sm_ref[r:r + rb, :], zb, preferred_element_type=F32)
            kr = kr_ref[r:r + rb, :]
            ki = ki_ref[r:r + rb, :]
            pre_scr[r:r + rb, :] = (z_re * kr - z_im * ki).astype(BF16)
            pim_scr[r:r + rb, :] = (z_re * ki + z_im * kr).astype(BF16)
        xg_scr[...] = short_conv(xg_ref, o + 1)
        for r in range(0, seq_len, rb):
            zc = (jnp.dot(cm_ref[r:r + rb, :], pre_scr[...], preferred_element_type=F32)
                  + jnp.dot(sm_ref[r:r + rb, :], pim_scr[...], preferred_element_type=F32)
                  + sign[r:r + rb] * ny)
            z_scr[r:r + rb, :] = xg_scr[r:r + rb, :] * (zc + z_scr[r:r + rb, :] * sk_ref[o:o + 1, :])
    o_ref[...] = (z_scr[...] * _silu(gate_ref[...].astype(F32))).astype(o_ref.dtype)


def hyena_core(proj3, conv_w, conv_b, kr, ki, kn, skip, cm, sm, cb):
    nseq, seq_len, _ = proj3.shape
    nj = HY_WIDTH // cb
    once = pl.Buffered(1)

    def act(k):
        return pl.BlockSpec((None, seq_len, cb), lambda j, b: (b, 0, k * nj + j))

    def spec(k):
        return pl.BlockSpec((seq_len, cb), lambda j, b: (0, k * nj + j), pipeline_mode=once)

    dft = pl.BlockSpec((seq_len, seq_len), lambda j, b: (0, 0), pipeline_mode=once)
    body = functools.partial(_hyena_body, seq_len=seq_len)
    return pl.pallas_call(
        body,
        grid=(nj, nseq),
        in_specs=[act(0), act(1), act(2), act(3),
                  pl.BlockSpec((HY_SHORT, 3, cb), lambda j, b: (0, 0, j)),
                  pl.BlockSpec((3, cb), lambda j, b: (0, j)),
                  spec(0), spec(0), spec(1), spec(1),
                  pl.BlockSpec((HY_ORDER, cb), lambda j, b: (0, j)),
                  pl.BlockSpec((HY_ORDER, cb), lambda j, b: (0, j)),
                  dft, dft],
        out_specs=pl.BlockSpec((None, seq_len, cb), lambda j, b: (b, 0, j)),
        out_shape=jax.ShapeDtypeStruct((nseq, seq_len, HY_WIDTH), BF16),
        scratch_shapes=[pltpu.VMEM((seq_len, cb), F32), pltpu.VMEM((seq_len, cb), F32),
                        pltpu.VMEM((seq_len, cb), BF16), pltpu.VMEM((seq_len, cb), BF16)],
        compiler_params=_params("parallel", "arbitrary", vmem=VMEM_LIMIT_BYTES),
        name="hyena",
    )(proj3, proj3, proj3, proj3,
      conv_w.reshape(HY_SHORT, 3, HY_WIDTH), conv_b.reshape(3, HY_WIDTH),
      kr, ki, kr, ki, kn.reshape(HY_ORDER, HY_WIDTH), skip, cm, sm)


def hyena_layer(x, g, mod, rows_per_mod, nseq, seq_len, w_in, conv_w, conv_b, filt_w, skip, w_out, tm, cb):
    proj = in_proj(x, g, mod, w_in, rows_per_mod, tm, D_MODEL)
    proj3 = proj.reshape(nseq, seq_len, 4 * HY_WIDTH)
    cm, sm = dft_matrices(seq_len)
    kr, ki, kn = hyena_filter_spectrum(seq_len, cm, sm, *filt_w)
    y = hyena_core(proj3, conv_w, conv_b, kr, ki, kn, skip, cm, sm, cb)
    return out_proj(y.reshape(nseq * seq_len, HY_WIDTH), w_out, x, mod, rows_per_mod, min(tm, OUT_ROWS))


def kernel(x_prompt, x_sample, c, state_s5, state_ret, c_ctx, norm_g, mod_w, mod_b, s5_w_in, s5_lam_re, s5_lam_im, s5_log_step, s5_b_re, s5_b_im, s5_c_re, s5_c_im, s5_d, s5_w_glu, s5_b_glu, s5_w_out, ret_w_in, ret_decay_logit, ret_w_out, hy_w_in, hy_conv_w, hy_conv_b, hy_f_w1, hy_f_b1, hy_f_w2, hy_f_b2, hy_f_w3, hy_skip, hy_w_out, final_g):
    n_ctx, l_ctx, _ = x_prompt.shape
    n_dec, l_dec, _ = x_sample.shape
    xc = x_prompt.reshape(n_ctx * l_ctx, D_MODEL)
    xl = x_sample.reshape(n_dec * l_dec, D_MODEL)

    pad = (-(1 + n_dec)) % 8
    cvecs = jnp.concatenate([c_ctx[None, :], c, jnp.zeros((pad, D_MODEL), F32)], axis=0)
    mods = ada_mod_all(cvecs, mod_w, mod_b).reshape(DEPTH, -1, 3, D_MODEL)

    tm = 1024
    perm = s5_lane_permutation()
    last_is_s5 = (DEPTH - 1) % N_MIXERS == 0
    s5_new, ret_new = [], []
    for i in range(DEPTH):
        kind, j = i % N_MIXERS, i // N_MIXERS
        mod_c = mods[i, 0:1]
        mod_l = mods[i, 1:1 + n_dec]
        g = norm_g[i]
        if kind == 0:
            tables = s5_tables(s5_lam_re[j], s5_lam_im[j], s5_log_step[j], s5_b_re[j], s5_b_im[j],
                               s5_c_re[j], s5_c_im[j], s5_d[j])
            w = (s5_w_in[j].astype(BF16), tables, perm, s5_w_glu[j].astype(BF16), s5_b_glu[j],
                 s5_w_out[j].astype(BF16))
            fg = final_g if last_is_s5 and i == DEPTH - 1 else None
            xc, st = s5_layer(xc, g, mod_c, n_ctx, l_ctx, None, *w, nb=n_ctx, final_g=fg)
            xl, _ = s5_layer(xl, g, mod_l, n_dec, l_dec, state_s5[:, j], *w, nb=S5_SEQS_PER_STEP, final_g=fg)
            s5_new.append(st)
        elif kind == 1:
            w_in = ret_w_in[j]
            kscale = jnp.concatenate([jnp.ones((RET_QK,), F32), jnp.full((RET_QK,), RET_DK ** -0.5, F32),
                                      jnp.ones((2 * RET_V,), F32)])
            w_in = (w_in * kscale[None, :]).astype(BF16)
            lg = jax.nn.log_sigmoid(ret_decay_logit[j])
            lg = jnp.broadcast_to(jnp.transpose(lg)[:, :, None], (RET_HEADS, 2, 128))
            w = (w_in, lg, ret_w_out[j].astype(BF16), tm)
            xc, st = retention_layer(xc, g, mod_c, n_ctx * l_ctx, n_ctx, l_ctx, None, False, *w)
            xl, _ = retention_layer(xl, g, mod_l, l_dec, n_dec, l_dec, state_ret[:, j], True, *w)
            ret_new.append(st)
        else:
            filt_w = (hy_f_w1[j], hy_f_b1[j], hy_f_w2[j], hy_f_b2[j], hy_f_w3[j])
            w = (hy_w_in[j].astype(BF16), hy_conv_w[j], hy_conv_b[j], filt_w, hy_skip[j], hy_w_out[j].astype(BF16), tm)
            xc = hyena_layer(xc, g, mod_c, n_ctx * l_ctx, n_ctx, l_ctx, *w, cb=1024)
            xl = hyena_layer(xl, g, mod_l, l_dec, n_dec, l_dec, *w, cb=256)

    if not last_is_s5:
        xc, xl = final_norm(xc, final_g, tm), final_norm(xl, final_g, tm)
    y_prompt = xc.reshape(n_ctx, l_ctx, D_MODEL)
    y_sample = xl.reshape(n_dec, l_dec, D_MODEL)
    new_state_s5 = jnp.stack(s5_new, axis=1)
    new_state_ret = jnp.stack(ret_new, axis=1)
    return (y_prompt, y_sample, new_state_s5, new_state_ret)
```

```python
import functools
import math

import jax
import jax.numpy as jnp
from jax import lax
from jax.experimental import pallas as pl
from jax.experimental.pallas import tpu as pltpu

F32 = jnp.float32
BF16 = jnp.bfloat16

D_MODEL = 1024
DEPTH = 4
N_MIXERS = 3
EPS = 1e-6
GRID_W = 64

S5_GROUP = 16
S5_GROUPS = D_MODEL // S5_GROUP
S5_STATE = 64
S5_CHUNK = 16
S5_TILE = S5_CHUNK * S5_GROUP
S5_SLAB = 128 // S5_GROUP
S5_SEQS_PER_STEP = 4
S5_ROWS = 512

RET_HEADS = 8
RET_QK = D_MODEL
RET_V = 2 * D_MODEL
RET_DK = RET_QK // RET_HEADS
RET_DV = RET_V // RET_HEADS
RET_BLOCK = 256
ROPE_BASE = 10000.0

HY_WIDTH = 2 * D_MODEL
HY_ORDER = 2
HY_SHORT = 3
HY_BANDS = 16
HY_SHORT_DECAY_PCT = 0.3
HY_LONG_DECAY_PCT = 1.5
HY_DECAY_TARGET = 1e-2
HY_LANES = 128
HY_PARITY_ROWS = 256
HY_ROW_BLOCK = 512

VMEM_LIMIT_BYTES = 56 * 1024 * 1024
OUT_ROWS = 512


def _params(*sem, vmem=None):
    return pltpu.CompilerParams(dimension_semantics=sem, vmem_limit_bytes=vmem)


def _silu(x):
    return x * jax.nn.sigmoid(x)


def _mod_body(c_ref, w_ref, b_ref, o_ref):
    a = _silu(c_ref[...]).astype(BF16)
    o_ref[0] = jnp.dot(a, w_ref[0].astype(BF16), preferred_element_type=F32) + b_ref[0]


def ada_mod_all(cvecs, mod_w, mod_b):
    r = cvecs.shape[0]
    tn = D_MODEL
    return pl.pallas_call(
        _mod_body,
        grid=(DEPTH, 3 * D_MODEL // tn),
        in_specs=[pl.BlockSpec((r, D_MODEL), lambda i, j: (0, 0)),
                  pl.BlockSpec((1, D_MODEL, tn), lambda i, j: (i, 0, j)),
                  pl.BlockSpec((1, 1, tn), lambda i, j: (i, 0, j))],
        out_specs=pl.BlockSpec((1, r, tn), lambda i, j: (i, 0, j)),
        out_shape=jax.ShapeDtypeStruct((DEPTH, r, 3 * D_MODEL), F32),
        compiler_params=_params("parallel", "parallel"),
        name="ada_mod",
    )(cvecs, mod_w, mod_b.reshape(DEPTH, 1, 3 * D_MODEL))


def _inproj_body(x_ref, g_ref, mod_ref, w_ref, o_ref, h_scr):
    @pl.when(pl.program_id(1) == 0)
    def _():
        x = x_ref[...]
        y = x * lax.rsqrt(jnp.mean(x * x, axis=-1, keepdims=True) + EPS) * g_ref[...]
        h_scr[...] = (y * (1.0 + mod_ref[1:2, :]) + mod_ref[0:1, :]).astype(BF16)

    o_ref[...] = jnp.dot(h_scr[...], w_ref[...], preferred_element_type=F32).astype(o_ref.dtype)


def in_proj(x, g, mod, w, rows_per_mod, tm, tn):
    n, nout = x.shape[0], w.shape[1]
    return pl.pallas_call(
        _inproj_body,
        grid=(n // tm, nout // tn),
        in_specs=[pl.BlockSpec((tm, D_MODEL), lambda i, j: (i, 0)),
                  pl.BlockSpec((1, D_MODEL), lambda i, j: (0, 0)),
                  pl.BlockSpec((None, 3, D_MODEL), lambda i, j: ((i * tm) // rows_per_mod, 0, 0)),
                  pl.BlockSpec((D_MODEL, tn), lambda i, j: (0, j))],
        out_specs=pl.BlockSpec((tm, tn), lambda i, j: (i, j)),
        out_shape=jax.ShapeDtypeStruct((n, nout), BF16),
        scratch_shapes=[pltpu.VMEM((tm, D_MODEL), BF16)],
        compiler_params=_params("parallel", "arbitrary", vmem=VMEM_LIMIT_BYTES),
        name="in_proj",
    )(x, g.reshape(1, D_MODEL), mod, w)


def _outproj_body(y_ref, w_ref, x_ref, mod_ref, o_ref):
    out = jnp.dot(y_ref[...], w_ref[...], preferred_element_type=F32)
    o_ref[...] = x_ref[...] + mod_ref[2:3, :] * out


def out_proj(y, w, x, mod, rows_per_mod, tm):
    n, width = y.shape
    return pl.pallas_call(
        _outproj_body,
        grid=(n // tm,),
        in_specs=[pl.BlockSpec((tm, width), lambda i: (i, 0)),
                  pl.BlockSpec((width, D_MODEL), lambda i: (0, 0)),
                  pl.BlockSpec((tm, D_MODEL), lambda i: (i, 0)),
                  pl.BlockSpec((None, 3, D_MODEL), lambda i: ((i * tm) // rows_per_mod, 0, 0))],
        out_specs=pl.BlockSpec((tm, D_MODEL), lambda i: (i, 0)),
        out_shape=jax.ShapeDtypeStruct((n, D_MODEL), F32),
        compiler_params=_params("parallel", vmem=VMEM_LIMIT_BYTES),
        name="out_proj",
    )(y, w, x, mod)


def _final_norm_body(x_ref, g_ref, o_ref):
    x = x_ref[...]
    o_ref[...] = x * lax.rsqrt(jnp.mean(x * x, axis=-1, keepdims=True) + EPS) * g_ref[...]


def final_norm(x, g, tm):
    n = x.shape[0]
    return pl.pallas_call(
        _final_norm_body,
        grid=(n // tm,),
        in_specs=[pl.BlockSpec((tm, D_MODEL), lambda i: (i, 0)),
                  pl.BlockSpec((1, D_MODEL), lambda i: (0, 0))],
        out_specs=pl.BlockSpec((tm, D_MODEL), lambda i: (i, 0)),
        out_shape=jax.ShapeDtypeStruct((n, D_MODEL), F32),
        compiler_params=_params("parallel"),
        name="final_norm",
    )(x, g.reshape(1, D_MODEL))


def s5_tables(lam_re, lam_im, log_step, b_re, b_im, c_re, c_im, d_skip):
    t_len = S5_CHUNK
    dt = jnp.exp(log_step)[..., None]
    ab_re = jnp.exp(lam_re * dt) * jnp.cos(lam_im * dt)
    ab_im = jnp.exp(lam_re * dt) * jnp.sin(lam_im * dt)
    den = lam_re * lam_re + lam_im * lam_im
    nr, ni = ab_re - 1.0, ab_im
    f_re = (nr * lam_re + ni * lam_im) / den
    f_im = (ni * lam_re - nr * lam_im) / den
    bb_re = f_re[..., None] * b_re - f_im[..., None] * b_im
    bb_im = f_re[..., None] * b_im + f_im[..., None] * b_re
    ks = jnp.arange(t_len + 1, dtype=F32)[:, None, None, None]
    pw_mag = jnp.exp(ks * (lam_re * dt)[None])
    pw_re = pw_mag * jnp.cos(ks * (lam_im * dt)[None])
    pw_im = pw_mag * jnp.sin(ks * (lam_im * dt)[None])

    npair = S5_GROUPS // 2
    pair_eye = jnp.eye(2, dtype=F32)

    def pair_pack(a):
        a = a.reshape(2, npair, 2, S5_GROUP, S5_STATE)
        a = a[:, :, :, :, None, :] * pair_eye[None, None, :, None, :, None]
        return jnp.transpose(a.reshape(2, npair, 2, S5_GROUP, 2 * S5_STATE), (1, 0, 2, 3, 4))

    bbt = jnp.stack([pair_pack(jnp.swapaxes(bb_re, 2, 3)), pair_pack(jnp.swapaxes(bb_im, 2, 3))], axis=2)
    cpk = jnp.stack([pair_pack(c_re), pair_pack(c_im)], axis=2)
    pw = jnp.stack([pw_re, pw_im], axis=0).reshape(2, t_len + 1, 2, npair, 2 * S5_STATE)
    pw = jnp.transpose(pw, (3, 2, 0, 1, 4))
    dmat = jnp.eye(S5_GROUP, dtype=F32) * d_skip.reshape(S5_GROUPS, 1, S5_GROUP)
    dmat = jnp.pad(dmat, ((0, 0), (0, 0), (0, S5_TILE - S5_GROUP))).reshape(npair, 2, S5_GROUP, S5_TILE)
    m, f, et = s5_table_kernel(bbt, cpk, pw, dmat)

    dec = jnp.stack([pw_re[t_len, 0], pw_im[t_len, 0], pw_re[t_len, 1], pw_im[t_len, 1]], axis=0)
    dec = jnp.transpose(dec.reshape(4, S5_GROUPS // S5_SLAB, S5_SLAB * S5_STATE), (1, 0, 2))
    return m.reshape(S5_GROUPS, S5_TILE, S5_TILE), f, et, dec


def _s5_table_body(bb_ref, c_ref, pw_ref, d_ref, m_ref, f_ref, et_ref):
    t_len = S5_CHUNK
    nt_dims = (((1,), (1,)), ((), ()))
    hp = lax.Precision.HIGHEST

    def times_power(ref, d, g, k):
        pr, pi = pw_ref[d, 0, k:k + 1, :], pw_ref[d, 1, k:k + 1, :]
        ar, ai = ref[d, 0, g], ref[d, 1, g]
        return ar * pr - ai * pi, ar * pi + ai * pr

    def stack(parts):
        return jnp.concatenate(parts, axis=0)

    for d in range(2):
        f_pow = [t_len - 1 - s for s in range(t_len)] if d == 0 else list(range(t_len))
        e_pow = [s + 1 for s in range(t_len)] if d == 0 else [t_len - s for s in range(t_len)]
        fr, fi, er, ei = [], [], [], []
        for g in range(2):
            for s in range(t_len):
                a, b = times_power(bb_ref, d, g, f_pow[s])
                fr.append(a)
                fi.append(b)
                a, b = times_power(c_ref, d, g, e_pow[s])
                er.append(a)
                ei.append(-b)
        lanes = 2 * S5_STATE
        f_ref[:, (2 * d) * lanes:(2 * d + 1) * lanes] = stack(fr).astype(f_ref.dtype)
        f_ref[:, (2 * d + 1) * lanes:(2 * d + 2) * lanes] = stack(fi).astype(f_ref.dtype)
        et_ref[:, (2 * d) * lanes:(2 * d + 1) * lanes] = stack(er).astype(et_ref.dtype)
        et_ref[:, (2 * d + 1) * lanes:(2 * d + 2) * lanes] = stack(ei).astype(et_ref.dtype)

    zeros = jnp.zeros((S5_GROUP, S5_TILE), F32)
    for g in range(2):
        def lag_kernels(d, powers):
            car, cai = zip(*(times_power(c_ref, d, g, k) for k in powers))
            return (lax.dot_general(bb_ref[d, 0, g], stack(car), nt_dims, precision=hp, preferred_element_type=F32)
                    - lax.dot_general(bb_ref[d, 1, g], stack(cai), nt_dims, precision=hp,
                                      preferred_element_type=F32))
        v_f = jnp.concatenate([zeros, lag_kernels(0, range(t_len)) + d_ref[g]], axis=1)
        v_b = jnp.concatenate([lag_kernels(1, range(t_len - 1, -1, -1)), zeros], axis=1)
        width = 2 * S5_TILE
        for s in range(t_len):
            blk_f = pltpu.roll(v_f, (width - (t_len - s) * S5_GROUP) % width, 1)[:, :S5_TILE]
            blk_b = pltpu.roll(v_b, (width - (t_len - 1 - s) * S5_GROUP) % width, 1)[:, :S5_TILE]
            m_ref[g, s * S5_GROUP:(s + 1) * S5_GROUP, :] = (blk_f + blk_b).astype(m_ref.dtype)


def s5_table_kernel(bbt, cpk, pw, dmat):
    npair = S5_GROUPS // 2
    pspec = pl.BlockSpec((None, 2, 2, 2, S5_GROUP, 2 * S5_STATE), lambda k: (k, 0, 0, 0, 0, 0))
    wide = pl.BlockSpec((None, 2 * S5_TILE, 2 * S5_TILE), lambda k: (k, 0, 0))
    return pl.pallas_call(
        _s5_table_body,
        grid=(npair,),
        in_specs=[pspec, pspec,
                  pl.BlockSpec((None, 2, 2, S5_CHUNK + 1, 2 * S5_STATE), lambda k: (k, 0, 0, 0, 0)),
                  pl.BlockSpec((None, 2, S5_GROUP, S5_TILE), lambda k: (k, 0, 0, 0))],
        out_specs=[pl.BlockSpec((None, 2, S5_TILE, S5_TILE), lambda k: (k, 0, 0, 0)), wide, wide],
        out_shape=[jax.ShapeDtypeStruct((npair, 2, S5_TILE, S5_TILE), BF16),
                   jax.ShapeDtypeStruct((npair, 2 * S5_TILE, 2 * S5_TILE), BF16),
                   jax.ShapeDtypeStruct((npair, 2 * S5_TILE, 2 * S5_TILE), BF16)],
        compiler_params=_params("parallel"),
        name="s5_tables",
    )(bbt, cpk, pw, dmat)


def s5_lane_permutation():
    width = S5_CHUNK * S5_SLAB * S5_GROUP
    i = jnp.arange(width)
    step, grp, ch = i // (S5_SLAB * S5_GROUP), (i // S5_GROUP) % S5_SLAB, i % S5_GROUP
    dst = grp * S5_TILE + step * S5_GROUP + ch
    return (dst[:, None] == jnp.arange(width)[None, :]).astype(BF16)


def _s5_scan_body(u_ref, q_ref, m_ref, f_ref, et_ref, dec_ref, h0_ref, y_ref, hf_ref,
                  uy_s, st_s, *, nchunk, nseq):
    npair = S5_SLAB // 2
    pair_w = 2 * S5_TILE
    lane_w = 2 * S5_STATE
    nt_dims = (((1,), (1,)), ((), ()))
    xcat = jnp.concatenate([u_ref[s] for s in range(S5_CHUNK)], axis=1)
    for k in range(npair):
        cols = slice(k * pair_w, (k + 1) * pair_w)
        up = jnp.dot(xcat, q_ref[:, cols], preferred_element_type=F32).astype(BF16)
        uy_s[:, cols] = up
        x = jnp.dot(up, f_ref[k], preferred_element_type=F32)
        for q in range(4):
            st_s[q * npair + k] = x[:, q * lane_w:(q + 1) * lane_w]

    def lanes_of(ref, q, k):
        return ref[q, :, k * lane_w:(k + 1) * lane_w]

    dec = [[dec_ref[q:q + 1, k * lane_w:(k + 1) * lane_w] for k in range(npair)] for q in range(4)]

    def step(i, carry):
        rows_f = pl.ds(i, nseq, stride=nchunk)
        rows_b = pl.ds(nchunk - 1 - i, nseq, stride=nchunk)
        new = []
        for k in range(npair):
            fr, fi, br, bi = (carry[q * npair + k] for q in range(4))
            xr, xi = st_s[0 * npair + k, rows_f, :], st_s[1 * npair + k, rows_f, :]
            yr, yi = st_s[2 * npair + k, rows_b, :], st_s[3 * npair + k, rows_b, :]
            st_s[0 * npair + k, rows_f, :] = fr
            st_s[1 * npair + k, rows_f, :] = fi
            st_s[2 * npair + k, rows_b, :] = br
            st_s[3 * npair + k, rows_b, :] = bi
            new.append((dec[0][k] * fr - dec[1][k] * fi + xr, dec[0][k] * fi + dec[1][k] * fr + xi,
                        dec[2][k] * br - dec[3][k] * bi + yr, dec[2][k] * bi + dec[3][k] * br + yi))
        return tuple(new[k][q] for q in range(4) for k in range(npair))

    init = tuple(lanes_of(h0_ref, q, k) for q in range(4) for k in range(npair))
    final = lax.fori_loop(0, nchunk, step, init)
    for q in range(4):
        for k in range(npair):
            hf_ref[q, :, k * lane_w:(k + 1) * lane_w] = final[q * npair + k]

    for k in range(npair):
        h = jnp.concatenate([st_s[q * npair + k] for q in range(4)], axis=1)
        ye = lax.dot_general(h.astype(BF16), et_ref[k], nt_dims, preferred_element_type=F32)
        for g in range(2):
            cols = slice(k * pair_w + g * S5_TILE, k * pair_w + (g + 1) * S5_TILE)
            y = jnp.dot(uy_s[:, cols], m_ref[2 * k + g], preferred_element_type=F32)
            uy_s[:, cols] = (y + ye[:, g * S5_TILE:(g + 1) * S5_TILE]).astype(BF16)
    slab_w = S5_SLAB * S5_GROUP
    steps = pair_w // slab_w
    for t0 in range(0, S5_CHUNK, steps):
        yp = lax.dot_general(uy_s[...], q_ref[t0 * slab_w:(t0 + steps) * slab_w, :], nt_dims,
                             preferred_element_type=F32).astype(y_ref.dtype)
        for t in range(steps):
            y_ref[t0 + t] = yp[:, t * slab_w:(t + 1) * slab_w]


def s5_scan(u_s, perm, m, f, e, dec, h0, nchunk, nseq, nb):
    rows = nchunk * nb
    nslab = S5_GROUPS // S5_SLAB
    slab_w = S5_SLAB * S5_GROUP
    state_w = S5_SLAB * S5_STATE
    width = S5_CHUNK * slab_w
    body = functools.partial(_s5_scan_body, nchunk=nchunk, nseq=nb)
    hspec = pl.BlockSpec((None, None, 4, nb, state_w), lambda s, b: (s, b, 0, 0, 0))
    state = pltpu.VMEM((2 * S5_SLAB, rows, 2 * S5_STATE), F32)
    return pl.pallas_call(
        body,
        grid=(nslab, nseq // nb),
        in_specs=[pl.BlockSpec((S5_CHUNK, rows, slab_w), lambda s, b: (0, b, s)),
                  pl.BlockSpec((width, width), lambda s, b: (0, 0), pipeline_mode=pl.Buffered(1)),
                  pl.BlockSpec((S5_SLAB, S5_TILE, S5_TILE), lambda s, b: (s, 0, 0)),
                  pl.BlockSpec((S5_SLAB // 2, 2 * S5_TILE, 2 * S5_TILE), lambda s, b: (s, 0, 0)),
                  pl.BlockSpec((S5_SLAB // 2, 2 * S5_TILE, 2 * S5_TILE), lambda s, b: (s, 0, 0)),
                  pl.BlockSpec((None, 4, state_w), lambda s, b: (s, 0, 0)),
                  hspec],
        out_specs=[pl.BlockSpec((S5_CHUNK, rows, slab_w), lambda s, b: (0, b, s)), hspec],
        out_shape=[jax.ShapeDtypeStruct((S5_CHUNK, nseq * nchunk, D_MODEL), BF16),
                   jax.ShapeDtypeStruct((nslab, nseq // nb, 4, nb, state_w), F32)],
        scratch_shapes=[pltpu.VMEM((rows, width), BF16), state],
        compiler_params=_params("parallel", "parallel", vmem=VMEM_LIMIT_BYTES),
        name="s5_scan",
    )(u_s, perm, m, f, e, dec, h0)


def s5_row_permutation():
    i = jnp.arange(S5_ROWS)
    src = (i % (S5_ROWS // S5_CHUNK)) * S5_CHUNK + i // (S5_ROWS // S5_CHUNK)
    p = (src[:, None] == jnp.arange(S5_ROWS)[None, :]).astype(BF16)
    return p, jnp.transpose(p)


def _inproj_steps_body(x_ref, g_ref, mod_ref, p_ref, w_ref, o_ref):
    x = x_ref[...]
    y = x * lax.rsqrt(jnp.mean(x * x, axis=-1, keepdims=True) + EPS) * g_ref[...]
    h = (y * (1.0 + mod_ref[1:2, :]) + mod_ref[0:1, :]).astype(BF16)
    h = jnp.dot(p_ref[...], h, preferred_element_type=F32).astype(BF16)
    out = jnp.dot(h, w_ref[...], preferred_element_type=F32).astype(o_ref.dtype)
    o_ref[...] = out.reshape(o_ref.shape)


def in_proj_steps(x, g, mod, rows_per_mod, p, w):
    n, nout = x.shape[0], w.shape[1]
    tm = S5_ROWS
    return pl.pallas_call(
        _inproj_steps_body,
        grid=(n // tm,),
        in_specs=[pl.BlockSpec((tm, D_MODEL), lambda i: (i, 0)),
                  pl.BlockSpec((1, D_MODEL), lambda i: (0, 0)),
                  pl.BlockSpec((None, 3, D_MODEL), lambda i: ((i * tm) // rows_per_mod, 0, 0)),
                  pl.BlockSpec((tm, tm), lambda i: (0, 0)),
                  pl.BlockSpec((D_MODEL, nout), lambda i: (0, 0))],
        out_specs=pl.BlockSpec((S5_CHUNK, tm // S5_CHUNK, nout), lambda i: (0, i, 0)),
        out_shape=jax.ShapeDtypeStruct((S5_CHUNK, n // S5_CHUNK, nout), BF16),
        compiler_params=_params("parallel", vmem=VMEM_LIMIT_BYTES),
        name="in_proj_steps",
    )(x, g.reshape(1, D_MODEL), mod, p, w)


def _s5_out_body(y_ref, gate_ref, wg_ref, bg_ref, wo_ref, pt_ref, x_ref, mod_ref, *rest):
    o_ref = rest[-1]
    rows = x_ref.shape[0]
    g = jax.nn.gelu(y_ref[...].astype(F32).reshape(rows, D_MODEL))
    t = jnp.dot(g.astype(BF16), wg_ref[...], preferred_element_type=F32) + bg_ref[...]
    z = g * jax.nn.sigmoid(t) * _silu(gate_ref[...].astype(F32).reshape(rows, D_MODEL))
    z = jnp.dot(pt_ref[...], z.astype(BF16), preferred_element_type=F32).astype(BF16)
    out = jnp.dot(z, wo_ref[...], preferred_element_type=F32)
    x = x_ref[...] + mod_ref[2:3, :] * out
    if len(rest) == 2:
        x = x * lax.rsqrt(jnp.mean(x * x, axis=-1, keepdims=True) + EPS) * rest[0][...]
    o_ref[...] = x


def s5_out(y_s, u_s, w_glu, b_glu, w_out, pt, x, mod, rows_per_mod, final_g=None):
    n = x.shape[0]
    tm = S5_ROWS
    xspec = pl.BlockSpec((tm, D_MODEL), lambda i: (i, 0))
    wspec = pl.BlockSpec((D_MODEL, D_MODEL), lambda i: (0, 0))
    vspec = pl.BlockSpec((1, D_MODEL), lambda i: (0, 0))
    in_specs = [pl.BlockSpec((S5_CHUNK, tm // S5_CHUNK, D_MODEL), lambda i: (0, i, 0)),
                pl.BlockSpec((S5_CHUNK, tm // S5_CHUNK, D_MODEL), lambda i: (0, i, 1)),
                wspec, vspec, wspec,
                pl.BlockSpec((tm, tm), lambda i: (0, 0)),
                xspec,
                pl.BlockSpec((None, 3, D_MODEL), lambda i: ((i * tm) // rows_per_mod, 0, 0))]
    args = [y_s, u_s, w_glu, b_glu.reshape(1, D_MODEL), w_out, pt, x, mod]
    if final_g is not None:
        in_specs.append(vspec)
        args.append(final_g.reshape(1, D_MODEL))
    return pl.pallas_call(
        _s5_out_body,
        grid=(n // tm,),
        in_specs=in_specs,
        out_specs=xspec,
        out_shape=jax.ShapeDtypeStruct(x.shape, F32),
        compiler_params=_params("parallel", vmem=VMEM_LIMIT_BYTES),
        name="s5_out",
    )(*args)


def s5_layer(x, g, mod, rows_per_mod, nseq, seq_len, h0, w_in, tables, perms, w_glu, b_glu, w_out, nb,
             final_g=None):
    nchunk = seq_len // S5_CHUNK
    nslab = S5_GROUPS // S5_SLAB
    state_w = S5_SLAB * S5_STATE
    perm, row_p, row_pt = perms
    u_s = in_proj_steps(x, g, mod, rows_per_mod, row_p, w_in)
    if h0 is None:
        h0_p = jnp.zeros((nslab, nseq // nb, 4, nb, state_w), F32)
    else:
        h0_p = jnp.transpose(h0.reshape(nseq // nb, nb, 4, nslab, state_w), (3, 0, 2, 1, 4))
    m, f, e, dec = tables
    y_s, hf = s5_scan(u_s, perm, m, f, e, dec, h0_p, nchunk, nseq, nb)
    x_new = s5_out(y_s, u_s, w_glu, b_glu, w_out, row_pt, x, mod, rows_per_mod, final_g)
    states = jnp.transpose(hf, (1, 3, 2, 0, 4)).reshape(nseq, 2, 2, S5_GROUPS, S5_STATE)
    return x_new, states


def _ret_body(*refs, seq_len, rope, has_init, want_state):
    refs = list(refs)
    q_ref, k_ref, v_ref, gate_ref, lg_ref = refs[:5]
    pos = 5
    if rope:
        cos_ref, sin_ref = refs[pos:pos + 2]
        pos += 2
    if has_init:
        s0_ref = refs[pos]
        pos += 1
    o_ref = refs[pos]
    pos += 1
    if want_state:
        sfin_ref = refs[pos]
        pos += 1
    sf_scr = refs[pos]

    c_len = RET_BLOCK
    nblk = seq_len // c_len
    lgf = lg_ref[0:1, 0:1]
    lgb = lg_ref[1:2, 0:1]
    ii = lax.broadcasted_iota(jnp.int32, (c_len, c_len), 0)
    jj = lax.broadcasted_iota(jnp.int32, (c_len, c_len), 1)
    diff = (ii - jj).astype(F32)
    decay = jnp.exp(jnp.abs(diff) * jnp.where(diff >= 0, lgf, lgb))
    ic = lax.broadcasted_iota(jnp.int32, (c_len, 1), 0).astype(F32)
    q_dec_f = jnp.exp((ic + 1.0) * lgf)
    q_dec_b = jnp.exp((c_len - ic) * lgb)
    k_dec_f = jnp.exp((c_len - 1.0 - ic) * lgf)
    k_dec_b = jnp.exp(ic * lgb)
    c_dec_f = jnp.exp(c_len * lgf)
    c_dec_b = jnp.exp(c_len * lgb)

    q = q_ref[...].astype(F32)
    k = k_ref[...].astype(F32)
    if rope:
        lane = lax.broadcasted_iota(jnp.int32, (seq_len, RET_DK), 1)
        first = (lane % (RET_DK // 2)) < (RET_DK // 4)
        cos = cos_ref[...]
        sin = sin_ref[...]

        def rot(x):
            swapped = jnp.where(first, pltpu.roll(x, RET_DK - RET_DK // 4, 1), pltpu.roll(x, RET_DK // 4, 1))
            return x * cos + swapped * sin

        q = rot(q)
        k = rot(k)

    def blk(a, c):
        return a[c * c_len:(c + 1) * c_len]

    def kv_state(kd, c):
        kt = jnp.transpose(blk(k, c) * kd).astype(BF16)
        return jnp.dot(kt, v_ref[c * c_len:(c + 1) * c_len, :], preferred_element_type=F32)

    use_state = has_init or nblk > 1
    s_f = s0_ref[0] if has_init else jnp.zeros((RET_DK, RET_DV), F32)
    for c in range(nblk):
        if use_state:
            sf_scr[c] = s_f
        if want_state or c < nblk - 1:
            s_f = c_dec_f * s_f + kv_state(k_dec_f, c)
    if want_state:
        sfin_ref[0] = s_f

    s_b = s0_ref[1] if has_init else jnp.zeros((RET_DK, RET_DV), F32)
    for c in range(nblk - 1, -1, -1):
        qc = blk(q, c)
        kc = blk(k, c)
        vc = v_ref[c * c_len:(c + 1) * c_len, :]
        s = lax.dot_general(qc.astype(BF16), kc.astype(BF16), (((1,), (1,)), ((), ())),
                            preferred_element_type=F32)
        o = jnp.dot((s * decay).astype(BF16), vc, preferred_element_type=F32)
        if use_state:
            o = o + jnp.dot((qc * q_dec_f).astype(BF16), sf_scr[c].astype(BF16), preferred_element_type=F32)
            o = o + jnp.dot((qc * q_dec_b).astype(BF16), s_b.astype(BF16), preferred_element_type=F32)
        o = o * lax.rsqrt(jnp.mean(o * o, axis=-1, keepdims=True) + EPS)
        gate = gate_ref[c * c_len:(c + 1) * c_len, :].astype(F32)
        o_ref[c * c_len:(c + 1) * c_len, :] = (o * _silu(gate)).astype(o_ref.dtype)
        if want_state or c > 0:
            s_b = c_dec_b * s_b + kv_state(k_dec_b, c)
    if want_state:
        sfin_ref[1] = s_b


def retention_core(proj3, lg, rope_tabs, s0, want_state):
    nseq, seq_len, _ = proj3.shape
    nblk = seq_len // RET_BLOCK
    rope = rope_tabs is not None
    has_init = s0 is not None
    kq = RET_QK // RET_DK
    in_specs = [pl.BlockSpec((None, seq_len, RET_DK), lambda b, h: (b, 0, h)),
                pl.BlockSpec((None, seq_len, RET_DK), lambda b, h: (b, 0, kq + h)),
                pl.BlockSpec((None, seq_len, RET_DV), lambda b, h: (b, 0, 2 * RET_QK // RET_DV + h)),
                pl.BlockSpec((None, seq_len, RET_DV), lambda b, h: (b, 0, (2 * RET_QK + RET_V) // RET_DV + h)),
                pl.BlockSpec((None, 2, 128), lambda b, h: (h, 0, 0))]
    args = [proj3, proj3, proj3, proj3, lg]
    if rope:
        tab = pl.BlockSpec((seq_len, RET_DK), lambda b, h: (0, 0))
        in_specs += [tab, tab]
        args += list(rope_tabs)
    sspec = pl.BlockSpec((None, 2, None, RET_DK, RET_DV), lambda b, h: (b, 0, h, 0, 0))
    if has_init:
        in_specs.append(sspec)
        args.append(s0)
    out_specs = [pl.BlockSpec((None, seq_len, RET_DV), lambda b, h: (b, 0, h))]
    out_shape = [jax.ShapeDtypeStruct((nseq, seq_len, RET_V), BF16)]
    if want_state:
        out_specs.append(sspec)
        out_shape.append(jax.ShapeDtypeStruct((nseq, 2, RET_HEADS, RET_DK, RET_DV), F32))
    body = functools.partial(_ret_body, seq_len=seq_len, rope=rope, has_init=has_init, want_state=want_state)
    res = pl.pallas_call(
        body,
        grid=(nseq, RET_HEADS),
        in_specs=in_specs,
        out_specs=out_specs,
        out_shape=out_shape,
        scratch_shapes=[pltpu.VMEM((nblk, RET_DK, RET_DV), F32)],
        compiler_params=_params("parallel", "parallel", vmem=VMEM_LIMIT_BYTES),
        name="retention",
    )(*args)
    return res[0], (res[1] if want_state else None)


def rope_tables(seq_len):
    t = jnp.arange(seq_len)
    half = RET_DK // 2
    nfreq = half // 2
    inv = ROPE_BASE ** (-jnp.arange(nfreq, dtype=F32) / nfreq)
    parts_c, parts_s = [], []
    for p in (t // GRID_W, t % GRID_W):
        ang = p.astype(F32)[:, None] * inv[None, :]
        parts_c += [jnp.cos(ang), jnp.cos(ang)]
        parts_s += [-jnp.sin(ang), jnp.sin(ang)]
    return jnp.concatenate(parts_c, axis=-1), jnp.concatenate(parts_s, axis=-1)


def retention_layer(x, g, mod, rows_per_mod, nseq, seq_len, s0, grid_pos, w_in, lg, w_out, tm):
    proj = in_proj(x, g, mod, w_in, rows_per_mod, tm, D_MODEL)
    proj3 = proj.reshape(nseq, seq_len, 2 * RET_QK + 2 * RET_V)
    tabs = rope_tables(seq_len) if grid_pos else None
    y, states = retention_core(proj3, lg, tabs, s0, want_state=s0 is None)
    x_new = out_proj(y.reshape(nseq * seq_len, RET_V), w_out, x, mod, rows_per_mod, min(tm, OUT_ROWS))
    return x_new, states


def _filter_dft_body(a_ref, b_ref, norm_ref, o_ref, *, seq_len, tm):
    row = pl.program_id(0) * tm + lax.broadcasted_iota(jnp.int32, (tm, 1), 0)
    wf = jnp.where(row % (seq_len // 2) == 0, 1.0, 2.0) * (1.0 / (2.0 * seq_len))
    acc = jnp.dot(a_ref[...], b_ref[...], preferred_element_type=F32)
    o_ref[...] = acc * wf / (norm_ref[...] + EPS)


def filter_dft(a, b, norm, tm, tn):
    m, kd = a.shape
    n = b.shape[1]
    body = functools.partial(_filter_dft_body, seq_len=m, tm=tm)
    return pl.pallas_call(
        body,
        grid=(m // tm, n // tn),
        in_specs=[pl.BlockSpec((tm, kd), lambda i, j: (i, 0)),
                  pl.BlockSpec((kd, tn), lambda i, j: (0, j)),
                  pl.BlockSpec((1, tn), lambda i, j: (0, j))],
        out_specs=pl.BlockSpec((tm, tn), lambda i, j: (i, j)),
        out_shape=jax.ShapeDtypeStruct((m, n), F32),
        compiler_params=_params("parallel", "parallel", vmem=VMEM_LIMIT_BYTES),
        name="filter_dft",
    )(a, b, norm)


def _filter_gen_body(feat_ref, w1_ref, b1_ref, w2_ref, b2_ref, w3f_ref, w3b_ref, dl_ref,
                     fs_ref, fd_ref, norm_ref, mre_ref, mim_ref, *, seq_len, tr):
    r = pl.program_id(1)
    hp = lax.Precision.HIGHEST
    z = jnp.sin(jnp.dot(feat_ref[...], w1_ref[...], precision=hp, preferred_element_type=F32) + b1_ref[...])
    z = jnp.sin(jnp.dot(z, w2_ref[...], precision=hp, preferred_element_type=F32) + b2_ref[...]).astype(BF16)
    lag = r * tr + lax.broadcasted_iota(jnp.int32, (tr, 1), 0)
    t = lag.astype(F32) * (1.0 / (seq_len - 1))
    win = jnp.exp(-t * jnp.abs(dl_ref[...]))
    ff = jnp.dot(z, w3f_ref[...], preferred_element_type=F32) * win
    fb = jnp.dot(z, w3b_ref[...], preferred_element_type=F32) * win
    part = jnp.sum(jnp.abs(ff) + jnp.abs(fb), axis=0, keepdims=True)
    fb = jnp.where(lag == 0, 0.0, fb)
    fs = ff + fb
    fd = ff - fb
    quarter = jnp.where(lag % 4 < 2, 1.0, -1.0)
    cos_q = jnp.where(lag % 2 == 0, quarter, 0.0)
    sin_q = jnp.where(lag % 2 == 1, quarter, 0.0)

    @pl.when(r == 0)
    def _():
        norm_ref[...] = jnp.zeros_like(norm_ref)
        mre_ref[...] = jnp.zeros_like(mre_ref)
        mim_ref[...] = jnp.zeros_like(mim_ref)

    norm_ref[...] += part
    mre_ref[...] += jnp.sum(cos_q * fs, axis=0, keepdims=True)
    mim_ref[...] -= jnp.sum(sin_q * fd, axis=0, keepdims=True)
    fs_ref[...] = fs.astype(fs_ref.dtype)
    fd_ref[...] = fd.astype(fd_ref.dtype)


def filter_gen(feat, w1, b1, w2, b2, w3, deltas, tr, tn):
    seq_len, kf = feat.shape
    hid = w1.shape[1]
    ncol = HY_ORDER * HY_WIDTH
    nj = ncol // tn
    body = functools.partial(_filter_gen_body, seq_len=seq_len, tr=tr)
    full = lambda shape: pl.BlockSpec(shape, lambda j, r: (0, 0))
    row = pl.BlockSpec((1, tn), lambda j, r: (0, j))
    return pl.pallas_call(
        body,
        grid=(nj, seq_len // tr),
        in_specs=[pl.BlockSpec((tr, kf), lambda j, r: (r, 0)),
                  full((kf, hid)), full((1, hid)), full((hid, hid)), full((1, hid)),
                  pl.BlockSpec((hid, tn), lambda j, r: (0, j)),
                  pl.BlockSpec((hid, tn), lambda j, r: (0, nj + j)),
                  pl.BlockSpec((1, tn), lambda j, r: (0, j % (HY_WIDTH // tn)))],
        out_specs=[pl.BlockSpec((tr, tn), lambda j, r: (r, j)),
                   pl.BlockSpec((tr, tn), lambda j, r: (r, j)), row, row, row],
        out_shape=[jax.ShapeDtypeStruct((seq_len, ncol), BF16), jax.ShapeDtypeStruct((seq_len, ncol), BF16)]
        + [jax.ShapeDtypeStruct((1, ncol), F32)] * 3,
        compiler_params=_params("parallel", "arbitrary", vmem=VMEM_LIMIT_BYTES),
        name="filter_gen",
    )(feat, w1, b1, w2, b2, w3, w3, deltas)


def _cos_sin(freq, time, seq_len):
    prod = (freq[:, None] * time[None, :]) % (2 * seq_len)
    ang = prod.astype(F32) * (math.pi / seq_len)
    return jnp.cos(ang).astype(BF16), (-jnp.sin(ang)).astype(BF16)


def dft_matrices(seq_len):
    half = seq_len // 2
    lo = jnp.arange(half, dtype=jnp.int32)
    filt = _cos_sin(jnp.concatenate([lo, seq_len - lo]), jnp.arange(seq_len, dtype=jnp.int32), seq_len)
    c_e, s_e = _cos_sin(lo, 2 * lo, seq_len)
    c_o, s_o = _cos_sin(lo, 2 * lo + 1, seq_len)
    return filt, (c_e, c_o, s_e, s_o, jnp.transpose(c_o), jnp.transpose(s_o))


def hyena_filter_spectrum(seq_len, cm, sm, w1, b1, w2, b2, w3):
    t = jnp.linspace(0.0, 1.0, seq_len, dtype=F32)[:, None]
    w = 2.0 * math.pi * jnp.arange(seq_len, dtype=F32)[:, None] / seq_len
    f = jnp.linspace(1e-4, HY_BANDS - 1.0, HY_BANDS, dtype=F32)[None, :]
    feat = jnp.concatenate([t, jnp.cos(f * w), -jnp.sin(f * w)], axis=-1)
    emb, hid = w1.shape
    pe, ph = HY_LANES - emb, HY_LANES - hid
    feat = jnp.pad(feat, ((0, 0), (0, pe)))
    w1p = jnp.pad(w1, ((0, pe), (0, ph)))
    w2p = jnp.pad(w2, ((0, ph), (0, ph)))
    w3p = jnp.pad(w3, ((0, ph), (0, 0))).astype(BF16)
    b1p = jnp.pad(b1, (0, ph)).reshape(1, HY_LANES)
    b2p = jnp.pad(b2, (0, ph)).reshape(1, HY_LANES)
    max_decay = math.log(HY_DECAY_TARGET) / HY_SHORT_DECAY_PCT
    min_decay = math.log(HY_DECAY_TARGET) / HY_LONG_DECAY_PCT
    deltas = jnp.linspace(min_decay, max_decay, HY_WIDTH, dtype=F32).reshape(1, HY_WIDTH)
    tile = min(seq_len, 512)
    fs, fd, norm, mid_re, mid_im = filter_gen(feat, w1p, b1p, w2p, b2p, w3p, deltas, tile, 1024)
    kr = filter_dft(cm, fs, norm, tile, 512)
    ki = filter_dft(sm, fd, norm, tile, 512)
    kmid = jnp.stack([mid_re, mid_im], axis=0) / (norm + EPS) / seq_len
    kmid = jnp.transpose(kmid.reshape(2, HY_ORDER, HY_WIDTH), (1, 0, 2)).reshape(2 * HY_ORDER, HY_WIDTH)
    return kr, ki, kmid


def _hyena_body(v_ref, x1_ref, x2_ref, gate_ref, cw_ref, cb_ref, kr0_ref, ki0_ref, kr1_ref, ki1_ref,
                km_ref, sk_ref, ce_ref, co_ref, se_ref, so_ref, cot_ref, sot_ref, o_ref,
                z_scr, xg_scr, p_scr, *, seq_len):
    half = seq_len // 2
    cb = o_ref.shape[-1]
    row = lax.broadcasted_iota(jnp.int32, (half, 1), 0)
    alt = jnp.where(row % 2 == 0, 1.0, -1.0).astype(F32)

    def halves(ref):
        return (ref[:, 0].astype(F32).reshape(half, cb), ref[:, 1].astype(F32).reshape(half, cb))

    def short_conv(x_ref, which, dst):
        xe, xo = halves(x_ref)
        w0, w1, w2 = (cw_ref[j, which:which + 1, :] for j in range(HY_SHORT))
        bias = cb_ref[which:which + 1, :]
        xo_prev = jnp.where(row == 0, 0.0, pltpu.roll(xo, 1, 0))
        xe_next = jnp.where(row == half - 1, 0.0, pltpu.roll(xe, half - 1, 0))
        dst[0] = bias + xo_prev * w0 + xe * w1 + xo * w2
        dst[1] = bias + xe * w0 + xo * w1 + xe_next * w2

    rb = min(half, HY_ROW_BLOCK)
    short_conv(v_ref, 0, z_scr)
    for o, (xg_ref, kr_ref, ki_ref) in enumerate(((x1_ref, kr0_ref, ki0_ref), (x2_ref, kr1_ref, ki1_ref))):
        ze, zo = z_scr[0], z_scr[1]
        zeb, zob = ze.astype(BF16), zo.astype(BF16)
        zm_re = jnp.sum(ze * alt, axis=0, keepdims=True)
        zm_im = -jnp.sum(zo * alt, axis=0, keepdims=True)
        km_re, km_im = km_ref[2 * o:2 * o + 1, :], km_ref[2 * o + 1:2 * o + 2, :]
        pm_re = zm_re * km_re - zm_im * km_im
        pm_im = zm_re * km_im + zm_im * km_re
        for r in range(0, half, rb):
            a_e = jnp.dot(ce_ref[r:r + rb, :], zeb, preferred_element_type=F32)
            a_o = jnp.dot(co_ref[r:r + rb, :], zob, preferred_element_type=F32)
            b_e = jnp.dot(se_ref[r:r + rb, :], zeb, preferred_element_type=F32)
            b_o = jnp.dot(so_ref[r:r + rb, :], zob, preferred_element_type=F32)
            lo_re, lo_im, hi_re, hi_im = a_e + a_o, b_e + b_o, a_e - a_o, b_o - b_e
            kl_re, kl_im = kr_ref[r:r + rb, :], ki_ref[r:r + rb, :]
            kh_re, kh_im = kr_ref[half + r:half + r + rb, :], ki_ref[half + r:half + r + rb, :]
            pl_re, pl_im = lo_re * kl_re - lo_im * kl_im, lo_re * kl_im + lo_im * kl_re
            ph_re, ph_im = hi_re * kh_re - hi_im * kh_im, hi_re * kh_im + hi_im * kh_re
            p_scr[0, r:r + rb, :] = (pl_re + ph_re).astype(BF16)
            p_scr[1, r:r + rb, :] = (pl_im - ph_im).astype(BF16)
            p_scr[2, r:r + rb, :] = (pl_re - ph_re).astype(BF16)
            p_scr[3, r:r + rb, :] = (pl_im + ph_im).astype(BF16)
        short_conv(xg_ref, o + 1, xg_scr)
        sk = sk_ref[o:o + 1, :]
        for r in range(0, half, rb):
            y_e = (jnp.dot(ce_ref[r:r + rb, :], p_scr[0], preferred_element_type=F32)
                   + jnp.dot(se_ref[r:r + rb, :], p_scr[1], preferred_element_type=F32)
                   + alt[r:r + rb] * pm_re)
            y_o = (jnp.dot(cot_ref[r:r + rb, :], p_scr[2], preferred_element_type=F32)
                   + jnp.dot(sot_ref[r:r + rb, :], p_scr[3], preferred_element_type=F32)
                   - alt[r:r + rb] * pm_im)
            z_scr[0, r:r + rb, :] = xg_scr[0, r:r + rb, :] * (y_e + z_scr[0, r:r + rb, :] * sk)
            z_scr[1, r:r + rb, :] = xg_scr[1, r:r + rb, :] * (y_o + z_scr[1, r:r + rb, :] * sk)
    ge, go = halves(gate_ref)
    tiles = o_ref.shape[0]
    o_ref[:, 0] = (z_scr[0] * _silu(ge)).astype(o_ref.dtype).reshape(tiles, HY_PARITY_ROWS // 2, cb)
    o_ref[:, 1] = (z_scr[1] * _silu(go)).astype(o_ref.dtype).reshape(tiles, HY_PARITY_ROWS // 2, cb)


def hyena_core(proj5, conv_w, conv_b, kr, ki, kmid, skip, mats, cb):
    nseq, tiles, _, prow, _ = proj5.shape
    seq_len = tiles * 2 * prow
    half = seq_len // 2
    nj = HY_WIDTH // cb
    once = pl.Buffered(1)

    def act(k):
        return pl.BlockSpec((None, tiles, 2, prow, cb), lambda j, b: (b, 0, 0, 0, k * nj + j))

    def spec(k):
        return pl.BlockSpec((seq_len, cb), lambda j, b: (0, k * nj + j), pipeline_mode=once)

    dft = pl.BlockSpec((half, half), lambda j, b: (0, 0), pipeline_mode=once)
    body = functools.partial(_hyena_body, seq_len=seq_len)
    return pl.pallas_call(
        body,
        grid=(nj, nseq),
        in_specs=[act(0), act(1), act(2), act(3),
                  pl.BlockSpec((HY_SHORT, 3, cb), lambda j, b: (0, 0, j)),
                  pl.BlockSpec((3, cb), lambda j, b: (0, j)),
                  spec(0), spec(0), spec(1), spec(1),
                  pl.BlockSpec((2 * HY_ORDER, cb), lambda j, b: (0, j)),
                  pl.BlockSpec((HY_ORDER, cb), lambda j, b: (0, j))] + [dft] * 6,
        out_specs=pl.BlockSpec((None, tiles, 2, prow, cb), lambda j, b: (b, 0, 0, 0, j)),
        out_shape=jax.ShapeDtypeStruct((nseq, tiles, 2, prow, HY_WIDTH), BF16),
        scratch_shapes=[pltpu.VMEM((2, half, cb), F32), pltpu.VMEM((2, half, cb), F32),
                        pltpu.VMEM((4, half, cb), BF16)],
        compiler_params=_params("parallel", "arbitrary", vmem=VMEM_LIMIT_BYTES),
        name="hyena",
    )(proj5, proj5, proj5, proj5,
      conv_w.reshape(HY_SHORT, 3, HY_WIDTH), conv_b.reshape(3, HY_WIDTH),
      kr, ki, kr, ki, kmid, skip, *mats)


def parity_permutation():
    i = jnp.arange(HY_PARITY_ROWS)
    h = HY_PARITY_ROWS // 2
    src = jnp.where(i < h, 2 * i, 2 * (i - h) + 1)
    p = (src[:, None] == jnp.arange(HY_PARITY_ROWS)[None, :]).astype(BF16)
    return p, jnp.transpose(p)


def _inproj_parity_body(x_ref, g_ref, mod_ref, p_ref, w_ref, o_ref, h_scr):
    @pl.when(pl.program_id(1) == 0)
    def _():
        x = x_ref[...]
        y = x * lax.rsqrt(jnp.mean(x * x, axis=-1, keepdims=True) + EPS) * g_ref[...]
        h = (y * (1.0 + mod_ref[1:2, :]) + mod_ref[0:1, :]).astype(BF16)
        for r in range(0, x.shape[0], HY_PARITY_ROWS):
            h_scr[r:r + HY_PARITY_ROWS, :] = jnp.dot(p_ref[...], h[r:r + HY_PARITY_ROWS],
                                                     preferred_element_type=F32).astype(BF16)

    out = jnp.dot(h_scr[...], w_ref[...], preferred_element_type=F32).astype(o_ref.dtype)
    o_ref[...] = out.reshape(o_ref.shape)


def in_proj_parity(x, g, mod, p, w, rows_per_mod, tm, tn):
    n, nout = x.shape[0], w.shape[1]
    pr = HY_PARITY_ROWS
    return pl.pallas_call(
        _inproj_parity_body,
        grid=(n // tm, nout // tn),
        in_specs=[pl.BlockSpec((tm, D_MODEL), lambda i, j: (i, 0)),
                  pl.BlockSpec((1, D_MODEL), lambda i, j: (0, 0)),
                  pl.BlockSpec((None, 3, D_MODEL), lambda i, j: ((i * tm) // rows_per_mod, 0, 0)),
                  pl.BlockSpec((pr, pr), lambda i, j: (0, 0)),
                  pl.BlockSpec((D_MODEL, tn), lambda i, j: (0, j))],
        out_specs=pl.BlockSpec((tm // pr, 2, pr // 2, tn), lambda i, j: (i, 0, 0, j)),
        out_shape=jax.ShapeDtypeStruct((n // pr, 2, pr // 2, nout), BF16),
        scratch_shapes=[pltpu.VMEM((tm, D_MODEL), BF16)],
        compiler_params=_params("parallel", "arbitrary", vmem=VMEM_LIMIT_BYTES),
        name="in_proj_parity",
    )(x, g.reshape(1, D_MODEL), mod, p, w)


def _outproj_parity_body(y_ref, pt_ref, w_ref, x_ref, mod_ref, o_ref):
    tm = x_ref.shape[0]
    y = y_ref[...].reshape(tm, y_ref.shape[-1])
    parts = [jnp.dot(pt_ref[...], y[r:r + HY_PARITY_ROWS], preferred_element_type=F32).astype(BF16)
             for r in range(0, tm, HY_PARITY_ROWS)]
    out = jnp.dot(jnp.concatenate(parts, axis=0), w_ref[...], preferred_element_type=F32)
    o_ref[...] = x_ref[...] + mod_ref[2:3, :] * out


def out_proj_parity(y, pt, w, x, mod, rows_per_mod, tm):
    n = x.shape[0]
    width = y.shape[-1]
    pr = HY_PARITY_ROWS
    return pl.pallas_call(
        _outproj_parity_body,
        grid=(n // tm,),
        in_specs=[pl.BlockSpec((tm // pr, 2, pr // 2, width), lambda i: (i, 0, 0, 0)),
                  pl.BlockSpec((pr, pr), lambda i: (0, 0)),
                  pl.BlockSpec((width, D_MODEL), lambda i: (0, 0)),
                  pl.BlockSpec((tm, D_MODEL), lambda i: (i, 0)),
                  pl.BlockSpec((None, 3, D_MODEL), lambda i: ((i * tm) // rows_per_mod, 0, 0))],
        out_specs=pl.BlockSpec((tm, D_MODEL), lambda i: (i, 0)),
        out_shape=jax.ShapeDtypeStruct((n, D_MODEL), F32),
        compiler_params=_params("parallel", vmem=VMEM_LIMIT_BYTES),
        name="out_proj_parity",
    )(y, pt, w, x, mod)


def hyena_layer(x, g, mod, rows_per_mod, nseq, seq_len, w_in, conv_w, conv_b, filt_w, skip, w_out, perms, tm, cb):
    par_p, par_pt = perms
    pr = HY_PARITY_ROWS
    proj = in_proj_parity(x, g, mod, par_p, w_in, rows_per_mod, tm, D_MODEL)
    proj5 = proj.reshape(nseq, seq_len // pr, 2, pr // 2, 4 * HY_WIDTH)
    (cfil, sfil), mats = dft_matrices(seq_len)
    kr, ki, kmid = hyena_filter_spectrum(seq_len, cfil, sfil, *filt_w)
    y = hyena_core(proj5, conv_w, conv_b, kr, ki, kmid, skip, mats, cb)
    y = y.reshape(nseq * seq_len // pr, 2, pr // 2, HY_WIDTH)
    return out_proj_parity(y, par_pt, w_out, x, mod, rows_per_mod, min(tm, OUT_ROWS))


def kernel(x_prompt, x_sample, c, state_s5, state_ret, c_ctx, norm_g, mod_w, mod_b, s5_w_in, s5_lam_re, s5_lam_im, s5_log_step, s5_b_re, s5_b_im, s5_c_re, s5_c_im, s5_d, s5_w_glu, s5_b_glu, s5_w_out, ret_w_in, ret_decay_logit, ret_w_out, hy_w_in, hy_conv_w, hy_conv_b, hy_f_w1, hy_f_b1, hy_f_w2, hy_f_b2, hy_f_w3, hy_skip, hy_w_out, final_g):
    n_ctx, l_ctx, _ = x_prompt.shape
    n_dec, l_dec, _ = x_sample.shape
    xc = x_prompt.reshape(n_ctx * l_ctx, D_MODEL)
    xl = x_sample.reshape(n_dec * l_dec, D_MODEL)

    pad = (-(1 + n_dec)) % 8
    cvecs = jnp.concatenate([c_ctx[None, :], c, jnp.zeros((pad, D_MODEL), F32)], axis=0)
    mods = ada_mod_all(cvecs, mod_w, mod_b).reshape(DEPTH, -1, 3, D_MODEL)

    tm = 1024
    perm = (s5_lane_permutation(),) + s5_row_permutation()
    last_is_s5 = (DEPTH - 1) % N_MIXERS == 0
    s5_new, ret_new = [], []
    for i in range(DEPTH):
        kind, j = i % N_MIXERS, i // N_MIXERS
        mod_c = mods[i, 0:1]
        mod_l = mods[i, 1:1 + n_dec]
        g = norm_g[i]
        if kind == 0:
            tables = s5_tables(s5_lam_re[j], s5_lam_im[j], s5_log_step[j], s5_b_re[j], s5_b_im[j],
                               s5_c_re[j], s5_c_im[j], s5_d[j])
            w = (s5_w_in[j].astype(BF16), tables, perm, s5_w_glu[j].astype(BF16), s5_b_glu[j],
                 s5_w_out[j].astype(BF16))
            fg = final_g if last_is_s5 and i == DEPTH - 1 else None
            xc, st = s5_layer(xc, g, mod_c, n_ctx * l_ctx, n_ctx, l_ctx, None, *w, nb=n_ctx, final_g=fg)
            xl, _ = s5_layer(xl, g, mod_l, l_dec, n_dec, l_dec, state_s5[:, j], *w, nb=S5_SEQS_PER_STEP,
                             final_g=fg)
            s5_new.append(st)
        elif kind == 1:
            w_in = ret_w_in[j]
            kscale = jnp.concatenate([jnp.ones((RET_QK,), F32), jnp.full((RET_QK,), RET_DK ** -0.5, F32),
                                      jnp.ones((2 * RET_V,), F32)])
            w_in = (w_in * kscale[None, :]).astype(BF16)
            lg = jax.nn.log_sigmoid(ret_decay_logit[j])
            lg = jnp.broadcast_to(jnp.transpose(lg)[:, :, None], (RET_HEADS, 2, 128))
            w = (w_in, lg, ret_w_out[j].astype(BF16), tm)
            xc, st = retention_layer(xc, g, mod_c, n_ctx * l_ctx, n_ctx, l_ctx, None, False, *w)
            xl, _ = retention_layer(xl, g, mod_l, l_dec, n_dec, l_dec, state_ret[:, j], True, *w)
            ret_new.append(st)
        else:
            filt_w = (hy_f_w1[j], hy_f_b1[j], hy_f_w2[j], hy_f_b2[j], hy_f_w3[j])
            w = (hy_w_in[j].astype(BF16), hy_conv_w[j], hy_conv_b[j], filt_w, hy_skip[j], hy_w_out[j].astype(BF16),
                 parity_permutation(), tm)
            xc = hyena_layer(xc, g, mod_c, n_ctx * l_ctx, n_ctx, l_ctx, *w, cb=1024)
            xl = hyena_layer(xl, g, mod_l, l_dec, n_dec, l_dec, *w, cb=256)

    if not last_is_s5:
        xc, xl = final_norm(xc, final_g, tm), final_norm(xl, final_g, tm)
    y_prompt = xc.reshape(n_ctx, l_ctx, D_MODEL)
    y_sample = xl.reshape(n_dec, l_dec, D_MODEL)
    new_state_s5 = jnp.stack(s5_new, axis=1)
    new_state_ret = jnp.stack(ret_new, axis=1)
    return (y_prompt, y_sample, new_state_s5, new_state_ret)
```

```python
import functools
import math

import jax
import jax.numpy as jnp
from jax import lax
from jax.experimental import pallas as pl
from jax.experimental.pallas import tpu as pltpu

F32 = jnp.float32
BF16 = jnp.bfloat16

D_MODEL = 1024
DEPTH = 4
N_MIXERS = 3
EPS = 1e-6
GRID_W = 64

S5_GROUP = 16
S5_GROUPS = D_MODEL // S5_GROUP
S5_STATE = 64
S5_CHUNK = 16
S5_TILE = S5_CHUNK * S5_GROUP
S5_SLAB = 128 // S5_GROUP
S5_SEQS_PER_STEP = 8
S5_ROWS = 512

RET_HEADS = 8
RET_QK = D_MODEL
RET_V = 2 * D_MODEL
RET_DK = RET_QK // RET_HEADS
RET_DV = RET_V // RET_HEADS
RET_BLOCK = 256
ROPE_BASE = 10000.0

HY_WIDTH = 2 * D_MODEL
HY_ORDER = 2
HY_SHORT = 3
HY_BANDS = 16
HY_SHORT_DECAY_PCT = 0.3
HY_LONG_DECAY_PCT = 1.5
HY_DECAY_TARGET = 1e-2
HY_LANES = 128
HY_PARITY_ROWS = 256
HY_ROW_BLOCK = 512

VMEM_LIMIT_BYTES = 56 * 1024 * 1024
OUT_ROWS = 512


def _params(*sem, vmem=None):
    return pltpu.CompilerParams(dimension_semantics=sem, vmem_limit_bytes=vmem)


def _silu(x):
    return x * jax.nn.sigmoid(x)


def _mod_body(c_ref, w_ref, b_ref, o_ref):
    a = _silu(c_ref[...]).astype(BF16)
    o_ref[0] = jnp.dot(a, w_ref[0].astype(BF16), preferred_element_type=F32) + b_ref[0]


def ada_mod_all(cvecs, mod_w, mod_b):
    r = cvecs.shape[0]
    tn = D_MODEL
    return pl.pallas_call(
        _mod_body,
        grid=(DEPTH, 3 * D_MODEL // tn),
        in_specs=[pl.BlockSpec((r, D_MODEL), lambda i, j: (0, 0)),
                  pl.BlockSpec((1, D_MODEL, tn), lambda i, j: (i, 0, j)),
                  pl.BlockSpec((1, 1, tn), lambda i, j: (i, 0, j))],
        out_specs=pl.BlockSpec((1, r, tn), lambda i, j: (i, 0, j)),
        out_shape=jax.ShapeDtypeStruct((DEPTH, r, 3 * D_MODEL), F32),
        compiler_params=_params("parallel", "parallel"),
        name="ada_mod",
    )(cvecs, mod_w, mod_b.reshape(DEPTH, 1, 3 * D_MODEL))


def _inproj_body(x_ref, g_ref, mod_ref, w_ref, o_ref, h_scr):
    @pl.when(pl.program_id(1) == 0)
    def _():
        x = x_ref[...]
        y = x * lax.rsqrt(jnp.mean(x * x, axis=-1, keepdims=True) + EPS) * g_ref[...]
        h_scr[...] = (y * (1.0 + mod_ref[1:2, :]) + mod_ref[0:1, :]).astype(BF16)

    o_ref[...] = jnp.dot(h_scr[...], w_ref[...], preferred_element_type=F32).astype(o_ref.dtype)


def in_proj(x, g, mod, w, rows_per_mod, tm, tn):
    n, nout = x.shape[0], w.shape[1]
    return pl.pallas_call(
        _inproj_body,
        grid=(n // tm, nout // tn),
        in_specs=[pl.BlockSpec((tm, D_MODEL), lambda i, j: (i, 0)),
                  pl.BlockSpec((1, D_MODEL), lambda i, j: (0, 0)),
                  pl.BlockSpec((None, 3, D_MODEL), lambda i, j: ((i * tm) // rows_per_mod, 0, 0)),
                  pl.BlockSpec((D_MODEL, tn), lambda i, j: (0, j))],
        out_specs=pl.BlockSpec((tm, tn), lambda i, j: (i, j)),
        out_shape=jax.ShapeDtypeStruct((n, nout), BF16),
        scratch_shapes=[pltpu.VMEM((tm, D_MODEL), BF16)],
        compiler_params=_params("parallel", "arbitrary", vmem=VMEM_LIMIT_BYTES),
        name="in_proj",
    )(x, g.reshape(1, D_MODEL), mod, w)


def _outproj_body(y_ref, w_ref, x_ref, mod_ref, o_ref):
    out = jnp.dot(y_ref[...], w_ref[...], preferred_element_type=F32)
    o_ref[...] = x_ref[...] + mod_ref[2:3, :] * out


def out_proj(y, w, x, mod, rows_per_mod, tm):
    n, width = y.shape
    return pl.pallas_call(
        _outproj_body,
        grid=(n // tm,),
        in_specs=[pl.BlockSpec((tm, width), lambda i: (i, 0)),
                  pl.BlockSpec((width, D_MODEL), lambda i: (0, 0)),
                  pl.BlockSpec((tm, D_MODEL), lambda i: (i, 0)),
                  pl.BlockSpec((None, 3, D_MODEL), lambda i: ((i * tm) // rows_per_mod, 0, 0))],
        out_specs=pl.BlockSpec((tm, D_MODEL), lambda i: (i, 0)),
        out_shape=jax.ShapeDtypeStruct((n, D_MODEL), F32),
        compiler_params=_params("parallel", vmem=VMEM_LIMIT_BYTES),
        name="out_proj",
    )(y, w, x, mod)


def _final_norm_body(x_ref, g_ref, o_ref):
    x = x_ref[...]
    o_ref[...] = x * lax.rsqrt(jnp.mean(x * x, axis=-1, keepdims=True) + EPS) * g_ref[...]


def final_norm(x, g, tm):
    n = x.shape[0]
    return pl.pallas_call(
        _final_norm_body,
        grid=(n // tm,),
        in_specs=[pl.BlockSpec((tm, D_MODEL), lambda i: (i, 0)),
                  pl.BlockSpec((1, D_MODEL), lambda i: (0, 0))],
        out_specs=pl.BlockSpec((tm, D_MODEL), lambda i: (i, 0)),
        out_shape=jax.ShapeDtypeStruct((n, D_MODEL), F32),
        compiler_params=_params("parallel"),
        name="final_norm",
    )(x, g.reshape(1, D_MODEL))


def s5_tables(lam_re, lam_im, log_step, b_re, b_im, c_re, c_im, d_skip):
    t_len = S5_CHUNK
    dt = jnp.exp(log_step)[..., None]
    ab_re = jnp.exp(lam_re * dt) * jnp.cos(lam_im * dt)
    ab_im = jnp.exp(lam_re * dt) * jnp.sin(lam_im * dt)
    den = lam_re * lam_re + lam_im * lam_im
    nr, ni = ab_re - 1.0, ab_im
    f_re = (nr * lam_re + ni * lam_im) / den
    f_im = (ni * lam_re - nr * lam_im) / den
    bb_re = f_re[..., None] * b_re - f_im[..., None] * b_im
    bb_im = f_re[..., None] * b_im + f_im[..., None] * b_re
    ks = jnp.arange(t_len + 1, dtype=F32)[:, None, None, None]
    pw_mag = jnp.exp(ks * (lam_re * dt)[None])
    pw_re = pw_mag * jnp.cos(ks * (lam_im * dt)[None])
    pw_im = pw_mag * jnp.sin(ks * (lam_im * dt)[None])

    npair = S5_GROUPS // 2
    pair_eye = jnp.eye(2, dtype=F32)

    def pair_pack(a):
        a = a.reshape(2, npair, 2, S5_GROUP, S5_STATE)
        a = a[:, :, :, :, None, :] * pair_eye[None, None, :, None, :, None]
        return jnp.transpose(a.reshape(2, npair, 2, S5_GROUP, 2 * S5_STATE), (1, 0, 2, 3, 4))

    bbt = jnp.stack([pair_pack(jnp.swapaxes(bb_re, 2, 3)), pair_pack(jnp.swapaxes(bb_im, 2, 3))], axis=2)
    cpk = jnp.stack([pair_pack(c_re), pair_pack(c_im)], axis=2)
    pw = jnp.stack([pw_re, pw_im], axis=0).reshape(2, t_len + 1, 2, npair, 2 * S5_STATE)
    pw = jnp.transpose(pw, (3, 2, 0, 1, 4))
    dmat = jnp.eye(S5_GROUP, dtype=F32) * d_skip.reshape(S5_GROUPS, 1, S5_GROUP)
    dmat = jnp.pad(dmat, ((0, 0), (0, 0), (0, S5_TILE - S5_GROUP))).reshape(npair, 2, S5_GROUP, S5_TILE)
    m, f, et = s5_table_kernel(bbt, cpk, pw, dmat)

    dec = jnp.stack([pw_re[t_len, 0], pw_im[t_len, 0], pw_re[t_len, 1], pw_im[t_len, 1]], axis=0)
    dec = jnp.transpose(dec.reshape(4, S5_GROUPS // S5_SLAB, S5_SLAB * S5_STATE), (1, 0, 2))
    return m.reshape(S5_GROUPS, S5_TILE, S5_TILE), f, et, dec


def _s5_table_body(bb_ref, c_ref, pw_ref, d_ref, m_ref, f_ref, et_ref):
    t_len = S5_CHUNK
    nt_dims = (((1,), (1,)), ((), ()))
    hp = lax.Precision.HIGHEST

    def times_power(ref, d, g, k):
        pr, pi = pw_ref[d, 0, k:k + 1, :], pw_ref[d, 1, k:k + 1, :]
        ar, ai = ref[d, 0, g], ref[d, 1, g]
        return ar * pr - ai * pi, ar * pi + ai * pr

    def stack(parts):
        return jnp.concatenate(parts, axis=0)

    for d in range(2):
        f_pow = [t_len - 1 - s for s in range(t_len)] if d == 0 else list(range(t_len))
        e_pow = [s + 1 for s in range(t_len)] if d == 0 else [t_len - s for s in range(t_len)]
        fr, fi, er, ei = [], [], [], []
        for g in range(2):
            for s in range(t_len):
                a, b = times_power(bb_ref, d, g, f_pow[s])
                fr.append(a)
                fi.append(b)
                a, b = times_power(c_ref, d, g, e_pow[s])
                er.append(a)
                ei.append(-b)
        lanes = 2 * S5_STATE
        f_ref[:, (2 * d) * lanes:(2 * d + 1) * lanes] = stack(fr).astype(f_ref.dtype)
        f_ref[:, (2 * d + 1) * lanes:(2 * d + 2) * lanes] = stack(fi).astype(f_ref.dtype)
        et_ref[:, (2 * d) * lanes:(2 * d + 1) * lanes] = stack(er).astype(et_ref.dtype)
        et_ref[:, (2 * d + 1) * lanes:(2 * d + 2) * lanes] = stack(ei).astype(et_ref.dtype)

    zeros = jnp.zeros((S5_GROUP, S5_TILE), F32)
    for g in range(2):
        def lag_kernels(d, powers):
            car, cai = zip(*(times_power(c_ref, d, g, k) for k in powers))
            return (lax.dot_general(bb_ref[d, 0, g], stack(car), nt_dims, precision=hp, preferred_element_type=F32)
                    - lax.dot_general(bb_ref[d, 1, g], stack(cai), nt_dims, precision=hp,
                                      preferred_element_type=F32))
        v_f = jnp.concatenate([zeros, lag_kernels(0, range(t_len)) + d_ref[g]], axis=1)
        v_b = jnp.concatenate([lag_kernels(1, range(t_len - 1, -1, -1)), zeros], axis=1)
        width = 2 * S5_TILE
        for s in range(t_len):
            blk_f = pltpu.roll(v_f, (width - (t_len - s) * S5_GROUP) % width, 1)[:, :S5_TILE]
            blk_b = pltpu.roll(v_b, (width - (t_len - 1 - s) * S5_GROUP) % width, 1)[:, :S5_TILE]
            m_ref[g, s * S5_GROUP:(s + 1) * S5_GROUP, :] = (blk_f + blk_b).astype(m_ref.dtype)


def s5_table_kernel(bbt, cpk, pw, dmat):
    npair = S5_GROUPS // 2
    pspec = pl.BlockSpec((None, 2, 2, 2, S5_GROUP, 2 * S5_STATE), lambda k: (k, 0, 0, 0, 0, 0))
    wide = pl.BlockSpec((None, 2 * S5_TILE, 2 * S5_TILE), lambda k: (k, 0, 0))
    return pl.pallas_call(
        _s5_table_body,
        grid=(npair,),
        in_specs=[pspec, pspec,
                  pl.BlockSpec((None, 2, 2, S5_CHUNK + 1, 2 * S5_STATE), lambda k: (k, 0, 0, 0, 0)),
                  pl.BlockSpec((None, 2, S5_GROUP, S5_TILE), lambda k: (k, 0, 0, 0))],
        out_specs=[pl.BlockSpec((None, 2, S5_TILE, S5_TILE), lambda k: (k, 0, 0, 0)), wide, wide],
        out_shape=[jax.ShapeDtypeStruct((npair, 2, S5_TILE, S5_TILE), BF16),
                   jax.ShapeDtypeStruct((npair, 2 * S5_TILE, 2 * S5_TILE), BF16),
                   jax.ShapeDtypeStruct((npair, 2 * S5_TILE, 2 * S5_TILE), BF16)],
        compiler_params=_params("parallel"),
        name="s5_tables",
    )(bbt, cpk, pw, dmat)


def s5_lane_permutation():
    width = S5_CHUNK * S5_SLAB * S5_GROUP
    i = jnp.arange(width)
    step, grp, ch = i // (S5_SLAB * S5_GROUP), (i // S5_GROUP) % S5_SLAB, i % S5_GROUP
    dst = grp * S5_TILE + step * S5_GROUP + ch
    return (dst[:, None] == jnp.arange(width)[None, :]).astype(BF16)


def _s5_scan_body(u_ref, q_ref, m_ref, f_ref, et_ref, dec_ref, h0_ref, y_ref, hf_ref,
                  uy_s, st_s, *, nchunk, nseq):
    npair = S5_SLAB // 2
    pair_w = 2 * S5_TILE
    lane_w = 2 * S5_STATE
    nt_dims = (((1,), (1,)), ((), ()))
    xcat = jnp.concatenate([u_ref[s] for s in range(S5_CHUNK)], axis=1)
    for k in range(npair):
        cols = slice(k * pair_w, (k + 1) * pair_w)
        up = jnp.dot(xcat, q_ref[:, cols], preferred_element_type=F32).astype(BF16)
        uy_s[:, cols] = up
        x = jnp.dot(up, f_ref[k], preferred_element_type=F32)
        for q in range(4):
            st_s[q * npair + k] = x[:, q * lane_w:(q + 1) * lane_w]

    def lanes_of(ref, q, k):
        return ref[q, :, k * lane_w:(k + 1) * lane_w]

    dec = [[dec_ref[q:q + 1, k * lane_w:(k + 1) * lane_w] for k in range(npair)] for q in range(4)]

    def step(i, carry):
        rows_f = pl.ds(pl.multiple_of(i * nseq, nseq), nseq)
        rows_b = pl.ds(pl.multiple_of((nchunk - 1 - i) * nseq, nseq), nseq)
        new = []
        for k in range(npair):
            fr, fi, br, bi = (carry[q * npair + k] for q in range(4))
            xr, xi = st_s[0 * npair + k, rows_f, :], st_s[1 * npair + k, rows_f, :]
            yr, yi = st_s[2 * npair + k, rows_b, :], st_s[3 * npair + k, rows_b, :]
            st_s[0 * npair + k, rows_f, :] = fr
            st_s[1 * npair + k, rows_f, :] = fi
            st_s[2 * npair + k, rows_b, :] = br
            st_s[3 * npair + k, rows_b, :] = bi
            new.append((dec[0][k] * fr - dec[1][k] * fi + xr, dec[0][k] * fi + dec[1][k] * fr + xi,
                        dec[2][k] * br - dec[3][k] * bi + yr, dec[2][k] * bi + dec[3][k] * br + yi))
        return tuple(new[k][q] for q in range(4) for k in range(npair))

    init = tuple(lanes_of(h0_ref, q, k) for q in range(4) for k in range(npair))
    final = lax.fori_loop(0, nchunk, step, init)
    for q in range(4):
        for k in range(npair):
            hf_ref[q, :, k * lane_w:(k + 1) * lane_w] = final[q * npair + k]

    for k in range(npair):
        h = jnp.concatenate([st_s[q * npair + k] for q in range(4)], axis=1)
        ye = lax.dot_general(h.astype(BF16), et_ref[k], nt_dims, preferred_element_type=F32)
        for g in range(2):
            cols = slice(k * pair_w + g * S5_TILE, k * pair_w + (g + 1) * S5_TILE)
            y = jnp.dot(uy_s[:, cols], m_ref[2 * k + g], preferred_element_type=F32)
            uy_s[:, cols] = (y + ye[:, g * S5_TILE:(g + 1) * S5_TILE]).astype(BF16)
    slab_w = S5_SLAB * S5_GROUP
    steps = pair_w // slab_w
    for t0 in range(0, S5_CHUNK, steps):
        yp = lax.dot_general(uy_s[...], q_ref[t0 * slab_w:(t0 + steps) * slab_w, :], nt_dims,
                             preferred_element_type=F32).astype(y_ref.dtype)
        for t in range(steps):
            y_ref[t0 + t] = yp[:, t * slab_w:(t + 1) * slab_w]


def s5_scan(u_s, perm, m, f, e, dec, h0, nchunk, nseq, nb):
    rows = nchunk * nb
    nslab = S5_GROUPS // S5_SLAB
    slab_w = S5_SLAB * S5_GROUP
    state_w = S5_SLAB * S5_STATE
    width = S5_CHUNK * slab_w
    body = functools.partial(_s5_scan_body, nchunk=nchunk, nseq=nb)
    hspec = pl.BlockSpec((None, None, 4, nb, state_w), lambda s, b: (s, b, 0, 0, 0))
    state = pltpu.VMEM((2 * S5_SLAB, rows, 2 * S5_STATE), F32)
    once = pl.Buffered(1)
    return pl.pallas_call(
        body,
        grid=(nslab, nseq // nb),
        in_specs=[pl.BlockSpec((S5_CHUNK, rows, slab_w), lambda s, b: (0, b, s), pipeline_mode=once),
                  pl.BlockSpec((width, width), lambda s, b: (0, 0), pipeline_mode=once),
                  pl.BlockSpec((S5_SLAB, S5_TILE, S5_TILE), lambda s, b: (s, 0, 0), pipeline_mode=once),
                  pl.BlockSpec((S5_SLAB // 2, 2 * S5_TILE, 2 * S5_TILE), lambda s, b: (s, 0, 0), pipeline_mode=once),
                  pl.BlockSpec((S5_SLAB // 2, 2 * S5_TILE, 2 * S5_TILE), lambda s, b: (s, 0, 0), pipeline_mode=once),
                  pl.BlockSpec((None, 4, state_w), lambda s, b: (s, 0, 0)),
                  hspec],
        out_specs=[pl.BlockSpec((S5_CHUNK, rows, slab_w), lambda s, b: (0, b, s)), hspec],
        out_shape=[jax.ShapeDtypeStruct((S5_CHUNK, nseq * nchunk, D_MODEL), BF16),
                   jax.ShapeDtypeStruct((nslab, nseq // nb, 4, nb, state_w), F32)],
        scratch_shapes=[pltpu.VMEM((rows, width), BF16), state],
        compiler_params=_params("parallel", "parallel", vmem=VMEM_LIMIT_BYTES),
        name="s5_scan",
    )(u_s, perm, m, f, e, dec, h0)


def s5_row_permutation(nb):
    cpt = S5_ROWS // (S5_CHUNK * nb)
    i = jnp.arange(S5_ROWS)
    step, chunk, seq = i // (cpt * nb), (i // nb) % cpt, i % nb
    src = seq * (cpt * S5_CHUNK) + chunk * S5_CHUNK + step
    p = (src[:, None] == jnp.arange(S5_ROWS)[None, :]).astype(BF16)
    return p, jnp.transpose(p)


def _s5_tile_specs(x3, mod, nb):
    nseq, seq_len, _ = x3.shape
    cpt = S5_ROWS // (S5_CHUNK * nb)
    ctiles = seq_len // (cpt * S5_CHUNK)
    grid = (nseq // nb, ctiles)
    xspec = pl.BlockSpec((nb, cpt * S5_CHUNK, D_MODEL), lambda b, c: (b, c, 0))
    if mod.shape[0] == 1:
        mspec = pl.BlockSpec((1, 3, D_MODEL), lambda b, c: (0, 0, 0))
    else:
        mspec = pl.BlockSpec((nb, 3, D_MODEL), lambda b, c: (b, 0, 0))

    def stepped(width, col):
        return pl.BlockSpec((S5_CHUNK, S5_ROWS // S5_CHUNK, width), lambda b, c: (0, b * ctiles + c, col))

    return grid, xspec, mspec, stepped


def _inproj_steps_body(x_ref, g_ref, mod_ref, p_ref, w_ref, o_ref):
    x = x_ref[...]
    y = x * lax.rsqrt(jnp.mean(x * x, axis=-1, keepdims=True) + EPS) * g_ref[...]
    h = (y * (1.0 + mod_ref[:, 1:2, :]) + mod_ref[:, 0:1, :]).astype(BF16).reshape(S5_ROWS, D_MODEL)
    h = jnp.dot(p_ref[...], h, preferred_element_type=F32).astype(BF16)
    out = jnp.dot(h, w_ref[...], preferred_element_type=F32).astype(o_ref.dtype)
    o_ref[...] = out.reshape(o_ref.shape)


def in_proj_steps(x3, g, mod, p, w, nb):
    nseq, seq_len, _ = x3.shape
    nout = w.shape[1]
    grid, xspec, mspec, stepped = _s5_tile_specs(x3, mod, nb)
    return pl.pallas_call(
        _inproj_steps_body,
        grid=grid,
        in_specs=[xspec,
                  pl.BlockSpec((1, D_MODEL), lambda b, c: (0, 0)),
                  mspec,
                  pl.BlockSpec((S5_ROWS, S5_ROWS), lambda b, c: (0, 0)),
                  pl.BlockSpec((D_MODEL, nout), lambda b, c: (0, 0))],
        out_specs=stepped(nout, 0),
        out_shape=jax.ShapeDtypeStruct((S5_CHUNK, nseq * seq_len // S5_CHUNK, nout), BF16),
        compiler_params=_params("parallel", "parallel", vmem=VMEM_LIMIT_BYTES),
        name="in_proj_steps",
    )(x3, g.reshape(1, D_MODEL), mod, p, w)


def _s5_out_body(y_ref, gate_ref, wg_ref, bg_ref, wo_ref, pt_ref, x_ref, mod_ref, *rest):
    o_ref = rest[-1]
    g = jax.nn.gelu(y_ref[...].astype(F32).reshape(S5_ROWS, D_MODEL))
    t = jnp.dot(g.astype(BF16), wg_ref[...], preferred_element_type=F32) + bg_ref[...]
    z = g * jax.nn.sigmoid(t) * _silu(gate_ref[...].astype(F32).reshape(S5_ROWS, D_MODEL))
    z = jnp.dot(pt_ref[...], z.astype(BF16), preferred_element_type=F32).astype(BF16)
    out = jnp.dot(z, wo_ref[...], preferred_element_type=F32)
    x = x_ref[...] + mod_ref[:, 2:3, :] * out.reshape(x_ref.shape)
    if len(rest) == 2:
        x = x * lax.rsqrt(jnp.mean(x * x, axis=-1, keepdims=True) + EPS) * rest[0][...]
    o_ref[...] = x


def s5_out(y_s, u_s, w_glu, b_glu, w_out, pt, x3, mod, nb, final_g=None):
    grid, xspec, mspec, stepped = _s5_tile_specs(x3, mod, nb)
    wspec = pl.BlockSpec((D_MODEL, D_MODEL), lambda b, c: (0, 0))
    vspec = pl.BlockSpec((1, D_MODEL), lambda b, c: (0, 0))
    in_specs = [stepped(D_MODEL, 0), stepped(D_MODEL, 1), wspec, vspec, wspec,
                pl.BlockSpec((S5_ROWS, S5_ROWS), lambda b, c: (0, 0)), xspec, mspec]
    args = [y_s, u_s, w_glu, b_glu.reshape(1, D_MODEL), w_out, pt, x3, mod]
    if final_g is not None:
        in_specs.append(vspec)
        args.append(final_g.reshape(1, D_MODEL))
    return pl.pallas_call(
        _s5_out_body,
        grid=grid,
        in_specs=in_specs,
        out_specs=xspec,
        out_shape=jax.ShapeDtypeStruct(x3.shape, F32),
        compiler_params=_params("parallel", "parallel", vmem=VMEM_LIMIT_BYTES),
        name="s5_out",
    )(*args)


def s5_layer(x, g, mod, nseq, seq_len, h0, w_in, tables, perm, w_glu, b_glu, w_out, nb, final_g=None):
    nchunk = seq_len // S5_CHUNK
    nslab = S5_GROUPS // S5_SLAB
    state_w = S5_SLAB * S5_STATE
    row_p, row_pt = s5_row_permutation(nb)
    x3 = x.reshape(nseq, seq_len, D_MODEL)
    u_s = in_proj_steps(x3, g, mod, row_p, w_in, nb)
    if h0 is None:
        h0_p = jnp.zeros((nslab, nseq // nb, 4, nb, state_w), F32)
    else:
        h0_p = jnp.transpose(h0.reshape(nseq // nb, nb, 4, nslab, state_w), (3, 0, 2, 1, 4))
    m, f, e, dec = tables
    y_s, hf = s5_scan(u_s, perm, m, f, e, dec, h0_p, nchunk, nseq, nb)
    x_new = s5_out(y_s, u_s, w_glu, b_glu, w_out, row_pt, x3, mod, nb, final_g).reshape(nseq * seq_len, D_MODEL)
    states = jnp.transpose(hf, (1, 3, 2, 0, 4)).reshape(nseq, 2, 2, S5_GROUPS, S5_STATE)
    return x_new, states


def _ret_body(*refs, seq_len, rope, has_init, want_state):
    refs = list(refs)
    q_ref, k_ref, v_ref, gate_ref, lg_ref = refs[:5]
    pos = 5
    if rope:
        cos_ref, sin_ref = refs[pos:pos + 2]
        pos += 2
    if has_init:
        s0_ref = refs[pos]
        pos += 1
    o_ref = refs[pos]
    pos += 1
    if want_state:
        sfin_ref = refs[pos]
        pos += 1
    sf_scr = refs[pos]

    c_len = RET_BLOCK
    nblk = seq_len // c_len
    ii = lax.broadcasted_iota(jnp.int32, (c_len, c_len), 0)
    jj = lax.broadcasted_iota(jnp.int32, (c_len, c_len), 1)
    diff = (ii - jj).astype(F32)
    ic = lax.broadcasted_iota(jnp.int32, (c_len, 1), 0).astype(F32)
    for hh in range(lg_ref.shape[0]):
        _ret_head(hh, q_ref, k_ref, v_ref, gate_ref, lg_ref,
                  (cos_ref, sin_ref) if rope else None, s0_ref if has_init else None,
                  o_ref, sfin_ref if want_state else None, sf_scr, diff, ic, seq_len)


def _ret_head(hh, q_ref, k_ref, v_ref, gate_ref, lg_ref, rope_refs, s0_ref, o_ref, sfin_ref, sf_scr,
              diff, ic, seq_len):
    rope, has_init, want_state = rope_refs is not None, s0_ref is not None, sfin_ref is not None
    c_len = RET_BLOCK
    nblk = seq_len // c_len
    qk_cols = slice(hh * RET_DK, (hh + 1) * RET_DK)
    v_cols = slice(hh * RET_DV, (hh + 1) * RET_DV)
    lgf = lg_ref[hh, 0:1, 0:1]
    lgb = lg_ref[hh, 1:2, 0:1]
    decay = jnp.exp(jnp.abs(diff) * jnp.where(diff >= 0, lgf, lgb))
    q_dec_f = jnp.exp((ic + 1.0) * lgf)
    q_dec_b = jnp.exp((c_len - ic) * lgb)
    k_dec_f = jnp.exp((c_len - 1.0 - ic) * lgf)
    k_dec_b = jnp.exp(ic * lgb)
    c_dec_f = jnp.exp(c_len * lgf)
    c_dec_b = jnp.exp(c_len * lgb)

    q = q_ref[:, qk_cols].astype(F32)
    k = k_ref[:, qk_cols].astype(F32)
    if rope:
        cos_ref, sin_ref = rope_refs
        lane = lax.broadcasted_iota(jnp.int32, (seq_len, RET_DK), 1)
        first = (lane % (RET_DK // 2)) < (RET_DK // 4)
        cos = cos_ref[...]
        sin = sin_ref[...]

        def rot(x):
            swapped = jnp.where(first, pltpu.roll(x, RET_DK - RET_DK // 4, 1), pltpu.roll(x, RET_DK // 4, 1))
            return x * cos + swapped * sin

        q = rot(q)
        k = rot(k)

    def blk(a, c):
        return a[c * c_len:(c + 1) * c_len]

    def kv_state(kd, c):
        kt = jnp.transpose(blk(k, c) * kd).astype(BF16)
        return jnp.dot(kt, v_ref[c * c_len:(c + 1) * c_len, v_cols], preferred_element_type=F32)

    use_state = has_init or nblk > 1
    s_f = s0_ref[0, hh] if has_init else jnp.zeros((RET_DK, RET_DV), F32)
    for c in range(nblk):
        if use_state:
            sf_scr[c] = s_f
        if want_state or c < nblk - 1:
            s_f = c_dec_f * s_f + kv_state(k_dec_f, c)
    if want_state:
        sfin_ref[0, hh] = s_f

    s_b = s0_ref[1, hh] if has_init else jnp.zeros((RET_DK, RET_DV), F32)
    for c in range(nblk - 1, -1, -1):
        qc = blk(q, c)
        kc = blk(k, c)
        vc = v_ref[c * c_len:(c + 1) * c_len, v_cols]
        s = lax.dot_general(qc.astype(BF16), kc.astype(BF16), (((1,), (1,)), ((), ())),
                            preferred_element_type=F32)
        o = jnp.dot((s * decay).astype(BF16), vc, preferred_element_type=F32)
        if use_state:
            o = o + jnp.dot((qc * q_dec_f).astype(BF16), sf_scr[c].astype(BF16), preferred_element_type=F32)
            o = o + jnp.dot((qc * q_dec_b).astype(BF16), s_b.astype(BF16), preferred_element_type=F32)
        o = o * lax.rsqrt(jnp.mean(o * o, axis=-1, keepdims=True) + EPS)
        gate = gate_ref[c * c_len:(c + 1) * c_len, v_cols].astype(F32)
        o_ref[c * c_len:(c + 1) * c_len, v_cols] = (o * _silu(gate)).astype(o_ref.dtype)
        if want_state or c > 0:
            s_b = c_dec_b * s_b + kv_state(k_dec_b, c)
    if want_state:
        sfin_ref[1, hh] = s_b


def retention_core(proj3, lg, rope_tabs, s0, want_state, hp):
    nseq, seq_len, _ = proj3.shape
    nblk = seq_len // RET_BLOCK
    rope = rope_tabs is not None
    has_init = s0 is not None
    qw, vw = hp * RET_DK, hp * RET_DV
    in_specs = [pl.BlockSpec((None, seq_len, qw), lambda b, h: (b, 0, h)),
                pl.BlockSpec((None, seq_len, qw), lambda b, h: (b, 0, RET_QK // qw + h)),
                pl.BlockSpec((None, seq_len, vw), lambda b, h: (b, 0, 2 * RET_QK // vw + h)),
                pl.BlockSpec((None, seq_len, vw), lambda b, h: (b, 0, (2 * RET_QK + RET_V) // vw + h)),
                pl.BlockSpec((hp, 2, 128), lambda b, h: (h, 0, 0))]
    args = [proj3, proj3, proj3, proj3, lg]
    if rope:
        tab = pl.BlockSpec((seq_len, RET_DK), lambda b, h: (0, 0))
        in_specs += [tab, tab]
        args += list(rope_tabs)
    sspec = pl.BlockSpec((None, 2, hp, RET_DK, RET_DV), lambda b, h: (b, 0, h, 0, 0))
    if has_init:
        in_specs.append(sspec)
        args.append(s0)
    out_specs = [pl.BlockSpec((None, seq_len, vw), lambda b, h: (b, 0, h))]
    out_shape = [jax.ShapeDtypeStruct((nseq, seq_len, RET_V), BF16)]
    if want_state:
        out_specs.append(sspec)
        out_shape.append(jax.ShapeDtypeStruct((nseq, 2, RET_HEADS, RET_DK, RET_DV), F32))
    body = functools.partial(_ret_body, seq_len=seq_len, rope=rope, has_init=has_init, want_state=want_state)
    res = pl.pallas_call(
        body,
        grid=(nseq, RET_HEADS // hp),
        in_specs=in_specs,
        out_specs=out_specs,
        out_shape=out_shape,
        scratch_shapes=[pltpu.VMEM((nblk, RET_DK, RET_DV), F32)],
        compiler_params=_params("parallel", "parallel", vmem=VMEM_LIMIT_BYTES),
        name="retention",
    )(*args)
    return res[0], (res[1] if want_state else None)


def rope_tables(seq_len):
    t = jnp.arange(seq_len)
    half = RET_DK // 2
    nfreq = half // 2
    inv = ROPE_BASE ** (-jnp.arange(nfreq, dtype=F32) / nfreq)
    parts_c, parts_s = [], []
    for p in (t // GRID_W, t % GRID_W):
        ang = p.astype(F32)[:, None] * inv[None, :]
        parts_c += [jnp.cos(ang), jnp.cos(ang)]
        parts_s += [-jnp.sin(ang), jnp.sin(ang)]
    return jnp.concatenate(parts_c, axis=-1), jnp.concatenate(parts_s, axis=-1)


def retention_layer(x, g, mod, rows_per_mod, nseq, seq_len, s0, grid_pos, w_in, lg, w_out, tm):
    proj = in_proj(x, g, mod, w_in, rows_per_mod, tm, D_MODEL)
    proj3 = proj.reshape(nseq, seq_len, 2 * RET_QK + 2 * RET_V)
    tabs = rope_tables(seq_len) if grid_pos else None
    hp = RET_HEADS if seq_len <= RET_BLOCK else 1
    y, states = retention_core(proj3, lg, tabs, s0, want_state=s0 is None, hp=hp)
    x_new = out_proj(y.reshape(nseq * seq_len, RET_V), w_out, x, mod, rows_per_mod, min(tm, OUT_ROWS))
    return x_new, states


def _filter_dft_body(a_ref, b_ref, norm_ref, o_ref, *, seq_len, tm):
    row = pl.program_id(0) * tm + lax.broadcasted_iota(jnp.int32, (tm, 1), 0)
    wf = jnp.where(row % (seq_len // 2) == 0, 1.0, 2.0) * (1.0 / (2.0 * seq_len))
    acc = jnp.dot(a_ref[...], b_ref[...], preferred_element_type=F32)
    o_ref[...] = acc * wf / (norm_ref[...] + EPS)


def filter_dft(a, b, norm, tm, tn):
    m, kd = a.shape
    n = b.shape[1]
    body = functools.partial(_filter_dft_body, seq_len=m, tm=tm)
    return pl.pallas_call(
        body,
        grid=(m // tm, n // tn),
        in_specs=[pl.BlockSpec((tm, kd), lambda i, j: (i, 0)),
                  pl.BlockSpec((kd, tn), lambda i, j: (0, j)),
                  pl.BlockSpec((1, tn), lambda i, j: (0, j))],
        out_specs=pl.BlockSpec((tm, tn), lambda i, j: (i, j)),
        out_shape=jax.ShapeDtypeStruct((m, n), F32),
        compiler_params=_params("parallel", "parallel", vmem=VMEM_LIMIT_BYTES),
        name="filter_dft",
    )(a, b, norm)


def _filter_gen_body(feat_ref, w1_ref, b1_ref, w2_ref, b2_ref, w3f_ref, w3b_ref, dl_ref,
                     fs_ref, fd_ref, norm_ref, mre_ref, mim_ref, *, seq_len, tr):
    r = pl.program_id(1)
    hp = lax.Precision.HIGHEST
    z = jnp.sin(jnp.dot(feat_ref[...], w1_ref[...], precision=hp, preferred_element_type=F32) + b1_ref[...])
    z = jnp.sin(jnp.dot(z, w2_ref[...], precision=hp, preferred_element_type=F32) + b2_ref[...]).astype(BF16)
    lag = r * tr + lax.broadcasted_iota(jnp.int32, (tr, 1), 0)
    t = lag.astype(F32) * (1.0 / (seq_len - 1))
    win = jnp.exp(-t * jnp.abs(dl_ref[...]))
    ff = jnp.dot(z, w3f_ref[...], preferred_element_type=F32) * win
    fb = jnp.dot(z, w3b_ref[...], preferred_element_type=F32) * win
    part = jnp.sum(jnp.abs(ff) + jnp.abs(fb), axis=0, keepdims=True)
    fb = jnp.where(lag == 0, 0.0, fb)
    fs = ff + fb
    fd = ff - fb
    quarter = jnp.where(lag % 4 < 2, 1.0, -1.0)
    cos_q = jnp.where(lag % 2 == 0, quarter, 0.0)
    sin_q = jnp.where(lag % 2 == 1, quarter, 0.0)

    @pl.when(r == 0)
    def _():
        norm_ref[...] = jnp.zeros_like(norm_ref)
        mre_ref[...] = jnp.zeros_like(mre_ref)
        mim_ref[...] = jnp.zeros_like(mim_ref)

    norm_ref[...] += part
    mre_ref[...] += jnp.sum(cos_q * fs, axis=0, keepdims=True)
    mim_ref[...] -= jnp.sum(sin_q * fd, axis=0, keepdims=True)
    fs_ref[...] = fs.astype(fs_ref.dtype)
    fd_ref[...] = fd.astype(fd_ref.dtype)


def filter_gen(feat, w1, b1, w2, b2, w3, deltas, tr, tn):
    seq_len, kf = feat.shape
    hid = w1.shape[1]
    ncol = HY_ORDER * HY_WIDTH
    nj = ncol // tn
    body = functools.partial(_filter_gen_body, seq_len=seq_len, tr=tr)
    full = lambda shape: pl.BlockSpec(shape, lambda j, r: (0, 0))
    row = pl.BlockSpec((1, tn), lambda j, r: (0, j))
    return pl.pallas_call(
        body,
        grid=(nj, seq_len // tr),
        in_specs=[pl.BlockSpec((tr, kf), lambda j, r: (r, 0)),
                  full((kf, hid)), full((1, hid)), full((hid, hid)), full((1, hid)),
                  pl.BlockSpec((hid, tn), lambda j, r: (0, j)),
                  pl.BlockSpec((hid, tn), lambda j, r: (0, nj + j)),
                  pl.BlockSpec((1, tn), lambda j, r: (0, j % (HY_WIDTH // tn)))],
        out_specs=[pl.BlockSpec((tr, tn), lambda j, r: (r, j)),
                   pl.BlockSpec((tr, tn), lambda j, r: (r, j)), row, row, row],
        out_shape=[jax.ShapeDtypeStruct((seq_len, ncol), BF16), jax.ShapeDtypeStruct((seq_len, ncol), BF16)]
        + [jax.ShapeDtypeStruct((1, ncol), F32)] * 3,
        compiler_params=_params("parallel", "arbitrary", vmem=VMEM_LIMIT_BYTES),
        name="filter_gen",
    )(feat, w1, b1, w2, b2, w3, w3, deltas)


def _cos_sin(freq, time, seq_len):
    prod = (freq[:, None] * time[None, :]) % (2 * seq_len)
    ang = prod.astype(F32) * (math.pi / seq_len)
    return jnp.cos(ang).astype(BF16), (-jnp.sin(ang)).astype(BF16)


def dft_matrices(seq_len):
    half = seq_len // 2
    t = jnp.arange(seq_len, dtype=jnp.int32)
    c_lo, s_lo = _cos_sin(jnp.arange(half, dtype=jnp.int32), t, seq_len)
    alt = jnp.where(t % 2 == 0, 1.0, -1.0).astype(BF16)[None, :]
    filt = (jnp.concatenate([c_lo, c_lo * alt], axis=0), jnp.concatenate([s_lo, -(s_lo * alt)], axis=0))
    c_e, c_o, s_e, s_o = c_lo[:, 0::2], c_lo[:, 1::2], s_lo[:, 0::2], s_lo[:, 1::2]
    return filt, (c_e, c_o, s_e, s_o, jnp.transpose(c_o), jnp.transpose(s_o))


def hyena_filter_spectrum(seq_len, cm, sm, w1, b1, w2, b2, w3):
    t = jnp.linspace(0.0, 1.0, seq_len, dtype=F32)[:, None]
    w = 2.0 * math.pi * jnp.arange(seq_len, dtype=F32)[:, None] / seq_len
    f = jnp.linspace(1e-4, HY_BANDS - 1.0, HY_BANDS, dtype=F32)[None, :]
    feat = jnp.concatenate([t, jnp.cos(f * w), -jnp.sin(f * w)], axis=-1)
    emb, hid = w1.shape
    pe, ph = HY_LANES - emb, HY_LANES - hid
    feat = jnp.pad(feat, ((0, 0), (0, pe)))
    w1p = jnp.pad(w1, ((0, pe), (0, ph)))
    w2p = jnp.pad(w2, ((0, ph), (0, ph)))
    w3p = jnp.pad(w3, ((0, ph), (0, 0))).astype(BF16)
    b1p = jnp.pad(b1, (0, ph)).reshape(1, HY_LANES)
    b2p = jnp.pad(b2, (0, ph)).reshape(1, HY_LANES)
    max_decay = math.log(HY_DECAY_TARGET) / HY_SHORT_DECAY_PCT
    min_decay = math.log(HY_DECAY_TARGET) / HY_LONG_DECAY_PCT
    deltas = jnp.linspace(min_decay, max_decay, HY_WIDTH, dtype=F32).reshape(1, HY_WIDTH)
    tile = min(seq_len, 512)
    fs, fd, norm, mid_re, mid_im = filter_gen(feat, w1p, b1p, w2p, b2p, w3p, deltas, tile, 1024)
    kr = filter_dft(cm, fs, norm, tile, 512)
    ki = filter_dft(sm, fd, norm, tile, 512)
    kmid = jnp.stack([mid_re, mid_im], axis=0) / (norm + EPS) / seq_len
    kmid = jnp.transpose(kmid.reshape(2, HY_ORDER, HY_WIDTH), (1, 0, 2)).reshape(2 * HY_ORDER, HY_WIDTH)
    return kr, ki, kmid


def _hyena_body(v_ref, x1_ref, x2_ref, gate_ref, cw_ref, cb_ref, kr0_ref, ki0_ref, kr1_ref, ki1_ref,
                km_ref, sk_ref, ce_ref, co_ref, se_ref, so_ref, cot_ref, sot_ref, o_ref,
                z_scr, xg_scr, p_scr, *, seq_len):
    half = seq_len // 2
    cb = o_ref.shape[-1]
    row = lax.broadcasted_iota(jnp.int32, (half, 1), 0)
    alt = jnp.where(row % 2 == 0, 1.0, -1.0).astype(F32)

    def halves(ref):
        return (ref[:, 0].astype(F32).reshape(half, cb), ref[:, 1].astype(F32).reshape(half, cb))

    def short_conv(x_ref, which, dst):
        xe, xo = halves(x_ref)
        w0, w1, w2 = (cw_ref[j, which:which + 1, :] for j in range(HY_SHORT))
        bias = cb_ref[which:which + 1, :]
        xo_prev = jnp.where(row == 0, 0.0, pltpu.roll(xo, 1, 0))
        xe_next = jnp.where(row == half - 1, 0.0, pltpu.roll(xe, half - 1, 0))
        dst[0] = bias + xo_prev * w0 + xe * w1 + xo * w2
        dst[1] = bias + xe * w0 + xo * w1 + xe_next * w2

    rb = min(half, HY_ROW_BLOCK)
    short_conv(v_ref, 0, z_scr)
    for o, (xg_ref, kr_ref, ki_ref) in enumerate(((x1_ref, kr0_ref, ki0_ref), (x2_ref, kr1_ref, ki1_ref))):
        ze, zo = z_scr[0], z_scr[1]
        zeb, zob = ze.astype(BF16), zo.astype(BF16)
        zm_re = jnp.sum(ze * alt, axis=0, keepdims=True)
        zm_im = -jnp.sum(zo * alt, axis=0, keepdims=True)
        km_re, km_im = km_ref[2 * o:2 * o + 1, :], km_ref[2 * o + 1:2 * o + 2, :]
        pm_re = zm_re * km_re - zm_im * km_im
        pm_im = zm_re * km_im + zm_im * km_re
        for r in range(0, half, rb):
            a_e = jnp.dot(ce_ref[r:r + rb, :], zeb, preferred_element_type=F32)
            a_o = jnp.dot(co_ref[r:r + rb, :], zob, preferred_element_type=F32)
            b_e = jnp.dot(se_ref[r:r + rb, :], zeb, preferred_element_type=F32)
            b_o = jnp.dot(so_ref[r:r + rb, :], zob, preferred_element_type=F32)
            lo_re, lo_im, hi_re, hi_im = a_e + a_o, b_e + b_o, a_e - a_o, b_o - b_e
            kl_re, kl_im = kr_ref[r:r + rb, :], ki_ref[r:r + rb, :]
            kh_re, kh_im = kr_ref[half + r:half + r + rb, :], ki_ref[half + r:half + r + rb, :]
            pl_re, pl_im = lo_re * kl_re - lo_im * kl_im, lo_re * kl_im + lo_im * kl_re
            ph_re, ph_im = hi_re * kh_re - hi_im * kh_im, hi_re * kh_im + hi_im * kh_re
            p_scr[0, r:r + rb, :] = (pl_re + ph_re).astype(BF16)
            p_scr[1, r:r + rb, :] = (pl_im - ph_im).astype(BF16)
            p_scr[2, r:r + rb, :] = (pl_re - ph_re).astype(BF16)
            p_scr[3, r:r + rb, :] = (pl_im + ph_im).astype(BF16)
        short_conv(xg_ref, o + 1, xg_scr)
        sk = sk_ref[o:o + 1, :]
        for r in range(0, half, rb):
            y_e = (jnp.dot(ce_ref[r:r + rb, :], p_scr[0], preferred_element_type=F32)
                   + jnp.dot(se_ref[r:r + rb, :], p_scr[1], preferred_element_type=F32)
                   + alt[r:r + rb] * pm_re)
            y_o = (jnp.dot(cot_ref[r:r + rb, :], p_scr[2], preferred_element_type=F32)
                   + jnp.dot(sot_ref[r:r + rb, :], p_scr[3], preferred_element_type=F32)
                   - alt[r:r + rb] * pm_im)
            z_scr[0, r:r + rb, :] = xg_scr[0, r:r + rb, :] * (y_e + z_scr[0, r:r + rb, :] * sk)
            z_scr[1, r:r + rb, :] = xg_scr[1, r:r + rb, :] * (y_o + z_scr[1, r:r + rb, :] * sk)
    ge, go = halves(gate_ref)
    tiles = o_ref.shape[0]
    o_ref[:, 0] = (z_scr[0] * _silu(ge)).astype(o_ref.dtype).reshape(tiles, HY_PARITY_ROWS // 2, cb)
    o_ref[:, 1] = (z_scr[1] * _silu(go)).astype(o_ref.dtype).reshape(tiles, HY_PARITY_ROWS // 2, cb)


def hyena_core(proj5, conv_w, conv_b, kr, ki, kmid, skip, mats, cb):
    nseq, tiles, _, prow, _ = proj5.shape
    seq_len = tiles * 2 * prow
    half = seq_len // 2
    nj = HY_WIDTH // cb
    once = pl.Buffered(1)

    def act(k):
        return pl.BlockSpec((None, tiles, 2, prow, cb), lambda j, b: (b, 0, 0, 0, k * nj + j))

    def spec(k):
        return pl.BlockSpec((seq_len, cb), lambda j, b: (0, k * nj + j), pipeline_mode=once)

    dft = pl.BlockSpec((half, half), lambda j, b: (0, 0), pipeline_mode=once)
    body = functools.partial(_hyena_body, seq_len=seq_len)
    return pl.pallas_call(
        body,
        grid=(nj, nseq),
        in_specs=[act(0), act(1), act(2), act(3),
                  pl.BlockSpec((HY_SHORT, 3, cb), lambda j, b: (0, 0, j)),
                  pl.BlockSpec((3, cb), lambda j, b: (0, j)),
                  spec(0), spec(0), spec(1), spec(1),
                  pl.BlockSpec((2 * HY_ORDER, cb), lambda j, b: (0, j)),
                  pl.BlockSpec((HY_ORDER, cb), lambda j, b: (0, j))] + [dft] * 6,
        out_specs=pl.BlockSpec((None, tiles, 2, prow, cb), lambda j, b: (b, 0, 0, 0, j)),
        out_shape=jax.ShapeDtypeStruct((nseq, tiles, 2, prow, HY_WIDTH), BF16),
        scratch_shapes=[pltpu.VMEM((2, half, cb), F32), pltpu.VMEM((2, half, cb), F32),
                        pltpu.VMEM((4, half, cb), BF16)],
        compiler_params=_params("parallel", "arbitrary", vmem=VMEM_LIMIT_BYTES),
        name="hyena",
    )(proj5, proj5, proj5, proj5,
      conv_w.reshape(HY_SHORT, 3, HY_WIDTH), conv_b.reshape(3, HY_WIDTH),
      kr, ki, kr, ki, kmid, skip, *mats)


def parity_permutation():
    i = jnp.arange(HY_PARITY_ROWS)
    h = HY_PARITY_ROWS // 2
    src = jnp.where(i < h, 2 * i, 2 * (i - h) + 1)
    p = (src[:, None] == jnp.arange(HY_PARITY_ROWS)[None, :]).astype(BF16)
    return p, jnp.transpose(p)


def _inproj_parity_body(x_ref, g_ref, mod_ref, p_ref, w_ref, o_ref, h_scr):
    @pl.when(pl.program_id(1) == 0)
    def _():
        x = x_ref[...]
        y = x * lax.rsqrt(jnp.mean(x * x, axis=-1, keepdims=True) + EPS) * g_ref[...]
        h = (y * (1.0 + mod_ref[1:2, :]) + mod_ref[0:1, :]).astype(BF16)
        for r in range(0, x.shape[0], HY_PARITY_ROWS):
            h_scr[r:r + HY_PARITY_ROWS, :] = jnp.dot(p_ref[...], h[r:r + HY_PARITY_ROWS],
                                                     preferred_element_type=F32).astype(BF16)

    out = jnp.dot(h_scr[...], w_ref[...], preferred_element_type=F32).astype(o_ref.dtype)
    o_ref[...] = out.reshape(o_ref.shape)


def in_proj_parity(x, g, mod, p, w, rows_per_mod, tm, tn):
    n, nout = x.shape[0], w.shape[1]
    pr = HY_PARITY_ROWS
    return pl.pallas_call(
        _inproj_parity_body,
        grid=(n // tm, nout // tn),
        in_specs=[pl.BlockSpec((tm, D_MODEL), lambda i, j: (i, 0)),
                  pl.BlockSpec((1, D_MODEL), lambda i, j: (0, 0)),
                  pl.BlockSpec((None, 3, D_MODEL), lambda i, j: ((i * tm) // rows_per_mod, 0, 0)),
                  pl.BlockSpec((pr, pr), lambda i, j: (0, 0)),
                  pl.BlockSpec((D_MODEL, tn), lambda i, j: (0, j))],
        out_specs=pl.BlockSpec((tm // pr, 2, pr // 2, tn), lambda i, j: (i, 0, 0, j)),
        out_shape=jax.ShapeDtypeStruct((n // pr, 2, pr // 2, nout), BF16),
        scratch_shapes=[pltpu.VMEM((tm, D_MODEL), BF16)],
        compiler_params=_params("parallel", "arbitrary", vmem=VMEM_LIMIT_BYTES),
        name="in_proj_parity",
    )(x, g.reshape(1, D_MODEL), mod, p, w)


def _outproj_parity_body(y_ref, pt_ref, w_ref, x_ref, mod_ref, o_ref):
    tm = x_ref.shape[0]
    y = y_ref[...].reshape(tm, y_ref.shape[-1])
    parts = [jnp.dot(pt_ref[...], y[r:r + HY_PARITY_ROWS], preferred_element_type=F32).astype(BF16)
             for r in range(0, tm, HY_PARITY_ROWS)]
    out = jnp.dot(jnp.concatenate(parts, axis=0), w_ref[...], preferred_element_type=F32)
    o_ref[...] = x_ref[...] + mod_ref[2:3, :] * out


def out_proj_parity(y, pt, w, x, mod, rows_per_mod, tm):
    n = x.shape[0]
    width = y.shape[-1]
    pr = HY_PARITY_ROWS
    return pl.pallas_call(
        _outproj_parity_body,
        grid=(n // tm,),
        in_specs=[pl.BlockSpec((tm // pr, 2, pr // 2, width), lambda i: (i, 0, 0, 0)),
                  pl.BlockSpec((pr, pr), lambda i: (0, 0)),
                  pl.BlockSpec((width, D_MODEL), lambda i: (0, 0)),
                  pl.BlockSpec((tm, D_MODEL), lambda i: (i, 0)),
                  pl.BlockSpec((None, 3, D_MODEL), lambda i: ((i * tm) // rows_per_mod, 0, 0))],
        out_specs=pl.BlockSpec((tm, D_MODEL), lambda i: (i, 0)),
        out_shape=jax.ShapeDtypeStruct((n, D_MODEL), F32),
        compiler_params=_params("parallel", vmem=VMEM_LIMIT_BYTES),
        name="out_proj_parity",
    )(y, pt, w, x, mod)


def hyena_layer(x, g, mod, rows_per_mod, nseq, seq_len, w_in, conv_w, conv_b, filt_w, skip, w_out, perms, tm, cb):
    par_p, par_pt = perms
    pr = HY_PARITY_ROWS
    proj = in_proj_parity(x, g, mod, par_p, w_in, rows_per_mod, tm, D_MODEL)
    proj5 = proj.reshape(nseq, seq_len // pr, 2, pr // 2, 4 * HY_WIDTH)
    (cfil, sfil), mats = dft_matrices(seq_len)
    kr, ki, kmid = hyena_filter_spectrum(seq_len, cfil, sfil, *filt_w)
    y = hyena_core(proj5, conv_w, conv_b, kr, ki, kmid, skip, mats, cb)
    y = y.reshape(nseq * seq_len // pr, 2, pr // 2, HY_WIDTH)
    return out_proj_parity(y, par_pt, w_out, x, mod, rows_per_mod, min(tm, OUT_ROWS))


def kernel(x_prompt, x_sample, c, state_s5, state_ret, c_ctx, norm_g, mod_w, mod_b, s5_w_in, s5_lam_re, s5_lam_im, s5_log_step, s5_b_re, s5_b_im, s5_c_re, s5_c_im, s5_d, s5_w_glu, s5_b_glu, s5_w_out, ret_w_in, ret_decay_logit, ret_w_out, hy_w_in, hy_conv_w, hy_conv_b, hy_f_w1, hy_f_b1, hy_f_w2, hy_f_b2, hy_f_w3, hy_skip, hy_w_out, final_g):
    n_ctx, l_ctx, _ = x_prompt.shape
    n_dec, l_dec, _ = x_sample.shape
    xc = x_prompt.reshape(n_ctx * l_ctx, D_MODEL)
    xl = x_sample.reshape(n_dec * l_dec, D_MODEL)

    pad = (-(1 + n_dec)) % 8
    cvecs = jnp.concatenate([c_ctx[None, :], c, jnp.zeros((pad, D_MODEL), F32)], axis=0)
    mods = ada_mod_all(cvecs, mod_w, mod_b).reshape(DEPTH, -1, 3, D_MODEL)

    tm = 1024
    perm = s5_lane_permutation()
    last_is_s5 = (DEPTH - 1) % N_MIXERS == 0
    s5_new, ret_new = [], []
    for i in range(DEPTH):
        kind, j = i % N_MIXERS, i // N_MIXERS
        mod_c = mods[i, 0:1]
        mod_l = mods[i, 1:1 + n_dec]
        g = norm_g[i]
        if kind == 0:
            tables = s5_tables(s5_lam_re[j], s5_lam_im[j], s5_log_step[j], s5_b_re[j], s5_b_im[j],
                               s5_c_re[j], s5_c_im[j], s5_d[j])
            w = (s5_w_in[j].astype(BF16), tables, perm, s5_w_glu[j].astype(BF16), s5_b_glu[j],
                 s5_w_out[j].astype(BF16))
            fg = final_g if last_is_s5 and i == DEPTH - 1 else None
            xc, st = s5_layer(xc, g, mod_c, n_ctx, l_ctx, None, *w, nb=n_ctx, final_g=fg)
            xl, _ = s5_layer(xl, g, mod_l, n_dec, l_dec, state_s5[:, j], *w, nb=S5_SEQS_PER_STEP, final_g=fg)
            s5_new.append(st)
        elif kind == 1:
            w_in = ret_w_in[j]
            kscale = jnp.concatenate([jnp.ones((RET_QK,), F32), jnp.full((RET_QK,), RET_DK ** -0.5, F32),
                                      jnp.ones((2 * RET_V,), F32)])
            w_in = (w_in * kscale[None, :]).astype(BF16)
            lg = jax.nn.log_sigmoid(ret_decay_logit[j])
            lg = jnp.broadcast_to(jnp.transpose(lg)[:, :, None], (RET_HEADS, 2, 128))
            w = (w_in, lg, ret_w_out[j].astype(BF16), tm)
            xc, st = retention_layer(xc, g, mod_c, n_ctx * l_ctx, n_ctx, l_ctx, None, False, *w)
            xl, _ = retention_layer(xl, g, mod_l, l_dec, n_dec, l_dec, state_ret[:, j], True, *w)
            ret_new.append(st)
        else:
            filt_w = (hy_f_w1[j], hy_f_b1[j], hy_f_w2[j], hy_f_b2[j], hy_f_w3[j])
            w = (hy_w_in[j].astype(BF16), hy_conv_w[j], hy_conv_b[j], filt_w, hy_skip[j], hy_w_out[j].astype(BF16),
                 parity_permutation(), tm)
            xc = hyena_layer(xc, g, mod_c, n_ctx * l_ctx, n_ctx, l_ctx, *w, cb=1024)
            xl = hyena_layer(xl, g, mod_l, l_dec, n_dec, l_dec, *w, cb=256)

    if not last_is_s5:
        xc, xl = final_norm(xc, final_g, tm), final_norm(xl, final_g, tm)
    y_prompt = xc.reshape(n_ctx, l_ctx, D_MODEL)
    y_sample = xl.reshape(n_dec, l_dec, D_MODEL)
    new_state_s5 = jnp.stack(s5_new, axis=1)
    new_state_ret = jnp.stack(ret_new, axis=1)
    return (y_prompt, y_sample, new_state_s5, new_state_ret)
```

```python
import functools
import math

import jax
import jax.numpy as jnp
from jax import lax
from jax.experimental import pallas as pl
from jax.experimental.pallas import tpu as pltpu

F32 = jnp.float32
BF16 = jnp.bfloat16

D_MODEL = 1024
DEPTH = 4
N_MIXERS = 3
EPS = 1e-6
GRID_W = 64

S5_GROUP = 16
S5_GROUPS = D_MODEL // S5_GROUP
S5_STATE = 64
S5_CHUNK = 16
S5_TILE = S5_CHUNK * S5_GROUP
S5_SLAB = 128 // S5_GROUP
S5_SEQS_PER_STEP = 8
S5_ROWS = 512

RET_HEADS = 8
RET_QK = D_MODEL
RET_V = 2 * D_MODEL
RET_DK = RET_QK // RET_HEADS
RET_DV = RET_V // RET_HEADS
RET_BLOCK = 256
ROPE_BASE = 10000.0

HY_WIDTH = 2 * D_MODEL
HY_ORDER = 2
HY_SHORT = 3
HY_BANDS = 16
HY_SHORT_DECAY_PCT = 0.3
HY_LONG_DECAY_PCT = 1.5
HY_DECAY_TARGET = 1e-2
HY_LANES = 128
HY_PARITY_ROWS = 256
HY_ROW_BLOCK = 512

VMEM_LIMIT_BYTES = 56 * 1024 * 1024
PROJ_COLS = 2048
OUT_ROWS = 512


def _params(*sem, vmem=None):
    return pltpu.CompilerParams(dimension_semantics=sem, vmem_limit_bytes=vmem)


def _silu(x):
    return x * jax.nn.sigmoid(x)


def _mod_body(c_ref, w_ref, b_ref, o_ref):
    a = _silu(c_ref[...]).astype(BF16)
    o_ref[0] = jnp.dot(a, w_ref[0].astype(BF16), preferred_element_type=F32) + b_ref[0]


def ada_mod_all(cvecs, mod_w, mod_b):
    r = cvecs.shape[0]
    tn = D_MODEL
    return pl.pallas_call(
        _mod_body,
        grid=(DEPTH, 3 * D_MODEL // tn),
        in_specs=[pl.BlockSpec((r, D_MODEL), lambda i, j: (0, 0)),
                  pl.BlockSpec((1, D_MODEL, tn), lambda i, j: (i, 0, j)),
                  pl.BlockSpec((1, 1, tn), lambda i, j: (i, 0, j))],
        out_specs=pl.BlockSpec((1, r, tn), lambda i, j: (i, 0, j)),
        out_shape=jax.ShapeDtypeStruct((DEPTH, r, 3 * D_MODEL), F32),
        compiler_params=_params("parallel", "parallel"),
        name="ada_mod",
    )(cvecs, mod_w, mod_b.reshape(DEPTH, 1, 3 * D_MODEL))


def _inproj_body(x_ref, g_ref, mod_ref, w_ref, o_ref, h_scr):
    @pl.when(pl.program_id(1) == 0)
    def _():
        x = x_ref[...]
        y = x * lax.rsqrt(jnp.mean(x * x, axis=-1, keepdims=True) + EPS) * g_ref[...]
        h_scr[...] = (y * (1.0 + mod_ref[1:2, :]) + mod_ref[0:1, :]).astype(BF16)

    o_ref[...] = jnp.dot(h_scr[...], w_ref[...], preferred_element_type=F32).astype(o_ref.dtype)


def in_proj(x, g, mod, w, rows_per_mod, tm, tn):
    n, nout = x.shape[0], w.shape[1]
    return pl.pallas_call(
        _inproj_body,
        grid=(n // tm, nout // tn),
        in_specs=[pl.BlockSpec((tm, D_MODEL), lambda i, j: (i, 0)),
                  pl.BlockSpec((1, D_MODEL), lambda i, j: (0, 0)),
                  pl.BlockSpec((None, 3, D_MODEL), lambda i, j: ((i * tm) // rows_per_mod, 0, 0)),
                  pl.BlockSpec((D_MODEL, tn), lambda i, j: (0, j))],
        out_specs=pl.BlockSpec((tm, tn), lambda i, j: (i, j)),
        out_shape=jax.ShapeDtypeStruct((n, nout), BF16),
        scratch_shapes=[pltpu.VMEM((tm, D_MODEL), BF16)],
        compiler_params=_params("parallel", "arbitrary", vmem=VMEM_LIMIT_BYTES),
        name="in_proj",
    )(x, g.reshape(1, D_MODEL), mod, w)


def _outproj_body(y_ref, w_ref, x_ref, mod_ref, o_ref):
    out = jnp.dot(y_ref[...], w_ref[...], preferred_element_type=F32)
    o_ref[...] = x_ref[...] + mod_ref[2:3, :] * out


def out_proj(y, w, x, mod, rows_per_mod, tm):
    n, width = y.shape
    return pl.pallas_call(
        _outproj_body,
        grid=(n // tm,),
        in_specs=[pl.BlockSpec((tm, width), lambda i: (i, 0)),
                  pl.BlockSpec((width, D_MODEL), lambda i: (0, 0)),
                  pl.BlockSpec((tm, D_MODEL), lambda i: (i, 0)),
                  pl.BlockSpec((None, 3, D_MODEL), lambda i: ((i * tm) // rows_per_mod, 0, 0))],
        out_specs=pl.BlockSpec((tm, D_MODEL), lambda i: (i, 0)),
        out_shape=jax.ShapeDtypeStruct((n, D_MODEL), F32),
        compiler_params=_params("parallel", vmem=VMEM_LIMIT_BYTES),
        name="out_proj",
    )(y, w, x, mod)


def _final_norm_body(x_ref, g_ref, o_ref):
    x = x_ref[...]
    o_ref[...] = x * lax.rsqrt(jnp.mean(x * x, axis=-1, keepdims=True) + EPS) * g_ref[...]


def final_norm(x, g, tm):
    n = x.shape[0]
    return pl.pallas_call(
        _final_norm_body,
        grid=(n // tm,),
        in_specs=[pl.BlockSpec((tm, D_MODEL), lambda i: (i, 0)),
                  pl.BlockSpec((1, D_MODEL), lambda i: (0, 0))],
        out_specs=pl.BlockSpec((tm, D_MODEL), lambda i: (i, 0)),
        out_shape=jax.ShapeDtypeStruct((n, D_MODEL), F32),
        compiler_params=_params("parallel"),
        name="final_norm",
    )(x, g.reshape(1, D_MODEL))


def s5_tables(lam_re, lam_im, log_step, b_re, b_im, c_re, c_im, d_skip):
    t_len = S5_CHUNK
    dt = jnp.exp(log_step)[..., None]
    ab_re = jnp.exp(lam_re * dt) * jnp.cos(lam_im * dt)
    ab_im = jnp.exp(lam_re * dt) * jnp.sin(lam_im * dt)
    den = lam_re * lam_re + lam_im * lam_im
    nr, ni = ab_re - 1.0, ab_im
    f_re = (nr * lam_re + ni * lam_im) / den
    f_im = (ni * lam_re - nr * lam_im) / den
    bb_re = f_re[..., None] * b_re - f_im[..., None] * b_im
    bb_im = f_re[..., None] * b_im + f_im[..., None] * b_re
    ks = jnp.arange(t_len + 1, dtype=F32)[:, None, None, None]
    pw_mag = jnp.exp(ks * (lam_re * dt)[None])
    pw_re = pw_mag * jnp.cos(ks * (lam_im * dt)[None])
    pw_im = pw_mag * jnp.sin(ks * (lam_im * dt)[None])

    npair = S5_GROUPS // 2
    pair_eye = jnp.eye(2, dtype=F32)

    def pair_pack(a):
        a = a.reshape(2, npair, 2, S5_GROUP, S5_STATE)
        a = a[:, :, :, :, None, :] * pair_eye[None, None, :, None, :, None]
        return jnp.transpose(a.reshape(2, npair, 2, S5_GROUP, 2 * S5_STATE), (1, 0, 2, 3, 4))

    bbt = jnp.stack([pair_pack(jnp.swapaxes(bb_re, 2, 3)), pair_pack(jnp.swapaxes(bb_im, 2, 3))], axis=2)
    cpk = jnp.stack([pair_pack(c_re), pair_pack(c_im)], axis=2)
    pw = jnp.stack([pw_re, pw_im], axis=0).reshape(2, t_len + 1, 2, npair, 2 * S5_STATE)
    pw = jnp.transpose(pw, (3, 2, 0, 1, 4))
    dmat = jnp.eye(S5_GROUP, dtype=F32) * d_skip.reshape(S5_GROUPS, 1, S5_GROUP)
    dmat = jnp.pad(dmat, ((0, 0), (0, 0), (0, S5_TILE - S5_GROUP))).reshape(npair, 2, S5_GROUP, S5_TILE)
    m, f, et = s5_table_kernel(bbt, cpk, pw, dmat)

    dec = jnp.stack([pw_re[t_len, 0], pw_im[t_len, 0], pw_re[t_len, 1], pw_im[t_len, 1]], axis=0)
    dec = jnp.transpose(dec.reshape(4, S5_GROUPS // S5_SLAB, S5_SLAB * S5_STATE), (1, 0, 2))
    return m.reshape(S5_GROUPS, S5_TILE, S5_TILE), f, et, dec


def _s5_table_body(bb_ref, c_ref, pw_ref, d_ref, m_ref, f_ref, et_ref):
    t_len = S5_CHUNK
    nt_dims = (((1,), (1,)), ((), ()))
    hp = lax.Precision.HIGHEST

    def times_power(ref, d, g, k):
        pr, pi = pw_ref[d, 0, k:k + 1, :], pw_ref[d, 1, k:k + 1, :]
        ar, ai = ref[d, 0, g], ref[d, 1, g]
        return ar * pr - ai * pi, ar * pi + ai * pr

    def stack(parts):
        return jnp.concatenate(parts, axis=0)

    for d in range(2):
        f_pow = [t_len - 1 - s for s in range(t_len)] if d == 0 else list(range(t_len))
        e_pow = [s + 1 for s in range(t_len)] if d == 0 else [t_len - s for s in range(t_len)]
        fr, fi, er, ei = [], [], [], []
        for g in range(2):
            for s in range(t_len):
                a, b = times_power(bb_ref, d, g, f_pow[s])
                fr.append(a)
                fi.append(b)
                a, b = times_power(c_ref, d, g, e_pow[s])
                er.append(a)
                ei.append(-b)
        lanes = 2 * S5_STATE
        f_ref[:, (2 * d) * lanes:(2 * d + 1) * lanes] = stack(fr).astype(f_ref.dtype)
        f_ref[:, (2 * d + 1) * lanes:(2 * d + 2) * lanes] = stack(fi).astype(f_ref.dtype)
        et_ref[:, (2 * d) * lanes:(2 * d + 1) * lanes] = stack(er).astype(et_ref.dtype)
        et_ref[:, (2 * d + 1) * lanes:(2 * d + 2) * lanes] = stack(ei).astype(et_ref.dtype)

    zeros = jnp.zeros((S5_GROUP, S5_TILE), F32)
    for g in range(2):
        def lag_kernels(d, powers):
            car, cai = zip(*(times_power(c_ref, d, g, k) for k in powers))
            return (lax.dot_general(bb_ref[d, 0, g], stack(car), nt_dims, precision=hp, preferred_element_type=F32)
                    - lax.dot_general(bb_ref[d, 1, g], stack(cai), nt_dims, precision=hp,
                                      preferred_element_type=F32))
        v_f = jnp.concatenate([zeros, lag_kernels(0, range(t_len)) + d_ref[g]], axis=1)
        v_b = jnp.concatenate([lag_kernels(1, range(t_len - 1, -1, -1)), zeros], axis=1)
        width = 2 * S5_TILE
        for s in range(t_len):
            blk_f = pltpu.roll(v_f, (width - (t_len - s) * S5_GROUP) % width, 1)[:, :S5_TILE]
            blk_b = pltpu.roll(v_b, (width - (t_len - 1 - s) * S5_GROUP) % width, 1)[:, :S5_TILE]
            m_ref[g, s * S5_GROUP:(s + 1) * S5_GROUP, :] = (blk_f + blk_b).astype(m_ref.dtype)


def s5_table_kernel(bbt, cpk, pw, dmat):
    npair = S5_GROUPS // 2
    pspec = pl.BlockSpec((None, 2, 2, 2, S5_GROUP, 2 * S5_STATE), lambda k: (k, 0, 0, 0, 0, 0))
    wide = pl.BlockSpec((None, 2 * S5_TILE, 2 * S5_TILE), lambda k: (k, 0, 0))
    return pl.pallas_call(
        _s5_table_body,
        grid=(npair,),
        in_specs=[pspec, pspec,
                  pl.BlockSpec((None, 2, 2, S5_CHUNK + 1, 2 * S5_STATE), lambda k: (k, 0, 0, 0, 0)),
                  pl.BlockSpec((None, 2, S5_GROUP, S5_TILE), lambda k: (k, 0, 0, 0))],
        out_specs=[pl.BlockSpec((None, 2, S5_TILE, S5_TILE), lambda k: (k, 0, 0, 0)), wide, wide],
        out_shape=[jax.ShapeDtypeStruct((npair, 2, S5_TILE, S5_TILE), BF16),
                   jax.ShapeDtypeStruct((npair, 2 * S5_TILE, 2 * S5_TILE), BF16),
                   jax.ShapeDtypeStruct((npair, 2 * S5_TILE, 2 * S5_TILE), BF16)],
        compiler_params=_params("parallel"),
        name="s5_tables",
    )(bbt, cpk, pw, dmat)


def s5_lane_permutation():
    width = S5_CHUNK * S5_SLAB * S5_GROUP
    i = jnp.arange(width)
    step, grp, ch = i // (S5_SLAB * S5_GROUP), (i // S5_GROUP) % S5_SLAB, i % S5_GROUP
    dst = grp * S5_TILE + step * S5_GROUP + ch
    return (dst[:, None] == jnp.arange(width)[None, :]).astype(BF16)


def _s5_scan_body(u_ref, q_ref, m_ref, f_ref, et_ref, dec_ref, h0_ref, y_ref, hf_ref,
                  uy_s, st_s, *, nchunk, nseq):
    npair = S5_SLAB // 2
    pair_w = 2 * S5_TILE
    lane_w = 2 * S5_STATE
    nt_dims = (((1,), (1,)), ((), ()))
    xcat = jnp.concatenate([u_ref[s] for s in range(S5_CHUNK)], axis=1)
    for k in range(npair):
        cols = slice(k * pair_w, (k + 1) * pair_w)
        up = jnp.dot(xcat, q_ref[:, cols], preferred_element_type=F32).astype(BF16)
        uy_s[:, cols] = up
        x = jnp.dot(up, f_ref[k], preferred_element_type=F32)
        for q in range(4):
            st_s[q * npair + k] = x[:, q * lane_w:(q + 1) * lane_w]

    def lanes_of(ref, q, k):
        return ref[q, :, k * lane_w:(k + 1) * lane_w]

    dec = [[dec_ref[q:q + 1, k * lane_w:(k + 1) * lane_w] for k in range(npair)] for q in range(4)]

    def step(i, carry):
        rows_f = pl.ds(pl.multiple_of(i * nseq, nseq), nseq)
        rows_b = pl.ds(pl.multiple_of((nchunk - 1 - i) * nseq, nseq), nseq)
        new = []
        for k in range(npair):
            fr, fi, br, bi = (carry[q * npair + k] for q in range(4))
            xr, xi = st_s[0 * npair + k, rows_f, :], st_s[1 * npair + k, rows_f, :]
            yr, yi = st_s[2 * npair + k, rows_b, :], st_s[3 * npair + k, rows_b, :]
            st_s[0 * npair + k, rows_f, :] = fr
            st_s[1 * npair + k, rows_f, :] = fi
            st_s[2 * npair + k, rows_b, :] = br
            st_s[3 * npair + k, rows_b, :] = bi
            new.append((dec[0][k] * fr - dec[1][k] * fi + xr, dec[0][k] * fi + dec[1][k] * fr + xi,
                        dec[2][k] * br - dec[3][k] * bi + yr, dec[2][k] * bi + dec[3][k] * br + yi))
        return tuple(new[k][q] for q in range(4) for k in range(npair))

    init = tuple(lanes_of(h0_ref, q, k) for q in range(4) for k in range(npair))
    final = lax.fori_loop(0, nchunk, step, init)
    for q in range(4):
        for k in range(npair):
            hf_ref[q, :, k * lane_w:(k + 1) * lane_w] = final[q * npair + k]

    for k in range(npair):
        h = jnp.concatenate([st_s[q * npair + k] for q in range(4)], axis=1)
        ye = lax.dot_general(h.astype(BF16), et_ref[k], nt_dims, preferred_element_type=F32)
        for g in range(2):
            cols = slice(k * pair_w + g * S5_TILE, k * pair_w + (g + 1) * S5_TILE)
            y = jnp.dot(uy_s[:, cols], m_ref[2 * k + g], preferred_element_type=F32)
            uy_s[:, cols] = (y + ye[:, g * S5_TILE:(g + 1) * S5_TILE]).astype(BF16)
    slab_w = S5_SLAB * S5_GROUP
    steps = pair_w // slab_w
    for t0 in range(0, S5_CHUNK, steps):
        yp = lax.dot_general(uy_s[...], q_ref[t0 * slab_w:(t0 + steps) * slab_w, :], nt_dims,
                             preferred_element_type=F32).astype(y_ref.dtype)
        for t in range(steps):
            y_ref[t0 + t] = yp[:, t * slab_w:(t + 1) * slab_w]


def s5_scan(u_s, perm, m, f, e, dec, h0, nchunk, nseq, nb):
    rows = nchunk * nb
    nslab = S5_GROUPS // S5_SLAB
    slab_w = S5_SLAB * S5_GROUP
    state_w = S5_SLAB * S5_STATE
    width = S5_CHUNK * slab_w
    body = functools.partial(_s5_scan_body, nchunk=nchunk, nseq=nb)
    hspec = pl.BlockSpec((None, None, 4, nb, state_w), lambda s, b: (s, b, 0, 0, 0))
    state = pltpu.VMEM((2 * S5_SLAB, rows, 2 * S5_STATE), F32)
    once = pl.Buffered(1)
    return pl.pallas_call(
        body,
        grid=(nslab, nseq // nb),
        in_specs=[pl.BlockSpec((S5_CHUNK, rows, slab_w), lambda s, b: (0, b, s), pipeline_mode=once),
                  pl.BlockSpec((width, width), lambda s, b: (0, 0), pipeline_mode=once),
                  pl.BlockSpec((S5_SLAB, S5_TILE, S5_TILE), lambda s, b: (s, 0, 0), pipeline_mode=once),
                  pl.BlockSpec((S5_SLAB // 2, 2 * S5_TILE, 2 * S5_TILE), lambda s, b: (s, 0, 0), pipeline_mode=once),
                  pl.BlockSpec((S5_SLAB // 2, 2 * S5_TILE, 2 * S5_TILE), lambda s, b: (s, 0, 0), pipeline_mode=once),
                  pl.BlockSpec((None, 4, state_w), lambda s, b: (s, 0, 0)),
                  hspec],
        out_specs=[pl.BlockSpec((S5_CHUNK, rows, slab_w), lambda s, b: (0, b, s)), hspec],
        out_shape=[jax.ShapeDtypeStruct((S5_CHUNK, nseq * nchunk, D_MODEL), BF16),
                   jax.ShapeDtypeStruct((nslab, nseq // nb, 4, nb, state_w), F32)],
        scratch_shapes=[pltpu.VMEM((rows, width), BF16), state],
        compiler_params=_params("parallel", "parallel", vmem=VMEM_LIMIT_BYTES),
        name="s5_scan",
    )(u_s, perm, m, f, e, dec, h0)


def s5_row_permutation(nb):
    cpt = S5_ROWS // (S5_CHUNK * nb)
    i = jnp.arange(S5_ROWS)
    step, chunk, seq = i // (cpt * nb), (i // nb) % cpt, i % nb
    src = seq * (cpt * S5_CHUNK) + chunk * S5_CHUNK + step
    p = (src[:, None] == jnp.arange(S5_ROWS)[None, :]).astype(BF16)
    return p, jnp.transpose(p)


def _s5_tile_specs(x3, mod, nb):
    nseq, seq_len, _ = x3.shape
    cpt = S5_ROWS // (S5_CHUNK * nb)
    ctiles = seq_len // (cpt * S5_CHUNK)
    grid = (nseq // nb, ctiles)
    xspec = pl.BlockSpec((nb, cpt * S5_CHUNK, D_MODEL), lambda b, c: (b, c, 0))
    if mod.shape[0] == 1:
        mspec = pl.BlockSpec((1, 3, D_MODEL), lambda b, c: (0, 0, 0))
    else:
        mspec = pl.BlockSpec((nb, 3, D_MODEL), lambda b, c: (b, 0, 0))

    def stepped(width, col):
        return pl.BlockSpec((S5_CHUNK, S5_ROWS // S5_CHUNK, width), lambda b, c: (0, b * ctiles + c, col))

    return grid, xspec, mspec, stepped


def _inproj_steps_body(x_ref, g_ref, mod_ref, p_ref, w_ref, o_ref):
    x = x_ref[...]
    y = x * lax.rsqrt(jnp.mean(x * x, axis=-1, keepdims=True) + EPS) * g_ref[...]
    h = (y * (1.0 + mod_ref[:, 1:2, :]) + mod_ref[:, 0:1, :]).astype(BF16).reshape(S5_ROWS, D_MODEL)
    h = jnp.dot(p_ref[...], h, preferred_element_type=F32).astype(BF16)
    out = jnp.dot(h, w_ref[...], preferred_element_type=F32).astype(o_ref.dtype)
    o_ref[...] = out.reshape(o_ref.shape)


def in_proj_steps(x3, g, mod, p, w, nb):
    nseq, seq_len, _ = x3.shape
    nout = w.shape[1]
    grid, xspec, mspec, stepped = _s5_tile_specs(x3, mod, nb)
    return pl.pallas_call(
        _inproj_steps_body,
        grid=grid,
        in_specs=[xspec,
                  pl.BlockSpec((1, D_MODEL), lambda b, c: (0, 0)),
                  mspec,
                  pl.BlockSpec((S5_ROWS, S5_ROWS), lambda b, c: (0, 0)),
                  pl.BlockSpec((D_MODEL, nout), lambda b, c: (0, 0))],
        out_specs=stepped(nout, 0),
        out_shape=jax.ShapeDtypeStruct((S5_CHUNK, nseq * seq_len // S5_CHUNK, nout), BF16),
        compiler_params=_params("parallel", "parallel", vmem=VMEM_LIMIT_BYTES),
        name="in_proj_steps",
    )(x3, g.reshape(1, D_MODEL), mod, p, w)


def _s5_out_body(y_ref, gate_ref, wg_ref, bg_ref, wo_ref, pt_ref, x_ref, mod_ref, *rest):
    o_ref = rest[-1]
    g = jax.nn.gelu(y_ref[...].astype(F32).reshape(S5_ROWS, D_MODEL))
    t = jnp.dot(g.astype(BF16), wg_ref[...], preferred_element_type=F32) + bg_ref[...]
    z = g * jax.nn.sigmoid(t) * _silu(gate_ref[...].astype(F32).reshape(S5_ROWS, D_MODEL))
    z = jnp.dot(pt_ref[...], z.astype(BF16), preferred_element_type=F32).astype(BF16)
    out = jnp.dot(z, wo_ref[...], preferred_element_type=F32)
    x = x_ref[...] + mod_ref[:, 2:3, :] * out.reshape(x_ref.shape)
    if len(rest) == 2:
        x = x * lax.rsqrt(jnp.mean(x * x, axis=-1, keepdims=True) + EPS) * rest[0][...]
    o_ref[...] = x


def s5_out(y_s, u_s, w_glu, b_glu, w_out, pt, x3, mod, nb, final_g=None):
    grid, xspec, mspec, stepped = _s5_tile_specs(x3, mod, nb)
    wspec = pl.BlockSpec((D_MODEL, D_MODEL), lambda b, c: (0, 0))
    vspec = pl.BlockSpec((1, D_MODEL), lambda b, c: (0, 0))
    in_specs = [stepped(D_MODEL, 0), stepped(D_MODEL, 1), wspec, vspec, wspec,
                pl.BlockSpec((S5_ROWS, S5_ROWS), lambda b, c: (0, 0)), xspec, mspec]
    args = [y_s, u_s, w_glu, b_glu.reshape(1, D_MODEL), w_out, pt, x3, mod]
    if final_g is not None:
        in_specs.append(vspec)
        args.append(final_g.reshape(1, D_MODEL))
    return pl.pallas_call(
        _s5_out_body,
        grid=grid,
        in_specs=in_specs,
        out_specs=xspec,
        out_shape=jax.ShapeDtypeStruct(x3.shape, F32),
        compiler_params=_params("parallel", "parallel", vmem=VMEM_LIMIT_BYTES),
        name="s5_out",
    )(*args)


def s5_layer(x, g, mod, nseq, seq_len, h0, w_in, tables, perm, w_glu, b_glu, w_out, nb, final_g=None):
    nchunk = seq_len // S5_CHUNK
    nslab = S5_GROUPS // S5_SLAB
    state_w = S5_SLAB * S5_STATE
    row_p, row_pt = s5_row_permutation(nb)
    x3 = x.reshape(nseq, seq_len, D_MODEL)
    u_s = in_proj_steps(x3, g, mod, row_p, w_in, nb)
    if h0 is None:
        h0_p = jnp.zeros((nslab, nseq // nb, 4, nb, state_w), F32)
    else:
        h0_p = jnp.transpose(h0.reshape(nseq // nb, nb, 4, nslab, state_w), (3, 0, 2, 1, 4))
    m, f, e, dec = tables
    y_s, hf = s5_scan(u_s, perm, m, f, e, dec, h0_p, nchunk, nseq, nb)
    x_new = s5_out(y_s, u_s, w_glu, b_glu, w_out, row_pt, x3, mod, nb, final_g).reshape(nseq * seq_len, D_MODEL)
    states = jnp.transpose(hf, (1, 3, 2, 0, 4)).reshape(nseq, 2, 2, S5_GROUPS, S5_STATE)
    return x_new, states


def _ret_body(*refs, seq_len, rope, has_init, want_state):
    refs = list(refs)
    q_ref, k_ref, v_ref, gate_ref, lg_ref = refs[:5]
    pos = 5
    if rope:
        cos_ref, sin_ref = refs[pos:pos + 2]
        pos += 2
    if has_init:
        s0_ref = refs[pos]
        pos += 1
    o_ref = refs[pos]
    pos += 1
    if want_state:
        sfin_ref = refs[pos]
        pos += 1
    sf_scr = refs[pos]

    c_len = RET_BLOCK
    nblk = seq_len // c_len
    ii = lax.broadcasted_iota(jnp.int32, (c_len, c_len), 0)
    jj = lax.broadcasted_iota(jnp.int32, (c_len, c_len), 1)
    diff = (ii - jj).astype(F32)
    ic = lax.broadcasted_iota(jnp.int32, (c_len, 1), 0).astype(F32)
    for hh in range(lg_ref.shape[0]):
        _ret_head(hh, q_ref, k_ref, v_ref, gate_ref, lg_ref,
                  (cos_ref, sin_ref) if rope else None, s0_ref if has_init else None,
                  o_ref, sfin_ref if want_state else None, sf_scr, diff, ic, seq_len)


def _ret_head(hh, q_ref, k_ref, v_ref, gate_ref, lg_ref, rope_refs, s0_ref, o_ref, sfin_ref, sf_scr,
              diff, ic, seq_len):
    rope, has_init, want_state = rope_refs is not None, s0_ref is not None, sfin_ref is not None
    c_len = RET_BLOCK
    nblk = seq_len // c_len
    qk_cols = slice(hh * RET_DK, (hh + 1) * RET_DK)
    v_cols = slice(hh * RET_DV, (hh + 1) * RET_DV)
    lgf = lg_ref[hh, 0:1, 0:1]
    lgb = lg_ref[hh, 1:2, 0:1]
    decay = jnp.exp(jnp.abs(diff) * jnp.where(diff >= 0, lgf, lgb))
    q_dec_f = jnp.exp((ic + 1.0) * lgf)
    q_dec_b = jnp.exp((c_len - ic) * lgb)
    k_dec_f = jnp.exp((c_len - 1.0 - ic) * lgf)
    k_dec_b = jnp.exp(ic * lgb)
    c_dec_f = jnp.exp(c_len * lgf)
    c_dec_b = jnp.exp(c_len * lgb)

    q = q_ref[:, qk_cols].astype(F32)
    k = k_ref[:, qk_cols].astype(F32)
    if rope:
        cos_ref, sin_ref = rope_refs
        lane = lax.broadcasted_iota(jnp.int32, (seq_len, RET_DK), 1)
        first = (lane % (RET_DK // 2)) < (RET_DK // 4)
        cos = cos_ref[...]
        sin = sin_ref[...]

        def rot(x):
            swapped = jnp.where(first, pltpu.roll(x, RET_DK - RET_DK // 4, 1), pltpu.roll(x, RET_DK // 4, 1))
            return x * cos + swapped * sin

        q = rot(q)
        k = rot(k)

    def blk(a, c):
        return a[c * c_len:(c + 1) * c_len]

    def kv_state(kd, c):
        kt = jnp.transpose(blk(k, c) * kd).astype(BF16)
        return jnp.dot(kt, v_ref[c * c_len:(c + 1) * c_len, v_cols], preferred_element_type=F32)

    use_state = has_init or nblk > 1
    s_f = s0_ref[0, hh] if has_init else jnp.zeros((RET_DK, RET_DV), F32)
    for c in range(nblk):
        if use_state:
            sf_scr[c] = s_f
        if want_state or c < nblk - 1:
            s_f = c_dec_f * s_f + kv_state(k_dec_f, c)
    if want_state:
        sfin_ref[0, hh] = s_f

    s_b = s0_ref[1, hh] if has_init else jnp.zeros((RET_DK, RET_DV), F32)
    for c in range(nblk - 1, -1, -1):
        qc = blk(q, c)
        kc = blk(k, c)
        vc = v_ref[c * c_len:(c + 1) * c_len, v_cols]
        s = lax.dot_general(qc.astype(BF16), kc.astype(BF16), (((1,), (1,)), ((), ())),
                            preferred_element_type=F32)
        o = jnp.dot((s * decay).astype(BF16), vc, preferred_element_type=F32)
        if use_state:
            o = o + jnp.dot((qc * q_dec_f).astype(BF16), sf_scr[c].astype(BF16), preferred_element_type=F32)
            o = o + jnp.dot((qc * q_dec_b).astype(BF16), s_b.astype(BF16), preferred_element_type=F32)
        o = o * lax.rsqrt(jnp.mean(o * o, axis=-1, keepdims=True) + EPS)
        gate = gate_ref[c * c_len:(c + 1) * c_len, v_cols].astype(F32)
        o_ref[c * c_len:(c + 1) * c_len, v_cols] = (o * _silu(gate)).astype(o_ref.dtype)
        if want_state or c > 0:
            s_b = c_dec_b * s_b + kv_state(k_dec_b, c)
    if want_state:
        sfin_ref[1, hh] = s_b


def retention_core(proj3, lg, rope_tabs, s0, want_state, hp):
    nseq, seq_len, _ = proj3.shape
    nblk = seq_len // RET_BLOCK
    rope = rope_tabs is not None
    has_init = s0 is not None
    qw, vw = hp * RET_DK, hp * RET_DV
    in_specs = [pl.BlockSpec((None, seq_len, qw), lambda b, h: (b, 0, h)),
                pl.BlockSpec((None, seq_len, qw), lambda b, h: (b, 0, RET_QK // qw + h)),
                pl.BlockSpec((None, seq_len, vw), lambda b, h: (b, 0, 2 * RET_QK // vw + h)),
                pl.BlockSpec((None, seq_len, vw), lambda b, h: (b, 0, (2 * RET_QK + RET_V) // vw + h)),
                pl.BlockSpec((hp, 2, 128), lambda b, h: (h, 0, 0))]
    args = [proj3, proj3, proj3, proj3, lg]
    if rope:
        tab = pl.BlockSpec((seq_len, RET_DK), lambda b, h: (0, 0))
        in_specs += [tab, tab]
        args += list(rope_tabs)
    sspec = pl.BlockSpec((None, 2, hp, RET_DK, RET_DV), lambda b, h: (b, 0, h, 0, 0))
    if has_init:
        in_specs.append(sspec)
        args.append(s0)
    out_specs = [pl.BlockSpec((None, seq_len, vw), lambda b, h: (b, 0, h))]
    out_shape = [jax.ShapeDtypeStruct((nseq, seq_len, RET_V), BF16)]
    if want_state:
        out_specs.append(sspec)
        out_shape.append(jax.ShapeDtypeStruct((nseq, 2, RET_HEADS, RET_DK, RET_DV), F32))
    body = functools.partial(_ret_body, seq_len=seq_len, rope=rope, has_init=has_init, want_state=want_state)
    res = pl.pallas_call(
        body,
        grid=(nseq, RET_HEADS // hp),
        in_specs=in_specs,
        out_specs=out_specs,
        out_shape=out_shape,
        scratch_shapes=[pltpu.VMEM((nblk, RET_DK, RET_DV), F32)],
        compiler_params=_params("parallel", "parallel", vmem=VMEM_LIMIT_BYTES),
        name="retention",
    )(*args)
    return res[0], (res[1] if want_state else None)


def rope_tables(seq_len):
    t = jnp.arange(seq_len)
    half = RET_DK // 2
    nfreq = half // 2
    inv = ROPE_BASE ** (-jnp.arange(nfreq, dtype=F32) / nfreq)
    parts_c, parts_s = [], []
    for p in (t // GRID_W, t % GRID_W):
        ang = p.astype(F32)[:, None] * inv[None, :]
        parts_c += [jnp.cos(ang), jnp.cos(ang)]
        parts_s += [-jnp.sin(ang), jnp.sin(ang)]
    return jnp.concatenate(parts_c, axis=-1), jnp.concatenate(parts_s, axis=-1)


def retention_layer(x, g, mod, rows_per_mod, nseq, seq_len, s0, grid_pos, w_in, lg, w_out, tm):
    proj = in_proj(x, g, mod, w_in, rows_per_mod, tm, PROJ_COLS)
    proj3 = proj.reshape(nseq, seq_len, 2 * RET_QK + 2 * RET_V)
    tabs = rope_tables(seq_len) if grid_pos else None
    hp = RET_HEADS if seq_len <= RET_BLOCK else 1
    y, states = retention_core(proj3, lg, tabs, s0, want_state=s0 is None, hp=hp)
    x_new = out_proj(y.reshape(nseq * seq_len, RET_V), w_out, x, mod, rows_per_mod, min(tm, OUT_ROWS))
    return x_new, states


def _filter_dft_body(a_ref, b_ref, norm_ref, o_ref, *, seq_len, tm):
    row = pl.program_id(0) * tm + lax.broadcasted_iota(jnp.int32, (tm, 1), 0)
    wf = jnp.where(row % (seq_len // 2) == 0, 1.0, 2.0) * (1.0 / (2.0 * seq_len))
    acc = jnp.dot(a_ref[...], b_ref[...], preferred_element_type=F32)
    o_ref[...] = acc * wf / (norm_ref[...] + EPS)


def filter_dft(a, b, norm, tm, tn):
    half, kd = a.shape
    n = b.shape[2]
    nh = half // tm
    body = functools.partial(_filter_dft_body, seq_len=kd, tm=tm)
    return pl.pallas_call(
        body,
        grid=(2 * nh, n // tn),
        in_specs=[pl.BlockSpec((tm, kd), lambda i, j: (i % nh, 0)),
                  pl.BlockSpec((None, kd, tn), lambda i, j: (i // nh, 0, j)),
                  pl.BlockSpec((1, tn), lambda i, j: (0, j))],
        out_specs=pl.BlockSpec((tm, tn), lambda i, j: (i, j)),
        out_shape=jax.ShapeDtypeStruct((kd, n), F32),
        compiler_params=_params("parallel", "parallel", vmem=VMEM_LIMIT_BYTES),
        name="filter_dft",
    )(a, b, norm)


def _filter_gen_body(feat_ref, w1_ref, b1_ref, w2_ref, b2_ref, w3f_ref, w3b_ref, dl_ref,
                     fs_ref, fd_ref, norm_ref, mre_ref, mim_ref, *, seq_len, tr):
    r = pl.program_id(1)
    hp = lax.Precision.HIGHEST
    z = jnp.sin(jnp.dot(feat_ref[...], w1_ref[...], precision=hp, preferred_element_type=F32) + b1_ref[...])
    z = jnp.sin(jnp.dot(z, w2_ref[...], precision=hp, preferred_element_type=F32) + b2_ref[...]).astype(BF16)
    lag = r * tr + lax.broadcasted_iota(jnp.int32, (tr, 1), 0)
    t = lag.astype(F32) * (1.0 / (seq_len - 1))
    win = jnp.exp(-t * jnp.abs(dl_ref[...]))
    ff = jnp.dot(z, w3f_ref[...], preferred_element_type=F32) * win
    fb = jnp.dot(z, w3b_ref[...], preferred_element_type=F32) * win
    part = jnp.sum(jnp.abs(ff) + jnp.abs(fb), axis=0, keepdims=True)
    fb = jnp.where(lag == 0, 0.0, fb)
    fs = ff + fb
    fd = ff - fb
    quarter = jnp.where(lag % 4 < 2, 1.0, -1.0)
    cos_q = jnp.where(lag % 2 == 0, quarter, 0.0)
    sin_q = jnp.where(lag % 2 == 1, quarter, 0.0)

    @pl.when(r == 0)
    def _():
        norm_ref[...] = jnp.zeros_like(norm_ref)
        mre_ref[...] = jnp.zeros_like(mre_ref)
        mim_ref[...] = jnp.zeros_like(mim_ref)

    norm_ref[...] += part
    mre_ref[...] += jnp.sum(cos_q * fs, axis=0, keepdims=True)
    mim_ref[...] -= jnp.sum(sin_q * fd, axis=0, keepdims=True)
    alt = jnp.where(lag % 2 == 0, 1.0, -1.0)
    fs_ref[0] = fs.astype(fs_ref.dtype)
    fs_ref[1] = (fs * alt).astype(fs_ref.dtype)
    fd_ref[0] = fd.astype(fd_ref.dtype)
    fd_ref[1] = (-(fd * alt)).astype(fd_ref.dtype)


def filter_gen(feat, w1, b1, w2, b2, w3, deltas, tr, tn):
    seq_len, kf = feat.shape
    hid = w1.shape[1]
    ncol = HY_ORDER * HY_WIDTH
    nj = ncol // tn
    body = functools.partial(_filter_gen_body, seq_len=seq_len, tr=tr)
    full = lambda shape: pl.BlockSpec(shape, lambda j, r: (0, 0))
    row = pl.BlockSpec((1, tn), lambda j, r: (0, j))
    return pl.pallas_call(
        body,
        grid=(nj, seq_len // tr),
        in_specs=[pl.BlockSpec((tr, kf), lambda j, r: (r, 0)),
                  full((kf, hid)), full((1, hid)), full((hid, hid)), full((1, hid)),
                  pl.BlockSpec((hid, tn), lambda j, r: (0, j)),
                  pl.BlockSpec((hid, tn), lambda j, r: (0, nj + j)),
                  pl.BlockSpec((1, tn), lambda j, r: (0, j % (HY_WIDTH // tn)))],
        out_specs=[pl.BlockSpec((2, tr, tn), lambda j, r: (0, r, j)),
                   pl.BlockSpec((2, tr, tn), lambda j, r: (0, r, j)), row, row, row],
        out_shape=[jax.ShapeDtypeStruct((2, seq_len, ncol), BF16), jax.ShapeDtypeStruct((2, seq_len, ncol), BF16)]
        + [jax.ShapeDtypeStruct((1, ncol), F32)] * 3,
        compiler_params=_params("parallel", "arbitrary", vmem=VMEM_LIMIT_BYTES),
        name="filter_gen",
    )(feat, w1, b1, w2, b2, w3, w3, deltas)


def _cos_sin(freq, time, seq_len):
    prod = (freq[:, None] * time[None, :]) % (2 * seq_len)
    ang = prod.astype(F32) * (math.pi / seq_len)
    return jnp.cos(ang).astype(BF16), (-jnp.sin(ang)).astype(BF16)


def dft_matrices(seq_len):
    half = seq_len // 2
    lo = jnp.arange(half, dtype=jnp.int32)
    filt = _cos_sin(lo, jnp.arange(seq_len, dtype=jnp.int32), seq_len)
    c_e, s_e = _cos_sin(lo, 2 * lo, seq_len)
    c_o, s_o = _cos_sin(lo, 2 * lo + 1, seq_len)
    c_ot, s_ot = _cos_sin(2 * lo + 1, lo, seq_len)
    return filt, (c_e, c_o, s_e, s_o, c_ot, s_ot)


def hyena_filter_spectrum(seq_len, cm, sm, w1, b1, w2, b2, w3):
    t = jnp.linspace(0.0, 1.0, seq_len, dtype=F32)[:, None]
    w = 2.0 * math.pi * jnp.arange(seq_len, dtype=F32)[:, None] / seq_len
    f = jnp.linspace(1e-4, HY_BANDS - 1.0, HY_BANDS, dtype=F32)[None, :]
    feat = jnp.concatenate([t, jnp.cos(f * w), -jnp.sin(f * w)], axis=-1)
    emb, hid = w1.shape
    pe, ph = HY_LANES - emb, HY_LANES - hid
    feat = jnp.pad(feat, ((0, 0), (0, pe)))
    w1p = jnp.pad(w1, ((0, pe), (0, ph)))
    w2p = jnp.pad(w2, ((0, ph), (0, ph)))
    w3p = jnp.pad(w3, ((0, ph), (0, 0))).astype(BF16)
    b1p = jnp.pad(b1, (0, ph)).reshape(1, HY_LANES)
    b2p = jnp.pad(b2, (0, ph)).reshape(1, HY_LANES)
    max_decay = math.log(HY_DECAY_TARGET) / HY_SHORT_DECAY_PCT
    min_decay = math.log(HY_DECAY_TARGET) / HY_LONG_DECAY_PCT
    deltas = jnp.linspace(min_decay, max_decay, HY_WIDTH, dtype=F32).reshape(1, HY_WIDTH)
    tile = min(seq_len, 512)
    fs, fd, norm, mid_re, mid_im = filter_gen(feat, w1p, b1p, w2p, b2p, w3p, deltas, tile, 1024)
    kr = filter_dft(cm, fs, norm, min(seq_len // 2, tile), 512)
    ki = filter_dft(sm, fd, norm, min(seq_len // 2, tile), 512)
    kmid = jnp.stack([mid_re, mid_im], axis=0) / (norm + EPS) / seq_len
    kmid = jnp.transpose(kmid.reshape(2, HY_ORDER, HY_WIDTH), (1, 0, 2)).reshape(2 * HY_ORDER, HY_WIDTH)
    return kr, ki, kmid


def _hyena_body(v_ref, x1_ref, x2_ref, gate_ref, cw_ref, cb_ref, kr0_ref, ki0_ref, kr1_ref, ki1_ref,
                km_ref, sk_ref, ce_ref, co_ref, se_ref, so_ref, cot_ref, sot_ref, o_ref,
                z_scr, xg_scr, p_scr, *, seq_len):
    half = seq_len // 2
    cb = o_ref.shape[-1]
    row = lax.broadcasted_iota(jnp.int32, (half, 1), 0)
    alt = jnp.where(row % 2 == 0, 1.0, -1.0).astype(F32)

    def halves(ref):
        return (ref[:, 0].astype(F32).reshape(half, cb), ref[:, 1].astype(F32).reshape(half, cb))

    def short_conv(x_ref, which, dst):
        xe, xo = halves(x_ref)
        w0, w1, w2 = (cw_ref[j, which:which + 1, :] for j in range(HY_SHORT))
        bias = cb_ref[which:which + 1, :]
        xo_prev = jnp.where(row == 0, 0.0, pltpu.roll(xo, 1, 0))
        xe_next = jnp.where(row == half - 1, 0.0, pltpu.roll(xe, half - 1, 0))
        dst[0] = bias + xo_prev * w0 + xe * w1 + xo * w2
        dst[1] = bias + xe * w0 + xo * w1 + xe_next * w2

    rb = min(half, HY_ROW_BLOCK)
    short_conv(v_ref, 0, z_scr)
    for o, (xg_ref, kr_ref, ki_ref) in enumerate(((x1_ref, kr0_ref, ki0_ref), (x2_ref, kr1_ref, ki1_ref))):
        ze, zo = z_scr[0], z_scr[1]
        zeb, zob = ze.astype(BF16), zo.astype(BF16)
        zm_re = jnp.sum(ze * alt, axis=0, keepdims=True)
        zm_im = -jnp.sum(zo * alt, axis=0, keepdims=True)
        km_re, km_im = km_ref[2 * o:2 * o + 1, :], km_ref[2 * o + 1:2 * o + 2, :]
        pm_re = zm_re * km_re - zm_im * km_im
        pm_im = zm_re * km_im + zm_im * km_re
        for r in range(0, half, rb):
            a_e = jnp.dot(ce_ref[r:r + rb, :], zeb, preferred_element_type=F32)
            a_o = jnp.dot(co_ref[r:r + rb, :], zob, preferred_element_type=F32)
            b_e = jnp.dot(se_ref[r:r + rb, :], zeb, preferred_element_type=F32)
            b_o = jnp.dot(so_ref[r:r + rb, :], zob, preferred_element_type=F32)
            lo_re, lo_im, hi_re, hi_im = a_e + a_o, b_e + b_o, a_e - a_o, b_o - b_e
            kl_re, kl_im = kr_ref[r:r + rb, :], ki_ref[r:r + rb, :]
            kh_re, kh_im = kr_ref[half + r:half + r + rb, :], ki_ref[half + r:half + r + rb, :]
            pl_re, pl_im = lo_re * kl_re - lo_im * kl_im, lo_re * kl_im + lo_im * kl_re
            ph_re, ph_im = hi_re * kh_re - hi_im * kh_im, hi_re * kh_im + hi_im * kh_re
            p_scr[0, r:r + rb, :] = (pl_re + ph_re).astype(BF16)
            p_scr[1, r:r + rb, :] = (pl_im - ph_im).astype(BF16)
            p_scr[2, r:r + rb, :] = (pl_re - ph_re).astype(BF16)
            p_scr[3, r:r + rb, :] = (pl_im + ph_im).astype(BF16)
        short_conv(xg_ref, o + 1, xg_scr)
        sk = sk_ref[o:o + 1, :]
        for r in range(0, half, rb):
            y_e = (jnp.dot(ce_ref[r:r + rb, :], p_scr[0], preferred_element_type=F32)
                   + jnp.dot(se_ref[r:r + rb, :], p_scr[1], preferred_element_type=F32)
                   + alt[r:r + rb] * pm_re)
            y_o = (jnp.dot(cot_ref[r:r + rb, :], p_scr[2], preferred_element_type=F32)
                   + jnp.dot(sot_ref[r:r + rb, :], p_scr[3], preferred_element_type=F32)
                   - alt[r:r + rb] * pm_im)
            z_scr[0, r:r + rb, :] = xg_scr[0, r:r + rb, :] * (y_e + z_scr[0, r:r + rb, :] * sk)
            z_scr[1, r:r + rb, :] = xg_scr[1, r:r + rb, :] * (y_o + z_scr[1, r:r + rb, :] * sk)
    ge, go = halves(gate_ref)
    tiles = o_ref.shape[0]
    o_ref[:, 0] = (z_scr[0] * _silu(ge)).astype(o_ref.dtype).reshape(tiles, HY_PARITY_ROWS // 2, cb)
    o_ref[:, 1] = (z_scr[1] * _silu(go)).astype(o_ref.dtype).reshape(tiles, HY_PARITY_ROWS // 2, cb)


def hyena_core(proj5, conv_w, conv_b, kr, ki, kmid, skip, mats, cb):
    nseq, tiles, _, prow, _ = proj5.shape
    seq_len = tiles * 2 * prow
    half = seq_len // 2
    nj = HY_WIDTH // cb
    once = pl.Buffered(1)

    def act(k):
        return pl.BlockSpec((None, tiles, 2, prow, cb), lambda j, b: (b, 0, 0, 0, k * nj + j))

    def spec(k):
        return pl.BlockSpec((seq_len, cb), lambda j, b: (0, k * nj + j), pipeline_mode=once)

    dft = pl.BlockSpec((half, half), lambda j, b: (0, 0), pipeline_mode=once)
    body = functools.partial(_hyena_body, seq_len=seq_len)
    return pl.pallas_call(
        body,
        grid=(nj, nseq),
        in_specs=[act(0), act(1), act(2), act(3),
                  pl.BlockSpec((HY_SHORT, 3, cb), lambda j, b: (0, 0, j)),
                  pl.BlockSpec((3, cb), lambda j, b: (0, j)),
                  spec(0), spec(0), spec(1), spec(1),
                  pl.BlockSpec((2 * HY_ORDER, cb), lambda j, b: (0, j)),
                  pl.BlockSpec((HY_ORDER, cb), lambda j, b: (0, j))] + [dft] * 6,
        out_specs=pl.BlockSpec((None, tiles, 2, prow, cb), lambda j, b: (b, 0, 0, 0, j)),
        out_shape=jax.ShapeDtypeStruct((nseq, tiles, 2, prow, HY_WIDTH), BF16),
        scratch_shapes=[pltpu.VMEM((2, half, cb), F32), pltpu.VMEM((2, half, cb), F32),
                        pltpu.VMEM((4, half, cb), BF16)],
        compiler_params=_params("parallel", "arbitrary", vmem=VMEM_LIMIT_BYTES),
        name="hyena",
    )(proj5, proj5, proj5, proj5,
      conv_w.reshape(HY_SHORT, 3, HY_WIDTH), conv_b.reshape(3, HY_WIDTH),
      kr, ki, kr, ki, kmid, skip, *mats)


def parity_permutation():
    i = jnp.arange(HY_PARITY_ROWS)
    h = HY_PARITY_ROWS // 2
    src = jnp.where(i < h, 2 * i, 2 * (i - h) + 1)
    p = (src[:, None] == jnp.arange(HY_PARITY_ROWS)[None, :]).astype(BF16)
    return p, jnp.transpose(p)


def _inproj_parity_body(x_ref, g_ref, mod_ref, p_ref, w_ref, o_ref, h_scr):
    @pl.when(pl.program_id(1) == 0)
    def _():
        x = x_ref[...]
        y = x * lax.rsqrt(jnp.mean(x * x, axis=-1, keepdims=True) + EPS) * g_ref[...]
        h = (y * (1.0 + mod_ref[1:2, :]) + mod_ref[0:1, :]).astype(BF16)
        for r in range(0, x.shape[0], HY_PARITY_ROWS):
            h_scr[r:r + HY_PARITY_ROWS, :] = jnp.dot(p_ref[...], h[r:r + HY_PARITY_ROWS],
                                                     preferred_element_type=F32).astype(BF16)

    out = jnp.dot(h_scr[...], w_ref[...], preferred_element_type=F32).astype(o_ref.dtype)
    o_ref[...] = out.reshape(o_ref.shape)


def in_proj_parity(x, g, mod, p, w, rows_per_mod, tm, tn):
    n, nout = x.shape[0], w.shape[1]
    pr = HY_PARITY_ROWS
    return pl.pallas_call(
        _inproj_parity_body,
        grid=(n // tm, nout // tn),
        in_specs=[pl.BlockSpec((tm, D_MODEL), lambda i, j: (i, 0)),
                  pl.BlockSpec((1, D_MODEL), lambda i, j: (0, 0)),
                  pl.BlockSpec((None, 3, D_MODEL), lambda i, j: ((i * tm) // rows_per_mod, 0, 0)),
                  pl.BlockSpec((pr, pr), lambda i, j: (0, 0)),
                  pl.BlockSpec((D_MODEL, tn), lambda i, j: (0, j))],
        out_specs=pl.BlockSpec((tm // pr, 2, pr // 2, tn), lambda i, j: (i, 0, 0, j)),
        out_shape=jax.ShapeDtypeStruct((n // pr, 2, pr // 2, nout), BF16),
        scratch_shapes=[pltpu.VMEM((tm, D_MODEL), BF16)],
        compiler_params=_params("parallel", "arbitrary", vmem=VMEM_LIMIT_BYTES),
        name="in_proj_parity",
    )(x, g.reshape(1, D_MODEL), mod, p, w)


def _outproj_parity_body(y_ref, pt_ref, w_ref, x_ref, mod_ref, o_ref):
    tm = x_ref.shape[0]
    y = y_ref[...].reshape(tm, y_ref.shape[-1])
    parts = [jnp.dot(pt_ref[...], y[r:r + HY_PARITY_ROWS], preferred_element_type=F32).astype(BF16)
             for r in range(0, tm, HY_PARITY_ROWS)]
    out = jnp.dot(jnp.concatenate(parts, axis=0), w_ref[...], preferred_element_type=F32)
    o_ref[...] = x_ref[...] + mod_ref[2:3, :] * out


def out_proj_parity(y, pt, w, x, mod, rows_per_mod, tm):
    n = x.shape[0]
    width = y.shape[-1]
    pr = HY_PARITY_ROWS
    return pl.pallas_call(
        _outproj_parity_body,
        grid=(n // tm,),
        in_specs=[pl.BlockSpec((tm // pr, 2, pr // 2, width), lambda i: (i, 0, 0, 0)),
                  pl.BlockSpec((pr, pr), lambda i: (0, 0)),
                  pl.BlockSpec((width, D_MODEL), lambda i: (0, 0)),
                  pl.BlockSpec((tm, D_MODEL), lambda i: (i, 0)),
                  pl.BlockSpec((None, 3, D_MODEL), lambda i: ((i * tm) // rows_per_mod, 0, 0))],
        out_specs=pl.BlockSpec((tm, D_MODEL), lambda i: (i, 0)),
        out_shape=jax.ShapeDtypeStruct((n, D_MODEL), F32),
        compiler_params=_params("parallel", vmem=VMEM_LIMIT_BYTES),
        name="out_proj_parity",
    )(y, pt, w, x, mod)


def hyena_layer(x, g, mod, rows_per_mod, nseq, seq_len, w_in, conv_w, conv_b, filt_w, skip, w_out, perms, tm, cb):
    par_p, par_pt = perms
    pr = HY_PARITY_ROWS
    proj = in_proj_parity(x, g, mod, par_p, w_in, rows_per_mod, tm, PROJ_COLS)
    proj5 = proj.reshape(nseq, seq_len // pr, 2, pr // 2, 4 * HY_WIDTH)
    (cfil, sfil), mats = dft_matrices(seq_len)
    kr, ki, kmid = hyena_filter_spectrum(seq_len, cfil, sfil, *filt_w)
    y = hyena_core(proj5, conv_w, conv_b, kr, ki, kmid, skip, mats, cb)
    y = y.reshape(nseq * seq_len // pr, 2, pr // 2, HY_WIDTH)
    return out_proj_parity(y, par_pt, w_out, x, mod, rows_per_mod, min(tm, OUT_ROWS))


def kernel(x_prompt, x_sample, c, state_s5, state_ret, c_ctx, norm_g, mod_w, mod_b, s5_w_in, s5_lam_re, s5_lam_im, s5_log_step, s5_b_re, s5_b_im, s5_c_re, s5_c_im, s5_d, s5_w_glu, s5_b_glu, s5_w_out, ret_w_in, ret_decay_logit, ret_w_out, hy_w_in, hy_conv_w, hy_conv_b, hy_f_w1, hy_f_b1, hy_f_w2, hy_f_b2, hy_f_w3, hy_skip, hy_w_out, final_g):
    n_ctx, l_ctx, _ = x_prompt.shape
    n_dec, l_dec, _ = x_sample.shape
    xc = x_prompt.reshape(n_ctx * l_ctx, D_MODEL)
    xl = x_sample.reshape(n_dec * l_dec, D_MODEL)

    pad = (-(1 + n_dec)) % 8
    cvecs = jnp.concatenate([c_ctx[None, :], c, jnp.zeros((pad, D_MODEL), F32)], axis=0)
    mods = ada_mod_all(cvecs, mod_w, mod_b).reshape(DEPTH, -1, 3, D_MODEL)

    tm = 1024
    perm = s5_lane_permutation()
    last_is_s5 = (DEPTH - 1) % N_MIXERS == 0
    s5_new, ret_new = [], []
    for i in range(DEPTH):
        kind, j = i % N_MIXERS, i // N_MIXERS
        mod_c = mods[i, 0:1]
        mod_l = mods[i, 1:1 + n_dec]
        g = norm_g[i]
        if kind == 0:
            tables = s5_tables(s5_lam_re[j], s5_lam_im[j], s5_log_step[j], s5_b_re[j], s5_b_im[j],
                               s5_c_re[j], s5_c_im[j], s5_d[j])
            w = (s5_w_in[j].astype(BF16), tables, perm, s5_w_glu[j].astype(BF16), s5_b_glu[j],
                 s5_w_out[j].astype(BF16))
            fg = final_g if last_is_s5 and i == DEPTH - 1 else None
            xc, st = s5_layer(xc, g, mod_c, n_ctx, l_ctx, None, *w, nb=n_ctx, final_g=fg)
            xl, _ = s5_layer(xl, g, mod_l, n_dec, l_dec, state_s5[:, j], *w, nb=S5_SEQS_PER_STEP, final_g=fg)
            s5_new.append(st)
        elif kind == 1:
            w_in = ret_w_in[j]
            kscale = jnp.concatenate([jnp.ones((RET_QK,), F32), jnp.full((RET_QK,), RET_DK ** -0.5, F32),
                                      jnp.ones((2 * RET_V,), F32)])
            w_in = (w_in * kscale[None, :]).astype(BF16)
            lg = jax.nn.log_sigmoid(ret_decay_logit[j])
            lg = jnp.broadcast_to(jnp.transpose(lg)[:, :, None], (RET_HEADS, 2, 128))
            w = (w_in, lg, ret_w_out[j].astype(BF16), tm)
            xc, st = retention_layer(xc, g, mod_c, n_ctx * l_ctx, n_ctx, l_ctx, None, False, *w)
            xl, _ = retention_layer(xl, g, mod_l, l_dec, n_dec, l_dec, state_ret[:, j], True, *w)
            ret_new.append(st)
        else:
            filt_w = (hy_f_w1[j], hy_f_b1[j], hy_f_w2[j], hy_f_b2[j], hy_f_w3[j])
            w = (hy_w_in[j].astype(BF16), hy_conv_w[j], hy_conv_b[j], filt_w, hy_skip[j], hy_w_out[j].astype(BF16),
                 parity_permutation(), tm)
            xc = hyena_layer(xc, g, mod_c, n_ctx * l_ctx, n_ctx, l_ctx, *w, cb=1024)
            xl = hyena_layer(xl, g, mod_l, l_dec, n_dec, l_dec, *w, cb=256)

    if not last_is_s5:
        xc, xl = final_norm(xc, final_g, tm), final_norm(xl, final_g, tm)
    y_prompt = xc.reshape(n_ctx, l_ctx, D_MODEL)
    y_sample = xl.reshape(n_dec, l_dec, D_MODEL)
    new_state_s5 = jnp.stack(s5_new, axis=1)
    new_state_ret = jnp.stack(ret_new, axis=1)
    return (y_prompt, y_sample, new_state_s5, new_state_ret)
```

```python
import functools
import math

import jax
import jax.numpy as jnp
from jax import lax
from jax.experimental import pallas as pl
from jax.experimental.pallas import tpu as pltpu

F32 = jnp.float32
BF16 = jnp.bfloat16

D_MODEL = 1024
DEPTH = 4
N_MIXERS = 3
EPS = 1e-6
GRID_W = 64

S5_GROUP = 16
S5_GROUPS = D_MODEL // S5_GROUP
S5_STATE = 64
S5_CHUNK = 16
S5_TILE = S5_CHUNK * S5_GROUP
S5_SLAB = 128 // S5_GROUP
S5_SEQS_PER_STEP = 8
S5_ROWS = 512

RET_HEADS = 8
RET_QK = D_MODEL
RET_V = 2 * D_MODEL
RET_DK = RET_QK // RET_HEADS
RET_DV = RET_V // RET_HEADS
RET_BLOCK = 256
ROPE_BASE = 10000.0

HY_WIDTH = 2 * D_MODEL
HY_ORDER = 2
HY_SHORT = 3
HY_BANDS = 16
HY_SHORT_DECAY_PCT = 0.3
HY_LONG_DECAY_PCT = 1.5
HY_DECAY_TARGET = 1e-2
HY_LANES = 128
HY_PARITY_ROWS = 256
HY_ROW_BLOCK = 512

VMEM_LIMIT_BYTES = 56 * 1024 * 1024
PROJ_COLS = 2048
OUT_ROWS = 512


def _params(*sem, vmem=None):
    return pltpu.CompilerParams(dimension_semantics=sem, vmem_limit_bytes=vmem)


def _silu(x):
    return x * jax.nn.sigmoid(x)


def _mod_body(c_ref, w_ref, b_ref, o_ref):
    a = _silu(c_ref[...]).astype(BF16)
    o_ref[0] = jnp.dot(a, w_ref[0].astype(BF16), preferred_element_type=F32) + b_ref[0]


def ada_mod_all(cvecs, mod_w, mod_b):
    r = cvecs.shape[0]
    tn = D_MODEL
    return pl.pallas_call(
        _mod_body,
        grid=(DEPTH, 3 * D_MODEL // tn),
        in_specs=[pl.BlockSpec((r, D_MODEL), lambda i, j: (0, 0)),
                  pl.BlockSpec((1, D_MODEL, tn), lambda i, j: (i, 0, j)),
                  pl.BlockSpec((1, 1, tn), lambda i, j: (i, 0, j))],
        out_specs=pl.BlockSpec((1, r, tn), lambda i, j: (i, 0, j)),
        out_shape=jax.ShapeDtypeStruct((DEPTH, r, 3 * D_MODEL), F32),
        compiler_params=_params("parallel", "parallel"),
        name="ada_mod",
    )(cvecs, mod_w, mod_b.reshape(DEPTH, 1, 3 * D_MODEL))


def _inproj_body(x_ref, g_ref, mod_ref, w_ref, o_ref, h_scr):
    @pl.when(pl.program_id(1) == 0)
    def _():
        x = x_ref[...]
        y = x * lax.rsqrt(jnp.mean(x * x, axis=-1, keepdims=True) + EPS) * g_ref[...]
        h_scr[...] = (y * (1.0 + mod_ref[1:2, :]) + mod_ref[0:1, :]).astype(BF16)

    o_ref[...] = jnp.dot(h_scr[...], w_ref[...], preferred_element_type=F32).astype(o_ref.dtype)


def in_proj(x, g, mod, w, rows_per_mod, tm, tn):
    n, nout = x.shape[0], w.shape[1]
    return pl.pallas_call(
        _inproj_body,
        grid=(n // tm, nout // tn),
        in_specs=[pl.BlockSpec((tm, D_MODEL), lambda i, j: (i, 0)),
                  pl.BlockSpec((1, D_MODEL), lambda i, j: (0, 0)),
                  pl.BlockSpec((None, 3, D_MODEL), lambda i, j: ((i * tm) // rows_per_mod, 0, 0)),
                  pl.BlockSpec((D_MODEL, tn), lambda i, j: (0, j))],
        out_specs=pl.BlockSpec((tm, tn), lambda i, j: (i, j)),
        out_shape=jax.ShapeDtypeStruct((n, nout), BF16),
        scratch_shapes=[pltpu.VMEM((tm, D_MODEL), BF16)],
        compiler_params=_params("parallel", "arbitrary", vmem=VMEM_LIMIT_BYTES),
        name="in_proj",
    )(x, g.reshape(1, D_MODEL), mod, w)


def _outproj_body(y_ref, w_ref, x_ref, mod_ref, o_ref):
    out = jnp.dot(y_ref[...], w_ref[...], preferred_element_type=F32)
    o_ref[...] = x_ref[...] + mod_ref[2:3, :] * out


def out_proj(y, w, x, mod, rows_per_mod, tm):
    n, width = y.shape
    return pl.pallas_call(
        _outproj_body,
        grid=(n // tm,),
        in_specs=[pl.BlockSpec((tm, width), lambda i: (i, 0)),
                  pl.BlockSpec((width, D_MODEL), lambda i: (0, 0)),
                  pl.BlockSpec((tm, D_MODEL), lambda i: (i, 0)),
                  pl.BlockSpec((None, 3, D_MODEL), lambda i: ((i * tm) // rows_per_mod, 0, 0))],
        out_specs=pl.BlockSpec((tm, D_MODEL), lambda i: (i, 0)),
        out_shape=jax.ShapeDtypeStruct((n, D_MODEL), F32),
        compiler_params=_params("parallel", vmem=VMEM_LIMIT_BYTES),
        name="out_proj",
    )(y, w, x, mod)


def _final_norm_body(x_ref, g_ref, o_ref):
    x = x_ref[...]
    o_ref[...] = x * lax.rsqrt(jnp.mean(x * x, axis=-1, keepdims=True) + EPS) * g_ref[...]


def final_norm(x, g, tm):
    n = x.shape[0]
    return pl.pallas_call(
        _final_norm_body,
        grid=(n // tm,),
        in_specs=[pl.BlockSpec((tm, D_MODEL), lambda i: (i, 0)),
                  pl.BlockSpec((1, D_MODEL), lambda i: (0, 0))],
        out_specs=pl.BlockSpec((tm, D_MODEL), lambda i: (i, 0)),
        out_shape=jax.ShapeDtypeStruct((n, D_MODEL), F32),
        compiler_params=_params("parallel"),
        name="final_norm",
    )(x, g.reshape(1, D_MODEL))


def s5_tables(lam_re, lam_im, log_step, b_re, b_im, c_re, c_im, d_skip):
    t_len = S5_CHUNK
    dt = jnp.exp(log_step)[..., None]
    ab_re = jnp.exp(lam_re * dt) * jnp.cos(lam_im * dt)
    ab_im = jnp.exp(lam_re * dt) * jnp.sin(lam_im * dt)
    den = lam_re * lam_re + lam_im * lam_im
    nr, ni = ab_re - 1.0, ab_im
    f_re = (nr * lam_re + ni * lam_im) / den
    f_im = (ni * lam_re - nr * lam_im) / den
    bb_re = f_re[..., None] * b_re - f_im[..., None] * b_im
    bb_im = f_re[..., None] * b_im + f_im[..., None] * b_re
    ks = jnp.arange(t_len + 1, dtype=F32)[:, None, None, None]
    pw_mag = jnp.exp(ks * (lam_re * dt)[None])
    pw_re = pw_mag * jnp.cos(ks * (lam_im * dt)[None])
    pw_im = pw_mag * jnp.sin(ks * (lam_im * dt)[None])

    npair = S5_GROUPS // 2
    pair_eye = jnp.eye(2, dtype=F32)

    def pair_pack(a):
        a = a.reshape(2, npair, 2, S5_GROUP, S5_STATE)
        a = a[:, :, :, :, None, :] * pair_eye[None, None, :, None, :, None]
        return jnp.transpose(a.reshape(2, npair, 2, S5_GROUP, 2 * S5_STATE), (1, 0, 2, 3, 4))

    bbt = jnp.stack([pair_pack(jnp.swapaxes(bb_re, 2, 3)), pair_pack(jnp.swapaxes(bb_im, 2, 3))], axis=2)
    cpk = jnp.stack([pair_pack(c_re), pair_pack(c_im)], axis=2)
    pw = jnp.stack([pw_re, pw_im], axis=0).reshape(2, t_len + 1, 2, npair, 2 * S5_STATE)
    pw = jnp.transpose(pw, (3, 2, 0, 1, 4))
    dmat = jnp.eye(S5_GROUP, dtype=F32) * d_skip.reshape(S5_GROUPS, 1, S5_GROUP)
    dmat = jnp.pad(dmat, ((0, 0), (0, 0), (0, S5_TILE - S5_GROUP))).reshape(npair, 2, S5_GROUP, S5_TILE)
    m, f, et = s5_table_kernel(bbt, cpk, pw, dmat)

    dec = jnp.stack([pw_re[t_len, 0], pw_im[t_len, 0], pw_re[t_len, 1], pw_im[t_len, 1]], axis=0)
    dec = jnp.transpose(dec.reshape(4, S5_GROUPS // S5_SLAB, S5_SLAB * S5_STATE), (1, 0, 2))
    return m.reshape(S5_GROUPS, S5_TILE, S5_TILE), f, et, dec


def _s5_table_body(bb_ref, c_ref, pw_ref, d_ref, m_ref, f_ref, et_ref):
    t_len = S5_CHUNK
    nt_dims = (((1,), (1,)), ((), ()))
    hp = lax.Precision.HIGHEST

    def times_power(ref, d, g, k):
        pr, pi = pw_ref[d, 0, k:k + 1, :], pw_ref[d, 1, k:k + 1, :]
        ar, ai = ref[d, 0, g], ref[d, 1, g]
        return ar * pr - ai * pi, ar * pi + ai * pr

    def stack(parts):
        return jnp.concatenate(parts, axis=0)

    for d in range(2):
        f_pow = [t_len - 1 - s for s in range(t_len)] if d == 0 else list(range(t_len))
        e_pow = [s + 1 for s in range(t_len)] if d == 0 else [t_len - s for s in range(t_len)]
        fr, fi, er, ei = [], [], [], []
        for g in range(2):
            for s in range(t_len):
                a, b = times_power(bb_ref, d, g, f_pow[s])
                fr.append(a)
                fi.append(b)
                a, b = times_power(c_ref, d, g, e_pow[s])
                er.append(a)
                ei.append(-b)
        lanes = 2 * S5_STATE
        f_ref[:, (2 * d) * lanes:(2 * d + 1) * lanes] = stack(fr).astype(f_ref.dtype)
        f_ref[:, (2 * d + 1) * lanes:(2 * d + 2) * lanes] = stack(fi).astype(f_ref.dtype)
        et_ref[:, (2 * d) * lanes:(2 * d + 1) * lanes] = stack(er).astype(et_ref.dtype)
        et_ref[:, (2 * d + 1) * lanes:(2 * d + 2) * lanes] = stack(ei).astype(et_ref.dtype)

    zeros = jnp.zeros((S5_GROUP, S5_TILE), F32)
    for g in range(2):
        def lag_kernels(d, powers):
            car, cai = zip(*(times_power(c_ref, d, g, k) for k in powers))
            return (lax.dot_general(bb_ref[d, 0, g], stack(car), nt_dims, precision=hp, preferred_element_type=F32)
                    - lax.dot_general(bb_ref[d, 1, g], stack(cai), nt_dims, precision=hp,
                                      preferred_element_type=F32))
        v_f = jnp.concatenate([zeros, lag_kernels(0, range(t_len)) + d_ref[g]], axis=1)
        v_b = jnp.concatenate([lag_kernels(1, range(t_len - 1, -1, -1)), zeros], axis=1)
        width = 2 * S5_TILE
        for s in range(t_len):
            blk_f = pltpu.roll(v_f, (width - (t_len - s) * S5_GROUP) % width, 1)[:, :S5_TILE]
            blk_b = pltpu.roll(v_b, (width - (t_len - 1 - s) * S5_GROUP) % width, 1)[:, :S5_TILE]
            m_ref[g, s * S5_GROUP:(s + 1) * S5_GROUP, :] = (blk_f + blk_b).astype(m_ref.dtype)


def s5_table_kernel(bbt, cpk, pw, dmat):
    npair = S5_GROUPS // 2
    pspec = pl.BlockSpec((None, 2, 2, 2, S5_GROUP, 2 * S5_STATE), lambda k: (k, 0, 0, 0, 0, 0))
    wide = pl.BlockSpec((None, 2 * S5_TILE, 2 * S5_TILE), lambda k: (k, 0, 0))
    return pl.pallas_call(
        _s5_table_body,
        grid=(npair,),
        in_specs=[pspec, pspec,
                  pl.BlockSpec((None, 2, 2, S5_CHUNK + 1, 2 * S5_STATE), lambda k: (k, 0, 0, 0, 0)),
                  pl.BlockSpec((None, 2, S5_GROUP, S5_TILE), lambda k: (k, 0, 0, 0))],
        out_specs=[pl.BlockSpec((None, 2, S5_TILE, S5_TILE), lambda k: (k, 0, 0, 0)), wide, wide],
        out_shape=[jax.ShapeDtypeStruct((npair, 2, S5_TILE, S5_TILE), BF16),
                   jax.ShapeDtypeStruct((npair, 2 * S5_TILE, 2 * S5_TILE), BF16),
                   jax.ShapeDtypeStruct((npair, 2 * S5_TILE, 2 * S5_TILE), BF16)],
        compiler_params=_params("parallel"),
        name="s5_tables",
    )(bbt, cpk, pw, dmat)


def s5_lane_permutation():
    width = S5_CHUNK * S5_SLAB * S5_GROUP
    i = jnp.arange(width)
    step, grp, ch = i // (S5_SLAB * S5_GROUP), (i // S5_GROUP) % S5_SLAB, i % S5_GROUP
    dst = grp * S5_TILE + step * S5_GROUP + ch
    return (dst[:, None] == jnp.arange(width)[None, :]).astype(BF16)


def _s5_scan_body(u_ref, q_ref, m_ref, f_ref, et_ref, dec_ref, h0_ref, y_ref, hf_ref,
                  uy_s, st_s, *, nchunk, nseq):
    npair = S5_SLAB // 2
    pair_w = 2 * S5_TILE
    lane_w = 2 * S5_STATE
    nt_dims = (((1,), (1,)), ((), ()))
    xcat = jnp.concatenate([u_ref[s] for s in range(S5_CHUNK)], axis=1)
    for k in range(npair):
        cols = slice(k * pair_w, (k + 1) * pair_w)
        up = jnp.dot(xcat, q_ref[:, cols], preferred_element_type=F32).astype(BF16)
        uy_s[:, cols] = up
        x = jnp.dot(up, f_ref[k], preferred_element_type=F32)
        for q in range(4):
            st_s[q * npair + k] = x[:, q * lane_w:(q + 1) * lane_w]

    def lanes_of(ref, q, k):
        return ref[q, :, k * lane_w:(k + 1) * lane_w]

    dec = [[dec_ref[q:q + 1, k * lane_w:(k + 1) * lane_w] for k in range(npair)] for q in range(4)]

    def step(i, carry):
        rows_f = pl.ds(pl.multiple_of(i * nseq, nseq), nseq)
        rows_b = pl.ds(pl.multiple_of((nchunk - 1 - i) * nseq, nseq), nseq)
        new = []
        for k in range(npair):
            fr, fi, br, bi = (carry[q * npair + k] for q in range(4))
            xr, xi = st_s[0 * npair + k, rows_f, :], st_s[1 * npair + k, rows_f, :]
            yr, yi = st_s[2 * npair + k, rows_b, :], st_s[3 * npair + k, rows_b, :]
            st_s[0 * npair + k, rows_f, :] = fr
            st_s[1 * npair + k, rows_f, :] = fi
            st_s[2 * npair + k, rows_b, :] = br
            st_s[3 * npair + k, rows_b, :] = bi
            new.append((dec[0][k] * fr - dec[1][k] * fi + xr, dec[0][k] * fi + dec[1][k] * fr + xi,
                        dec[2][k] * br - dec[3][k] * bi + yr, dec[2][k] * bi + dec[3][k] * br + yi))
        return tuple(new[k][q] for q in range(4) for k in range(npair))

    init = tuple(lanes_of(h0_ref, q, k) for q in range(4) for k in range(npair))
    final = lax.fori_loop(0, nchunk, step, init)
    for q in range(4):
        for k in range(npair):
            hf_ref[q, :, k * lane_w:(k + 1) * lane_w] = final[q * npair + k]

    for k in range(npair):
        h = jnp.concatenate([st_s[q * npair + k] for q in range(4)], axis=1)
        ye = lax.dot_general(h.astype(BF16), et_ref[k], nt_dims, preferred_element_type=F32)
        for g in range(2):
            cols = slice(k * pair_w + g * S5_TILE, k * pair_w + (g + 1) * S5_TILE)
            y = jnp.dot(uy_s[:, cols], m_ref[2 * k + g], preferred_element_type=F32)
            uy_s[:, cols] = (y + ye[:, g * S5_TILE:(g + 1) * S5_TILE]).astype(BF16)
    slab_w = S5_SLAB * S5_GROUP
    steps = pair_w // slab_w
    for t0 in range(0, S5_CHUNK, steps):
        yp = lax.dot_general(uy_s[...], q_ref[t0 * slab_w:(t0 + steps) * slab_w, :], nt_dims,
                             preferred_element_type=F32).astype(y_ref.dtype)
        for t in range(steps):
            y_ref[t0 + t] = yp[:, t * slab_w:(t + 1) * slab_w]


def s5_scan(u_s, perm, m, f, e, dec, h0, nchunk, nseq, nb):
    rows = nchunk * nb
    nslab = S5_GROUPS // S5_SLAB
    slab_w = S5_SLAB * S5_GROUP
    state_w = S5_SLAB * S5_STATE
    width = S5_CHUNK * slab_w
    body = functools.partial(_s5_scan_body, nchunk=nchunk, nseq=nb)
    hspec = pl.BlockSpec((None, None, 4, nb, state_w), lambda s, b: (s, b, 0, 0, 0))
    state = pltpu.VMEM((2 * S5_SLAB, rows, 2 * S5_STATE), F32)
    once = pl.Buffered(1)
    return pl.pallas_call(
        body,
        grid=(nslab, nseq // nb),
        in_specs=[pl.BlockSpec((S5_CHUNK, rows, slab_w), lambda s, b: (0, b, s)),
                  pl.BlockSpec((width, width), lambda s, b: (0, 0), pipeline_mode=once),
                  pl.BlockSpec((S5_SLAB, S5_TILE, S5_TILE), lambda s, b: (s, 0, 0)),
                  pl.BlockSpec((S5_SLAB // 2, 2 * S5_TILE, 2 * S5_TILE), lambda s, b: (s, 0, 0)),
                  pl.BlockSpec((S5_SLAB // 2, 2 * S5_TILE, 2 * S5_TILE), lambda s, b: (s, 0, 0)),
                  pl.BlockSpec((None, 4, state_w), lambda s, b: (s, 0, 0)),
                  hspec],
        out_specs=[pl.BlockSpec((S5_CHUNK, rows, slab_w), lambda s, b: (0, b, s)), hspec],
        out_shape=[jax.ShapeDtypeStruct((S5_CHUNK, nseq * nchunk, D_MODEL), BF16),
                   jax.ShapeDtypeStruct((nslab, nseq // nb, 4, nb, state_w), F32)],
        scratch_shapes=[pltpu.VMEM((rows, width), BF16), state],
        compiler_params=_params("parallel", "parallel", vmem=VMEM_LIMIT_BYTES),
        name="s5_scan",
    )(u_s, perm, m, f, e, dec, h0)


def s5_row_permutation(nb):
    cpt = S5_ROWS // (S5_CHUNK * nb)
    i = jnp.arange(S5_ROWS)
    step, chunk, seq = i // (cpt * nb), (i // nb) % cpt, i % nb
    src = seq * (cpt * S5_CHUNK) + chunk * S5_CHUNK + step
    p = (src[:, None] == jnp.arange(S5_ROWS)[None, :]).astype(BF16)
    return p, jnp.transpose(p)


def _s5_tile_specs(x3, mod, nb):
    nseq, seq_len, _ = x3.shape
    cpt = S5_ROWS // (S5_CHUNK * nb)
    ctiles = seq_len // (cpt * S5_CHUNK)
    grid = (nseq // nb, ctiles)
    xspec = pl.BlockSpec((nb, cpt * S5_CHUNK, D_MODEL), lambda b, c: (b, c, 0))
    if mod.shape[0] == 1:
        mspec = pl.BlockSpec((1, 3, D_MODEL), lambda b, c: (0, 0, 0))
    else:
        mspec = pl.BlockSpec((nb, 3, D_MODEL), lambda b, c: (b, 0, 0))

    def stepped(width, col):
        return pl.BlockSpec((S5_CHUNK, S5_ROWS // S5_CHUNK, width), lambda b, c: (0, b * ctiles + c, col))

    return grid, xspec, mspec, stepped


def _inproj_steps_body(x_ref, g_ref, mod_ref, p_ref, w_ref, o_ref):
    x = x_ref[...]
    y = x * lax.rsqrt(jnp.mean(x * x, axis=-1, keepdims=True) + EPS) * g_ref[...]
    h = (y * (1.0 + mod_ref[:, 1:2, :]) + mod_ref[:, 0:1, :]).astype(BF16).reshape(S5_ROWS, D_MODEL)
    h = jnp.dot(p_ref[...], h, preferred_element_type=F32).astype(BF16)
    out = jnp.dot(h, w_ref[...], preferred_element_type=F32).astype(o_ref.dtype)
    o_ref[...] = out.reshape(o_ref.shape)


def in_proj_steps(x3, g, mod, p, w, nb):
    nseq, seq_len, _ = x3.shape
    nout = w.shape[1]
    grid, xspec, mspec, stepped = _s5_tile_specs(x3, mod, nb)
    return pl.pallas_call(
        _inproj_steps_body,
        grid=grid,
        in_specs=[xspec,
                  pl.BlockSpec((1, D_MODEL), lambda b, c: (0, 0)),
                  mspec,
                  pl.BlockSpec((S5_ROWS, S5_ROWS), lambda b, c: (0, 0)),
                  pl.BlockSpec((D_MODEL, nout), lambda b, c: (0, 0))],
        out_specs=stepped(nout, 0),
        out_shape=jax.ShapeDtypeStruct((S5_CHUNK, nseq * seq_len // S5_CHUNK, nout), BF16),
        compiler_params=_params("parallel", "parallel", vmem=VMEM_LIMIT_BYTES),
        name="in_proj_steps",
    )(x3, g.reshape(1, D_MODEL), mod, p, w)


def _s5_out_body(y_ref, gate_ref, wg_ref, bg_ref, wo_ref, pt_ref, x_ref, mod_ref, *rest):
    o_ref = rest[-1]
    g = jax.nn.gelu(y_ref[...].astype(F32).reshape(S5_ROWS, D_MODEL))
    t = jnp.dot(g.astype(BF16), wg_ref[...], preferred_element_type=F32) + bg_ref[...]
    z = g * jax.nn.sigmoid(t) * _silu(gate_ref[...].astype(F32).reshape(S5_ROWS, D_MODEL))
    z = jnp.dot(pt_ref[...], z.astype(BF16), preferred_element_type=F32).astype(BF16)
    out = jnp.dot(z, wo_ref[...], preferred_element_type=F32)
    x = x_ref[...] + mod_ref[:, 2:3, :] * out.reshape(x_ref.shape)
    if len(rest) == 2:
        x = x * lax.rsqrt(jnp.mean(x * x, axis=-1, keepdims=True) + EPS) * rest[0][...]
    o_ref[...] = x


def s5_out(y_s, u_s, w_glu, b_glu, w_out, pt, x3, mod, nb, final_g=None):
    grid, xspec, mspec, stepped = _s5_tile_specs(x3, mod, nb)
    wspec = pl.BlockSpec((D_MODEL, D_MODEL), lambda b, c: (0, 0))
    vspec = pl.BlockSpec((1, D_MODEL), lambda b, c: (0, 0))
    in_specs = [stepped(D_MODEL, 0), stepped(D_MODEL, 1), wspec, vspec, wspec,
                pl.BlockSpec((S5_ROWS, S5_ROWS), lambda b, c: (0, 0)), xspec, mspec]
    args = [y_s, u_s, w_glu, b_glu.reshape(1, D_MODEL), w_out, pt, x3, mod]
    if final_g is not None:
        in_specs.append(vspec)
        args.append(final_g.reshape(1, D_MODEL))
    return pl.pallas_call(
        _s5_out_body,
        grid=grid,
        in_specs=in_specs,
        out_specs=xspec,
        out_shape=jax.ShapeDtypeStruct(x3.shape, F32),
        compiler_params=_params("parallel", "parallel", vmem=VMEM_LIMIT_BYTES),
        name="s5_out",
    )(*args)


def s5_layer(x, g, mod, nseq, seq_len, h0, w_in, tables, perm, w_glu, b_glu, w_out, nb, final_g=None):
    nchunk = seq_len // S5_CHUNK
    nslab = S5_GROUPS // S5_SLAB
    state_w = S5_SLAB * S5_STATE
    row_p, row_pt = s5_row_permutation(nb)
    x3 = x.reshape(nseq, seq_len, D_MODEL)
    u_s = in_proj_steps(x3, g, mod, row_p, w_in, nb)
    if h0 is None:
        h0_p = jnp.zeros((nslab, nseq // nb, 4, nb, state_w), F32)
    else:
        h0_p = jnp.transpose(h0.reshape(nseq // nb, nb, 4, nslab, state_w), (3, 0, 2, 1, 4))
    m, f, e, dec = tables
    y_s, hf = s5_scan(u_s, perm, m, f, e, dec, h0_p, nchunk, nseq, nb)
    x_new = s5_out(y_s, u_s, w_glu, b_glu, w_out, row_pt, x3, mod, nb, final_g).reshape(nseq * seq_len, D_MODEL)
    states = jnp.transpose(hf, (1, 3, 2, 0, 4)).reshape(nseq, 2, 2, S5_GROUPS, S5_STATE)
    return x_new, states


def _ret_body(*refs, seq_len, rope, has_init, want_state):
    refs = list(refs)
    q_ref, k_ref, v_ref, gate_ref, lg_ref = refs[:5]
    pos = 5
    if rope:
        cos_ref, sin_ref = refs[pos:pos + 2]
        pos += 2
    if has_init:
        s0_ref = refs[pos]
        pos += 1
    o_ref = refs[pos]
    pos += 1
    if want_state:
        sfin_ref = refs[pos]
        pos += 1
    sf_scr = refs[pos]

    c_len = RET_BLOCK
    nblk = seq_len // c_len
    ii = lax.broadcasted_iota(jnp.int32, (c_len, c_len), 0)
    jj = lax.broadcasted_iota(jnp.int32, (c_len, c_len), 1)
    diff = (ii - jj).astype(F32)
    ic = lax.broadcasted_iota(jnp.int32, (c_len, 1), 0).astype(F32)
    for hh in range(lg_ref.shape[0]):
        _ret_head(hh, q_ref, k_ref, v_ref, gate_ref, lg_ref,
                  (cos_ref, sin_ref) if rope else None, s0_ref if has_init else None,
                  o_ref, sfin_ref if want_state else None, sf_scr, diff, ic, seq_len)


def _ret_head(hh, q_ref, k_ref, v_ref, gate_ref, lg_ref, rope_refs, s0_ref, o_ref, sfin_ref, sf_scr,
              diff, ic, seq_len):
    rope, has_init, want_state = rope_refs is not None, s0_ref is not None, sfin_ref is not None
    c_len = RET_BLOCK
    nblk = seq_len // c_len
    qk_cols = slice(hh * RET_DK, (hh + 1) * RET_DK)
    v_cols = slice(hh * RET_DV, (hh + 1) * RET_DV)
    lgf = lg_ref[hh, 0:1, 0:1]
    lgb = lg_ref[hh, 1:2, 0:1]
    decay = jnp.exp(jnp.abs(diff) * jnp.where(diff >= 0, lgf, lgb))
    q_dec_f = jnp.exp((ic + 1.0) * lgf)
    q_dec_b = jnp.exp((c_len - ic) * lgb)
    k_dec_f = jnp.exp((c_len - 1.0 - ic) * lgf)
    k_dec_b = jnp.exp(ic * lgb)
    c_dec_f = jnp.exp(c_len * lgf)
    c_dec_b = jnp.exp(c_len * lgb)

    q = q_ref[:, qk_cols].astype(F32)
    k = k_ref[:, qk_cols].astype(F32)
    if rope:
        cos_ref, sin_ref = rope_refs
        lane = lax.broadcasted_iota(jnp.int32, (seq_len, RET_DK), 1)
        first = (lane % (RET_DK // 2)) < (RET_DK // 4)
        cos = cos_ref[...]
        sin = sin_ref[...]

        def rot(x):
            swapped = jnp.where(first, pltpu.roll(x, RET_DK - RET_DK // 4, 1), pltpu.roll(x, RET_DK // 4, 1))
            return x * cos + swapped * sin

        q = rot(q)
        k = rot(k)

    def blk(a, c):
        return a[c * c_len:(c + 1) * c_len]

    def kv_state(kd, c):
        kt = jnp.transpose(blk(k, c) * kd).astype(BF16)
        return jnp.dot(kt, v_ref[c * c_len:(c + 1) * c_len, v_cols], preferred_element_type=F32)

    use_state = has_init or nblk > 1
    s_f = s0_ref[0, hh] if has_init else jnp.zeros((RET_DK, RET_DV), F32)
    for c in range(nblk):
        if use_state:
            sf_scr[c] = s_f
        if want_state or c < nblk - 1:
            s_f = c_dec_f * s_f + kv_state(k_dec_f, c)
    if want_state:
        sfin_ref[0, hh] = s_f

    s_b = s0_ref[1, hh] if has_init else jnp.zeros((RET_DK, RET_DV), F32)
    for c in range(nblk - 1, -1, -1):
        qc = blk(q, c)
        kc = blk(k, c)
        vc = v_ref[c * c_len:(c + 1) * c_len, v_cols]
        s = lax.dot_general(qc.astype(BF16), kc.astype(BF16), (((1,), (1,)), ((), ())),
                            preferred_element_type=F32)
        o = jnp.dot((s * decay).astype(BF16), vc, preferred_element_type=F32)
        if use_state:
            o = o + jnp.dot((qc * q_dec_f).astype(BF16), sf_scr[c].astype(BF16), preferred_element_type=F32)
            o = o + jnp.dot((qc * q_dec_b).astype(BF16), s_b.astype(BF16), preferred_element_type=F32)
        o = o * lax.rsqrt(jnp.mean(o * o, axis=-1, keepdims=True) + EPS)
        gate = gate_ref[c * c_len:(c + 1) * c_len, v_cols].astype(F32)
        o_ref[c * c_len:(c + 1) * c_len, v_cols] = (o * _silu(gate)).astype(o_ref.dtype)
        if want_state or c > 0:
            s_b = c_dec_b * s_b + kv_state(k_dec_b, c)
    if want_state:
        sfin_ref[1, hh] = s_b


def retention_core(proj3, lg, rope_tabs, s0, want_state, hp):
    nseq, seq_len, _ = proj3.shape
    nblk = seq_len // RET_BLOCK
    rope = rope_tabs is not None
    has_init = s0 is not None
    qw, vw = hp * RET_DK, hp * RET_DV
    in_specs = [pl.BlockSpec((None, seq_len, qw), lambda b, h: (b, 0, h)),
                pl.BlockSpec((None, seq_len, qw), lambda b, h: (b, 0, RET_QK // qw + h)),
                pl.BlockSpec((None, seq_len, vw), lambda b, h: (b, 0, 2 * RET_QK // vw + h)),
                pl.BlockSpec((None, seq_len, vw), lambda b, h: (b, 0, (2 * RET_QK + RET_V) // vw + h)),
                pl.BlockSpec((hp, 2, 128), lambda b, h: (h, 0, 0))]
    args = [proj3, proj3, proj3, proj3, lg]
    if rope:
        tab = pl.BlockSpec((seq_len, RET_DK), lambda b, h: (0, 0))
        in_specs += [tab, tab]
        args += list(rope_tabs)
    sspec = pl.BlockSpec((None, 2, hp, RET_DK, RET_DV), lambda b, h: (b, 0, h, 0, 0))
    if has_init:
        in_specs.append(sspec)
        args.append(s0)
    out_specs = [pl.BlockSpec((None, seq_len, vw), lambda b, h: (b, 0, h))]
    out_shape = [jax.ShapeDtypeStruct((nseq, seq_len, RET_V), BF16)]
    if want_state:
        out_specs.append(sspec)
        out_shape.append(jax.ShapeDtypeStruct((nseq, 2, RET_HEADS, RET_DK, RET_DV), F32))
    body = functools.partial(_ret_body, seq_len=seq_len, rope=rope, has_init=has_init, want_state=want_state)
    res = pl.pallas_call(
        body,
        grid=(nseq, RET_HEADS // hp),
        in_specs=in_specs,
        out_specs=out_specs,
        out_shape=out_shape,
        scratch_shapes=[pltpu.VMEM((nblk, RET_DK, RET_DV), F32)],
        compiler_params=_params("parallel", "parallel", vmem=VMEM_LIMIT_BYTES),
        name="retention",
    )(*args)
    return res[0], (res[1] if want_state else None)


def rope_tables(seq_len):
    t = jnp.arange(seq_len)
    half = RET_DK // 2
    nfreq = half // 2
    inv = ROPE_BASE ** (-jnp.arange(nfreq, dtype=F32) / nfreq)
    parts_c, parts_s = [], []
    for p in (t // GRID_W, t % GRID_W):
        ang = p.astype(F32)[:, None] * inv[None, :]
        parts_c += [jnp.cos(ang), jnp.cos(ang)]
        parts_s += [-jnp.sin(ang), jnp.sin(ang)]
    return jnp.concatenate(parts_c, axis=-1), jnp.concatenate(parts_s, axis=-1)


def retention_layer(x, g, mod, rows_per_mod, nseq, seq_len, s0, grid_pos, w_in, lg, w_out, tm):
    proj = in_proj(x, g, mod, w_in, rows_per_mod, tm, PROJ_COLS)
    proj3 = proj.reshape(nseq, seq_len, 2 * RET_QK + 2 * RET_V)
    tabs = rope_tables(seq_len) if grid_pos else None
    hp = RET_HEADS if seq_len <= RET_BLOCK else 1
    y, states = retention_core(proj3, lg, tabs, s0, want_state=s0 is None, hp=hp)
    x_new = out_proj(y.reshape(nseq * seq_len, RET_V), w_out, x, mod, rows_per_mod, min(tm, OUT_ROWS))
    return x_new, states


def _filter_dft_body(ae_ref, ao_ref, b_ref, norm_ref, o_ref, *, seq_len, tm):
    half = seq_len // 2
    row = pl.program_id(0) * tm + lax.broadcasted_iota(jnp.int32, (tm, 1), 0)
    wf = jnp.where(row % half == 0, 1.0, 2.0) * (1.0 / (2.0 * seq_len))
    acc = (jnp.dot(ae_ref[...], b_ref[0:half, :], preferred_element_type=F32)
           + jnp.dot(ao_ref[...], b_ref[half:, :], preferred_element_type=F32))
    o_ref[...] = acc * wf / (norm_ref[...] + EPS)


def filter_dft(a_even, a_odd, b, norm, tm, tn):
    half = a_even.shape[0]
    n = b.shape[2]
    nh = half // tm
    body = functools.partial(_filter_dft_body, seq_len=2 * half, tm=tm)
    aspec = pl.BlockSpec((tm, half), lambda i, j: (i % nh, 0))
    return pl.pallas_call(
        body,
        grid=(2 * nh, n // tn),
        in_specs=[aspec, aspec,
                  pl.BlockSpec((None, 2 * half, tn), lambda i, j: (i // nh, 0, j)),
                  pl.BlockSpec((1, tn), lambda i, j: (0, j))],
        out_specs=pl.BlockSpec((tm, tn), lambda i, j: (i, j)),
        out_shape=jax.ShapeDtypeStruct((2 * half, n), F32),
        compiler_params=_params("parallel", "parallel", vmem=VMEM_LIMIT_BYTES),
        name="filter_dft",
    )(a_even, a_odd, b, norm)


def _filter_gen_body(feat_ref, w1_ref, b1_ref, w2_ref, b2_ref, w3f_ref, w3b_ref, dl_ref,
                     fs_ref, fd_ref, norm_ref, mre_ref, mim_ref, *, seq_len, tr):
    r = pl.program_id(1)
    hp = lax.Precision.HIGHEST
    z = jnp.sin(jnp.dot(feat_ref[...], w1_ref[...], precision=hp, preferred_element_type=F32) + b1_ref[...])
    z = jnp.sin(jnp.dot(z, w2_ref[...], precision=hp, preferred_element_type=F32) + b2_ref[...]).astype(BF16)
    row = r * tr + lax.broadcasted_iota(jnp.int32, (tr, 1), 0)
    lag = 2 * (row % (seq_len // 2)) + row // (seq_len // 2)
    t = lag.astype(F32) * (1.0 / (seq_len - 1))
    win = jnp.exp(-t * jnp.abs(dl_ref[...]))
    ff = jnp.dot(z, w3f_ref[...], preferred_element_type=F32) * win
    fb = jnp.dot(z, w3b_ref[...], preferred_element_type=F32) * win
    part = jnp.sum(jnp.abs(ff) + jnp.abs(fb), axis=0, keepdims=True)
    fb = jnp.where(lag == 0, 0.0, fb)
    fs = ff + fb
    fd = ff - fb
    quarter = jnp.where(lag % 4 < 2, 1.0, -1.0)
    cos_q = jnp.where(lag % 2 == 0, quarter, 0.0)
    sin_q = jnp.where(lag % 2 == 1, quarter, 0.0)

    @pl.when(r == 0)
    def _():
        norm_ref[...] = jnp.zeros_like(norm_ref)
        mre_ref[...] = jnp.zeros_like(mre_ref)
        mim_ref[...] = jnp.zeros_like(mim_ref)

    norm_ref[...] += part
    mre_ref[...] += jnp.sum(cos_q * fs, axis=0, keepdims=True)
    mim_ref[...] -= jnp.sum(sin_q * fd, axis=0, keepdims=True)
    alt = jnp.where(lag % 2 == 0, 1.0, -1.0)
    fs_ref[0] = fs.astype(fs_ref.dtype)
    fs_ref[1] = (fs * alt).astype(fs_ref.dtype)
    fd_ref[0] = fd.astype(fd_ref.dtype)
    fd_ref[1] = (-(fd * alt)).astype(fd_ref.dtype)


def filter_gen(feat, w1, b1, w2, b2, w3, deltas, tr, tn):
    seq_len, kf = feat.shape
    hid = w1.shape[1]
    ncol = HY_ORDER * HY_WIDTH
    nj = ncol // tn
    body = functools.partial(_filter_gen_body, seq_len=seq_len, tr=tr)
    full = lambda shape: pl.BlockSpec(shape, lambda j, r: (0, 0))
    row = pl.BlockSpec((1, tn), lambda j, r: (0, j))
    return pl.pallas_call(
        body,
        grid=(nj, seq_len // tr),
        in_specs=[pl.BlockSpec((tr, kf), lambda j, r: (r, 0)),
                  full((kf, hid)), full((1, hid)), full((hid, hid)), full((1, hid)),
                  pl.BlockSpec((hid, tn), lambda j, r: (0, j)),
                  pl.BlockSpec((hid, tn), lambda j, r: (0, nj + j)),
                  pl.BlockSpec((1, tn), lambda j, r: (0, j % (HY_WIDTH // tn)))],
        out_specs=[pl.BlockSpec((2, tr, tn), lambda j, r: (0, r, j)),
                   pl.BlockSpec((2, tr, tn), lambda j, r: (0, r, j)), row, row, row],
        out_shape=[jax.ShapeDtypeStruct((2, seq_len, ncol), BF16), jax.ShapeDtypeStruct((2, seq_len, ncol), BF16)]
        + [jax.ShapeDtypeStruct((1, ncol), F32)] * 3,
        compiler_params=_params("parallel", "arbitrary", vmem=VMEM_LIMIT_BYTES),
        name="filter_gen",
    )(feat, w1, b1, w2, b2, w3, w3, deltas)


def _cos_sin(freq, time, seq_len):
    prod = (freq[:, None] * time[None, :]) % (2 * seq_len)
    ang = prod.astype(F32) * (math.pi / seq_len)
    return jnp.cos(ang).astype(BF16), (-jnp.sin(ang)).astype(BF16)


def dft_matrices(seq_len):
    half = seq_len // 2
    lo = jnp.arange(half, dtype=jnp.int32)
    c_e, s_e = _cos_sin(lo, 2 * lo, seq_len)
    c_o, s_o = _cos_sin(lo, 2 * lo + 1, seq_len)
    c_ot, s_ot = _cos_sin(2 * lo + 1, lo, seq_len)
    return c_e, c_o, s_e, s_o, c_ot, s_ot


def hyena_filter_spectrum(seq_len, mats, w1, b1, w2, b2, w3):
    c_e, c_o, s_e, s_o = mats[:4]
    lags = jnp.concatenate([jnp.arange(0, seq_len, 2), jnp.arange(1, seq_len, 2)]).astype(F32)[:, None]
    t = lags / (seq_len - 1.0)
    w = 2.0 * math.pi * lags / seq_len
    f = jnp.linspace(1e-4, HY_BANDS - 1.0, HY_BANDS, dtype=F32)[None, :]
    feat = jnp.concatenate([t, jnp.cos(f * w), -jnp.sin(f * w)], axis=-1)
    emb, hid = w1.shape
    pe, ph = HY_LANES - emb, HY_LANES - hid
    feat = jnp.pad(feat, ((0, 0), (0, pe)))
    w1p = jnp.pad(w1, ((0, pe), (0, ph)))
    w2p = jnp.pad(w2, ((0, ph), (0, ph)))
    w3p = jnp.pad(w3, ((0, ph), (0, 0))).astype(BF16)
    b1p = jnp.pad(b1, (0, ph)).reshape(1, HY_LANES)
    b2p = jnp.pad(b2, (0, ph)).reshape(1, HY_LANES)
    max_decay = math.log(HY_DECAY_TARGET) / HY_SHORT_DECAY_PCT
    min_decay = math.log(HY_DECAY_TARGET) / HY_LONG_DECAY_PCT
    deltas = jnp.linspace(min_decay, max_decay, HY_WIDTH, dtype=F32).reshape(1, HY_WIDTH)
    tile = min(seq_len, 512)
    fs, fd, norm, mid_re, mid_im = filter_gen(feat, w1p, b1p, w2p, b2p, w3p, deltas, tile, 1024)
    kr = filter_dft(c_e, c_o, fs, norm, min(seq_len // 2, tile), 512)
    ki = filter_dft(s_e, s_o, fd, norm, min(seq_len // 2, tile), 512)
    kmid = jnp.stack([mid_re, mid_im], axis=0) / (norm + EPS) / seq_len
    kmid = jnp.transpose(kmid.reshape(2, HY_ORDER, HY_WIDTH), (1, 0, 2)).reshape(2 * HY_ORDER, HY_WIDTH)
    return kr, ki, kmid


def _hyena_body(v_ref, x1_ref, x2_ref, gate_ref, cw_ref, cb_ref, kr0_ref, ki0_ref, kr1_ref, ki1_ref,
                km_ref, sk_ref, ce_ref, co_ref, se_ref, so_ref, cot_ref, sot_ref, o_ref,
                z_scr, xg_scr, p_scr, *, seq_len):
    half = seq_len // 2
    cb = o_ref.shape[-1]
    row = lax.broadcasted_iota(jnp.int32, (half, 1), 0)
    alt = jnp.where(row % 2 == 0, 1.0, -1.0).astype(F32)

    def halves(ref):
        return (ref[:, 0].astype(F32).reshape(half, cb), ref[:, 1].astype(F32).reshape(half, cb))

    def short_conv(x_ref, which, dst):
        xe, xo = halves(x_ref)
        w0, w1, w2 = (cw_ref[j, which:which + 1, :] for j in range(HY_SHORT))
        bias = cb_ref[which:which + 1, :]
        xo_prev = jnp.where(row == 0, 0.0, pltpu.roll(xo, 1, 0))
        xe_next = jnp.where(row == half - 1, 0.0, pltpu.roll(xe, half - 1, 0))
        dst[0] = bias + xo_prev * w0 + xe * w1 + xo * w2
        dst[1] = bias + xe * w0 + xo * w1 + xe_next * w2

    rb = min(half, HY_ROW_BLOCK)
    short_conv(v_ref, 0, z_scr)
    for o, (xg_ref, kr_ref, ki_ref) in enumerate(((x1_ref, kr0_ref, ki0_ref), (x2_ref, kr1_ref, ki1_ref))):
        ze, zo = z_scr[0], z_scr[1]
        zeb, zob = ze.astype(BF16), zo.astype(BF16)
        zm_re = jnp.sum(ze * alt, axis=0, keepdims=True)
        zm_im = -jnp.sum(zo * alt, axis=0, keepdims=True)
        km_re, km_im = km_ref[2 * o:2 * o + 1, :], km_ref[2 * o + 1:2 * o + 2, :]
        pm_re = zm_re * km_re - zm_im * km_im
        pm_im = zm_re * km_im + zm_im * km_re
        for r in range(0, half, rb):
            a_e = jnp.dot(ce_ref[r:r + rb, :], zeb, preferred_element_type=F32)
            a_o = jnp.dot(co_ref[r:r + rb, :], zob, preferred_element_type=F32)
            b_e = jnp.dot(se_ref[r:r + rb, :], zeb, preferred_element_type=F32)
            b_o = jnp.dot(so_ref[r:r + rb, :], zob, preferred_element_type=F32)
            lo_re, lo_im, hi_re, hi_im = a_e + a_o, b_e + b_o, a_e - a_o, b_o - b_e
            kl_re, kl_im = kr_ref[r:r + rb, :], ki_ref[r:r + rb, :]
            kh_re, kh_im = kr_ref[half + r:half + r + rb, :], ki_ref[half + r:half + r + rb, :]
            pl_re, pl_im = lo_re * kl_re - lo_im * kl_im, lo_re * kl_im + lo_im * kl_re
            ph_re, ph_im = hi_re * kh_re - hi_im * kh_im, hi_re * kh_im + hi_im * kh_re
            p_scr[0, r:r + rb, :] = (pl_re + ph_re).astype(BF16)
            p_scr[1, r:r + rb, :] = (pl_im - ph_im).astype(BF16)
            p_scr[2, r:r + rb, :] = (pl_re - ph_re).astype(BF16)
            p_scr[3, r:r + rb, :] = (pl_im + ph_im).astype(BF16)
        short_conv(xg_ref, o + 1, xg_scr)
        sk = sk_ref[o:o + 1, :]
        for r in range(0, half, rb):
            y_e = (jnp.dot(ce_ref[r:r + rb, :], p_scr[0], preferred_element_type=F32)
                   + jnp.dot(se_ref[r:r + rb, :], p_scr[1], preferred_element_type=F32)
                   + alt[r:r + rb] * pm_re)
            y_o = (jnp.dot(cot_ref[r:r + rb, :], p_scr[2], preferred_element_type=F32)
                   + jnp.dot(sot_ref[r:r + rb, :], p_scr[3], preferred_element_type=F32)
                   - alt[r:r + rb] * pm_im)
            z_scr[0, r:r + rb, :] = xg_scr[0, r:r + rb, :] * (y_e + z_scr[0, r:r + rb, :] * sk)
            z_scr[1, r:r + rb, :] = xg_scr[1, r:r + rb, :] * (y_o + z_scr[1, r:r + rb, :] * sk)
    ge, go = halves(gate_ref)
    tiles = o_ref.shape[0]
    o_ref[:, 0] = (z_scr[0] * _silu(ge)).astype(o_ref.dtype).reshape(tiles, HY_PARITY_ROWS // 2, cb)
    o_ref[:, 1] = (z_scr[1] * _silu(go)).astype(o_ref.dtype).reshape(tiles, HY_PARITY_ROWS // 2, cb)


def hyena_core(proj5, conv_w, conv_b, kr, ki, kmid, skip, mats, cb):
    nseq, tiles, _, prow, _ = proj5.shape
    seq_len = tiles * 2 * prow
    half = seq_len // 2
    nj = HY_WIDTH // cb
    once = pl.Buffered(1)

    def act(k):
        return pl.BlockSpec((None, tiles, 2, prow, cb), lambda j, b: (b, 0, 0, 0, k * nj + j))

    def spec(k):
        return pl.BlockSpec((seq_len, cb), lambda j, b: (0, k * nj + j), pipeline_mode=once)

    dft = pl.BlockSpec((half, half), lambda j, b: (0, 0), pipeline_mode=once)
    body = functools.partial(_hyena_body, seq_len=seq_len)
    return pl.pallas_call(
        body,
        grid=(nj, nseq),
        in_specs=[act(0), act(1), act(2), act(3),
                  pl.BlockSpec((HY_SHORT, 3, cb), lambda j, b: (0, 0, j)),
                  pl.BlockSpec((3, cb), lambda j, b: (0, j)),
                  spec(0), spec(0), spec(1), spec(1),
                  pl.BlockSpec((2 * HY_ORDER, cb), lambda j, b: (0, j)),
                  pl.BlockSpec((HY_ORDER, cb), lambda j, b: (0, j))] + [dft] * 6,
        out_specs=pl.BlockSpec((None, tiles, 2, prow, cb), lambda j, b: (b, 0, 0, 0, j)),
        out_shape=jax.ShapeDtypeStruct((nseq, tiles, 2, prow, HY_WIDTH), BF16),
        scratch_shapes=[pltpu.VMEM((2, half, cb), F32), pltpu.VMEM((2, half, cb), F32),
                        pltpu.VMEM((4, half, cb), BF16)],
        compiler_params=_params("parallel", "arbitrary", vmem=VMEM_LIMIT_BYTES),
        name="hyena",
    )(proj5, proj5, proj5, proj5,
      conv_w.reshape(HY_SHORT, 3, HY_WIDTH), conv_b.reshape(3, HY_WIDTH),
      kr, ki, kr, ki, kmid, skip, *mats)


def parity_permutation():
    i = jnp.arange(HY_PARITY_ROWS)
    h = HY_PARITY_ROWS // 2
    src = jnp.where(i < h, 2 * i, 2 * (i - h) + 1)
    p = (src[:, None] == jnp.arange(HY_PARITY_ROWS)[None, :]).astype(BF16)
    return p, jnp.transpose(p)


def _inproj_parity_body(x_ref, g_ref, mod_ref, p_ref, w_ref, o_ref, h_scr):
    @pl.when(pl.program_id(1) == 0)
    def _():
        x = x_ref[...]
        y = x * lax.rsqrt(jnp.mean(x * x, axis=-1, keepdims=True) + EPS) * g_ref[...]
        h = (y * (1.0 + mod_ref[1:2, :]) + mod_ref[0:1, :]).astype(BF16)
        for r in range(0, x.shape[0], HY_PARITY_ROWS):
            h_scr[r:r + HY_PARITY_ROWS, :] = jnp.dot(p_ref[...], h[r:r + HY_PARITY_ROWS],
                                                     preferred_element_type=F32).astype(BF16)

    out = jnp.dot(h_scr[...], w_ref[...], preferred_element_type=F32).astype(o_ref.dtype)
    o_ref[...] = out.reshape(o_ref.shape)


def in_proj_parity(x, g, mod, p, w, rows_per_mod, tm, tn):
    n, nout = x.shape[0], w.shape[1]
    pr = HY_PARITY_ROWS
    return pl.pallas_call(
        _inproj_parity_body,
        grid=(n // tm, nout // tn),
        in_specs=[pl.BlockSpec((tm, D_MODEL), lambda i, j: (i, 0)),
                  pl.BlockSpec((1, D_MODEL), lambda i, j: (0, 0)),
                  pl.BlockSpec((None, 3, D_MODEL), lambda i, j: ((i * tm) // rows_per_mod, 0, 0)),
                  pl.BlockSpec((pr, pr), lambda i, j: (0, 0)),
                  pl.BlockSpec((D_MODEL, tn), lambda i, j: (0, j))],
        out_specs=pl.BlockSpec((tm // pr, 2, pr // 2, tn), lambda i, j: (i, 0, 0, j)),
        out_shape=jax.ShapeDtypeStruct((n // pr, 2, pr // 2, nout), BF16),
        scratch_shapes=[pltpu.VMEM((tm, D_MODEL), BF16)],
        compiler_params=_params("parallel", "arbitrary", vmem=VMEM_LIMIT_BYTES),
        name="in_proj_parity",
    )(x, g.reshape(1, D_MODEL), mod, p, w)


def _outproj_parity_body(y_ref, pt_ref, w_ref, x_ref, mod_ref, o_ref):
    tm = x_ref.shape[0]
    y = y_ref[...].reshape(tm, y_ref.shape[-1])
    parts = [jnp.dot(pt_ref[...], y[r:r + HY_PARITY_ROWS], preferred_element_type=F32).astype(BF16)
             for r in range(0, tm, HY_PARITY_ROWS)]
    out = jnp.dot(jnp.concatenate(parts, axis=0), w_ref[...], preferred_element_type=F32)
    o_ref[...] = x_ref[...] + mod_ref[2:3, :] * out


def out_proj_parity(y, pt, w, x, mod, rows_per_mod, tm):
    n = x.shape[0]
    width = y.shape[-1]
    pr = HY_PARITY_ROWS
    return pl.pallas_call(
        _outproj_parity_body,
        grid=(n // tm,),
        in_specs=[pl.BlockSpec((tm // pr, 2, pr // 2, width), lambda i: (i, 0, 0, 0)),
                  pl.BlockSpec((pr, pr), lambda i: (0, 0)),
                  pl.BlockSpec((width, D_MODEL), lambda i: (0, 0)),
                  pl.BlockSpec((tm, D_MODEL), lambda i: (i, 0)),
                  pl.BlockSpec((None, 3, D_MODEL), lambda i: ((i * tm) // rows_per_mod, 0, 0))],
        out_specs=pl.BlockSpec((tm, D_MODEL), lambda i: (i, 0)),
        out_shape=jax.ShapeDtypeStruct((n, D_MODEL), F32),
        compiler_params=_params("parallel", vmem=VMEM_LIMIT_BYTES),
        name="out_proj_parity",
    )(y, pt, w, x, mod)


def hyena_layer(x, g, mod, rows_per_mod, nseq, seq_len, w_in, conv_w, conv_b, filt_w, skip, w_out, perms, tm, cb):
    par_p, par_pt = perms
    pr = HY_PARITY_ROWS
    proj = in_proj_parity(x, g, mod, par_p, w_in, rows_per_mod, tm, PROJ_COLS)
    proj5 = proj.reshape(nseq, seq_len // pr, 2, pr // 2, 4 * HY_WIDTH)
    mats = dft_matrices(seq_len)
    kr, ki, kmid = hyena_filter_spectrum(seq_len, mats, *filt_w)
    y = hyena_core(proj5, conv_w, conv_b, kr, ki, kmid, skip, mats, cb)
    y = y.reshape(nseq * seq_len // pr, 2, pr // 2, HY_WIDTH)
    return out_proj_parity(y, par_pt, w_out, x, mod, rows_per_mod, min(tm, OUT_ROWS))


def kernel(x_prompt, x_sample, c, state_s5, state_ret, c_ctx, norm_g, mod_w, mod_b, s5_w_in, s5_lam_re, s5_lam_im, s5_log_step, s5_b_re, s5_b_im, s5_c_re, s5_c_im, s5_d, s5_w_glu, s5_b_glu, s5_w_out, ret_w_in, ret_decay_logit, ret_w_out, hy_w_in, hy_conv_w, hy_conv_b, hy_f_w1, hy_f_b1, hy_f_w2, hy_f_b2, hy_f_w3, hy_skip, hy_w_out, final_g):
    n_ctx, l_ctx, _ = x_prompt.shape
    n_dec, l_dec, _ = x_sample.shape
    xc = x_prompt.reshape(n_ctx * l_ctx, D_MODEL)
    xl = x_sample.reshape(n_dec * l_dec, D_MODEL)

    pad = (-(1 + n_dec)) % 8
    cvecs = jnp.concatenate([c_ctx[None, :], c, jnp.zeros((pad, D_MODEL), F32)], axis=0)
    mods = ada_mod_all(cvecs, mod_w, mod_b).reshape(DEPTH, -1, 3, D_MODEL)

    tm = 1024
    perm = s5_lane_permutation()
    last_is_s5 = (DEPTH - 1) % N_MIXERS == 0
    s5_new, ret_new = [], []
    for i in range(DEPTH):
        kind, j = i % N_MIXERS, i // N_MIXERS
        mod_c = mods[i, 0:1]
        mod_l = mods[i, 1:1 + n_dec]
        g = norm_g[i]
        if kind == 0:
            tables = s5_tables(s5_lam_re[j], s5_lam_im[j], s5_log_step[j], s5_b_re[j], s5_b_im[j],
                               s5_c_re[j], s5_c_im[j], s5_d[j])
            w = (s5_w_in[j].astype(BF16), tables, perm, s5_w_glu[j].astype(BF16), s5_b_glu[j],
                 s5_w_out[j].astype(BF16))
            fg = final_g if last_is_s5 and i == DEPTH - 1 else None
            xc, st = s5_layer(xc, g, mod_c, n_ctx, l_ctx, None, *w, nb=n_ctx, final_g=fg)
            xl, _ = s5_layer(xl, g, mod_l, n_dec, l_dec, state_s5[:, j], *w, nb=S5_SEQS_PER_STEP, final_g=fg)
            s5_new.append(st)
        elif kind == 1:
            w_in = ret_w_in[j]
            kscale = jnp.concatenate([jnp.ones((RET_QK,), F32), jnp.full((RET_QK,), RET_DK ** -0.5, F32),
                                      jnp.ones((2 * RET_V,), F32)])
            w_in = (w_in * kscale[None, :]).astype(BF16)
            lg = jax.nn.log_sigmoid(ret_decay_logit[j])
            lg = jnp.broadcast_to(jnp.transpose(lg)[:, :, None], (RET_HEADS, 2, 128))
            w = (w_in, lg, ret_w_out[j].astype(BF16), tm)
            xc, st = retention_layer(xc, g, mod_c, n_ctx * l_ctx, n_ctx, l_ctx, None, False, *w)
            xl, _ = retention_layer(xl, g, mod_l, l_dec, n_dec, l_dec, state_ret[:, j], True, *w)
            ret_new.append(st)
        else:
            filt_w = (hy_f_w1[j], hy_f_b1[j], hy_f_w2[j], hy_f_b2[j], hy_f_w3[j])
            w = (hy_w_in[j].astype(BF16), hy_conv_w[j], hy_conv_b[j], filt_w, hy_skip[j], hy_w_out[j].astype(BF16),
                 parity_permutation(), tm)
            xc = hyena_layer(xc, g, mod_c, n_ctx * l_ctx, n_ctx, l_ctx, *w, cb=1024)
            xl = hyena_layer(xl, g, mod_l, l_dec, n_dec, l_dec, *w, cb=256)

    if not last_is_s5:
        xc, xl = final_norm(xc, final_g, tm), final_norm(xl, final_g, tm)
    y_prompt = xc.reshape(n_ctx, l_ctx, D_MODEL)
    y_sample = xl.reshape(n_dec, l_dec, D_MODEL)
    new_state_s5 = jnp.stack(s5_new, axis=1)
    new_state_ret = jnp.stack(ret_new, axis=1)
    return (y_prompt, y_sample, new_state_s5, new_state_ret)
```

```python
import functools
import math

import jax
import jax.numpy as jnp
from jax import lax
from jax.experimental import pallas as pl
from jax.experimental.pallas import tpu as pltpu

F32 = jnp.float32
BF16 = jnp.bfloat16

D_MODEL = 1024
DEPTH = 4
N_MIXERS = 3
EPS = 1e-6
GRID_W = 64

S5_GROUP = 16
S5_GROUPS = D_MODEL // S5_GROUP
S5_STATE = 64
S5_CHUNK = 16
S5_TILE = S5_CHUNK * S5_GROUP
S5_SLAB = 128 // S5_GROUP
S5_SEQS_PER_STEP = 8
S5_ROWS = 512

RET_HEADS = 8
RET_QK = D_MODEL
RET_V = 2 * D_MODEL
RET_DK = RET_QK // RET_HEADS
RET_DV = RET_V // RET_HEADS
RET_BLOCK = 256
ROPE_BASE = 10000.0

HY_WIDTH = 2 * D_MODEL
HY_ORDER = 2
HY_SHORT = 3
HY_BANDS = 16
HY_SHORT_DECAY_PCT = 0.3
HY_LONG_DECAY_PCT = 1.5
HY_DECAY_TARGET = 1e-2
HY_LANES = 128
HY_PARITY_ROWS = 256
HY_ROW_BLOCK = 512

VMEM_LIMIT_BYTES = 56 * 1024 * 1024
PROJ_COLS = 2048
OUT_ROWS = 1024


def _params(*sem, vmem=None):
    return pltpu.CompilerParams(dimension_semantics=sem, vmem_limit_bytes=vmem)


def _silu(x):
    return x * jax.nn.sigmoid(x)


def _mod_body(c_ref, w_ref, b_ref, o_ref):
    a = _silu(c_ref[...]).astype(BF16)
    o_ref[0] = jnp.dot(a, w_ref[0].astype(BF16), preferred_element_type=F32) + b_ref[0]


def ada_mod_all(cvecs, mod_w, mod_b):
    r = cvecs.shape[0]
    tn = D_MODEL
    return pl.pallas_call(
        _mod_body,
        grid=(DEPTH, 3 * D_MODEL // tn),
        in_specs=[pl.BlockSpec((r, D_MODEL), lambda i, j: (0, 0)),
                  pl.BlockSpec((1, D_MODEL, tn), lambda i, j: (i, 0, j)),
                  pl.BlockSpec((1, 1, tn), lambda i, j: (i, 0, j))],
        out_specs=pl.BlockSpec((1, r, tn), lambda i, j: (i, 0, j)),
        out_shape=jax.ShapeDtypeStruct((DEPTH, r, 3 * D_MODEL), F32),
        compiler_params=_params("parallel", "parallel"),
        name="ada_mod",
    )(cvecs, mod_w, mod_b.reshape(DEPTH, 1, 3 * D_MODEL))


def _inproj_body(x_ref, g_ref, mod_ref, w_ref, o_ref, h_scr):
    @pl.when(pl.program_id(1) == 0)
    def _():
        x = x_ref[...]
        y = x * lax.rsqrt(jnp.mean(x * x, axis=-1, keepdims=True) + EPS) * g_ref[...]
        h_scr[...] = (y * (1.0 + mod_ref[1:2, :]) + mod_ref[0:1, :]).astype(BF16)

    o_ref[...] = jnp.dot(h_scr[...], w_ref[...], preferred_element_type=F32).astype(o_ref.dtype)


def in_proj(x, g, mod, w, rows_per_mod, tm, tn):
    n, nout = x.shape[0], w.shape[1]
    return pl.pallas_call(
        _inproj_body,
        grid=(n // tm, nout // tn),
        in_specs=[pl.BlockSpec((tm, D_MODEL), lambda i, j: (i, 0)),
                  pl.BlockSpec((1, D_MODEL), lambda i, j: (0, 0)),
                  pl.BlockSpec((None, 3, D_MODEL), lambda i, j: ((i * tm) // rows_per_mod, 0, 0)),
                  pl.BlockSpec((D_MODEL, tn), lambda i, j: (0, j))],
        out_specs=pl.BlockSpec((tm, tn), lambda i, j: (i, j)),
        out_shape=jax.ShapeDtypeStruct((n, nout), BF16),
        scratch_shapes=[pltpu.VMEM((tm, D_MODEL), BF16)],
        compiler_params=_params("parallel", "arbitrary", vmem=VMEM_LIMIT_BYTES),
        name="in_proj",
    )(x, g.reshape(1, D_MODEL), mod, w)


def _outproj_body(y_ref, w_ref, x_ref, mod_ref, o_ref):
    out = jnp.dot(y_ref[...], w_ref[...], preferred_element_type=F32)
    o_ref[...] = x_ref[...] + mod_ref[2:3, :] * out


def out_proj(y, w, x, mod, rows_per_mod, tm):
    n, width = y.shape
    return pl.pallas_call(
        _outproj_body,
        grid=(n // tm,),
        in_specs=[pl.BlockSpec((tm, width), lambda i: (i, 0)),
                  pl.BlockSpec((width, D_MODEL), lambda i: (0, 0)),
                  pl.BlockSpec((tm, D_MODEL), lambda i: (i, 0)),
                  pl.BlockSpec((None, 3, D_MODEL), lambda i: ((i * tm) // rows_per_mod, 0, 0))],
        out_specs=pl.BlockSpec((tm, D_MODEL), lambda i: (i, 0)),
        out_shape=jax.ShapeDtypeStruct((n, D_MODEL), F32),
        compiler_params=_params("parallel", vmem=VMEM_LIMIT_BYTES),
        name="out_proj",
    )(y, w, x, mod)


def _final_norm_body(x_ref, g_ref, o_ref):
    x = x_ref[...]
    o_ref[...] = x * lax.rsqrt(jnp.mean(x * x, axis=-1, keepdims=True) + EPS) * g_ref[...]


def final_norm(x, g, tm):
    n = x.shape[0]
    return pl.pallas_call(
        _final_norm_body,
        grid=(n // tm,),
        in_specs=[pl.BlockSpec((tm, D_MODEL), lambda i: (i, 0)),
                  pl.BlockSpec((1, D_MODEL), lambda i: (0, 0))],
        out_specs=pl.BlockSpec((tm, D_MODEL), lambda i: (i, 0)),
        out_shape=jax.ShapeDtypeStruct((n, D_MODEL), F32),
        compiler_params=_params("parallel"),
        name="final_norm",
    )(x, g.reshape(1, D_MODEL))


def s5_tables(lam_re, lam_im, log_step, b_re, b_im, c_re, c_im, d_skip):
    t_len = S5_CHUNK
    dt = jnp.exp(log_step)[..., None]
    ab_re = jnp.exp(lam_re * dt) * jnp.cos(lam_im * dt)
    ab_im = jnp.exp(lam_re * dt) * jnp.sin(lam_im * dt)
    den = lam_re * lam_re + lam_im * lam_im
    nr, ni = ab_re - 1.0, ab_im
    f_re = (nr * lam_re + ni * lam_im) / den
    f_im = (ni * lam_re - nr * lam_im) / den
    bb_re = f_re[..., None] * b_re - f_im[..., None] * b_im
    bb_im = f_re[..., None] * b_im + f_im[..., None] * b_re
    ks = jnp.arange(t_len + 1, dtype=F32)[:, None, None, None]
    pw_mag = jnp.exp(ks * (lam_re * dt)[None])
    pw_re = pw_mag * jnp.cos(ks * (lam_im * dt)[None])
    pw_im = pw_mag * jnp.sin(ks * (lam_im * dt)[None])

    npair = S5_GROUPS // 2
    pair_eye = jnp.eye(2, dtype=F32)

    def pair_pack(a):
        a = a.reshape(2, npair, 2, S5_GROUP, S5_STATE)
        a = a[:, :, :, :, None, :] * pair_eye[None, None, :, None, :, None]
        return jnp.transpose(a.reshape(2, npair, 2, S5_GROUP, 2 * S5_STATE), (1, 0, 2, 3, 4))

    bbt = jnp.stack([pair_pack(jnp.swapaxes(bb_re, 2, 3)), pair_pack(jnp.swapaxes(bb_im, 2, 3))], axis=2)
    cpk = jnp.stack([pair_pack(c_re), pair_pack(c_im)], axis=2)
    pw = jnp.stack([pw_re, pw_im], axis=0).reshape(2, t_len + 1, 2, npair, 2 * S5_STATE)
    pw = jnp.transpose(pw, (3, 2, 0, 1, 4))
    dmat = jnp.eye(S5_GROUP, dtype=F32) * d_skip.reshape(S5_GROUPS, 1, S5_GROUP)
    dmat = jnp.pad(dmat, ((0, 0), (0, 0), (0, S5_TILE - S5_GROUP))).reshape(npair, 2, S5_GROUP, S5_TILE)
    m, f, et = s5_table_kernel(bbt, cpk, pw, dmat)

    dec = jnp.stack([pw_re[t_len, 0], pw_im[t_len, 0], pw_re[t_len, 1], pw_im[t_len, 1]], axis=0)
    dec = jnp.transpose(dec.reshape(4, S5_GROUPS // S5_SLAB, S5_SLAB * S5_STATE), (1, 0, 2))
    return m.reshape(S5_GROUPS, S5_TILE, S5_TILE), f, et, dec


def _s5_table_body(bb_ref, c_ref, pw_ref, d_ref, m_ref, f_ref, et_ref):
    t_len = S5_CHUNK
    nt_dims = (((1,), (1,)), ((), ()))
    hp = lax.Precision.HIGHEST

    def times_power(ref, d, g, k):
        pr, pi = pw_ref[d, 0, k:k + 1, :], pw_ref[d, 1, k:k + 1, :]
        ar, ai = ref[d, 0, g], ref[d, 1, g]
        return ar * pr - ai * pi, ar * pi + ai * pr

    def stack(parts):
        return jnp.concatenate(parts, axis=0)

    for d in range(2):
        f_pow = [t_len - 1 - s for s in range(t_len)] if d == 0 else list(range(t_len))
        e_pow = [s + 1 for s in range(t_len)] if d == 0 else [t_len - s for s in range(t_len)]
        fr, fi, er, ei = [], [], [], []
        for g in range(2):
            for s in range(t_len):
                a, b = times_power(bb_ref, d, g, f_pow[s])
                fr.append(a)
                fi.append(b)
                a, b = times_power(c_ref, d, g, e_pow[s])
                er.append(a)
                ei.append(-b)
        lanes = 2 * S5_STATE
        f_ref[:, (2 * d) * lanes:(2 * d + 1) * lanes] = stack(fr).astype(f_ref.dtype)
        f_ref[:, (2 * d + 1) * lanes:(2 * d + 2) * lanes] = stack(fi).astype(f_ref.dtype)
        et_ref[:, (2 * d) * lanes:(2 * d + 1) * lanes] = stack(er).astype(et_ref.dtype)
        et_ref[:, (2 * d + 1) * lanes:(2 * d + 2) * lanes] = stack(ei).astype(et_ref.dtype)

    zeros = jnp.zeros((S5_GROUP, S5_TILE), F32)
    for g in range(2):
        def lag_kernels(d, powers):
            car, cai = zip(*(times_power(c_ref, d, g, k) for k in powers))
            return (lax.dot_general(bb_ref[d, 0, g], stack(car), nt_dims, precision=hp, preferred_element_type=F32)
                    - lax.dot_general(bb_ref[d, 1, g], stack(cai), nt_dims, precision=hp,
                                      preferred_element_type=F32))
        v_f = jnp.concatenate([zeros, lag_kernels(0, range(t_len)) + d_ref[g]], axis=1)
        v_b = jnp.concatenate([lag_kernels(1, range(t_len - 1, -1, -1)), zeros], axis=1)
        width = 2 * S5_TILE
        for s in range(t_len):
            blk_f = pltpu.roll(v_f, (width - (t_len - s) * S5_GROUP) % width, 1)[:, :S5_TILE]
            blk_b = pltpu.roll(v_b, (width - (t_len - 1 - s) * S5_GROUP) % width, 1)[:, :S5_TILE]
            m_ref[g, s * S5_GROUP:(s + 1) * S5_GROUP, :] = (blk_f + blk_b).astype(m_ref.dtype)


def s5_table_kernel(bbt, cpk, pw, dmat):
    npair = S5_GROUPS // 2
    pspec = pl.BlockSpec((None, 2, 2, 2, S5_GROUP, 2 * S5_STATE), lambda k: (k, 0, 0, 0, 0, 0))
    wide = pl.BlockSpec((None, 2 * S5_TILE, 2 * S5_TILE), lambda k: (k, 0, 0))
    return pl.pallas_call(
        _s5_table_body,
        grid=(npair,),
        in_specs=[pspec, pspec,
                  pl.BlockSpec((None, 2, 2, S5_CHUNK + 1, 2 * S5_STATE), lambda k: (k, 0, 0, 0, 0)),
                  pl.BlockSpec((None, 2, S5_GROUP, S5_TILE), lambda k: (k, 0, 0, 0))],
        out_specs=[pl.BlockSpec((None, 2, S5_TILE, S5_TILE), lambda k: (k, 0, 0, 0)), wide, wide],
        out_shape=[jax.ShapeDtypeStruct((npair, 2, S5_TILE, S5_TILE), BF16),
                   jax.ShapeDtypeStruct((npair, 2 * S5_TILE, 2 * S5_TILE), BF16),
                   jax.ShapeDtypeStruct((npair, 2 * S5_TILE, 2 * S5_TILE), BF16)],
        compiler_params=_params("parallel"),
        name="s5_tables",
    )(bbt, cpk, pw, dmat)


def s5_lane_permutation():
    width = S5_CHUNK * S5_SLAB * S5_GROUP
    i = jnp.arange(width)
    step, grp, ch = i // (S5_SLAB * S5_GROUP), (i // S5_GROUP) % S5_SLAB, i % S5_GROUP
    dst = grp * S5_TILE + step * S5_GROUP + ch
    return (dst[:, None] == jnp.arange(width)[None, :]).astype(BF16)


def _s5_scan_body(u_ref, q_ref, m_ref, f_ref, et_ref, dec_ref, h0_ref, y_ref, hf_ref,
                  uy_s, st_s, *, nchunk, nseq):
    npair = S5_SLAB // 2
    pair_w = 2 * S5_TILE
    lane_w = 2 * S5_STATE
    nt_dims = (((1,), (1,)), ((), ()))
    xcat = jnp.concatenate([u_ref[s] for s in range(S5_CHUNK)], axis=1)
    for k in range(npair):
        cols = slice(k * pair_w, (k + 1) * pair_w)
        up = jnp.dot(xcat, q_ref[:, cols], preferred_element_type=F32).astype(BF16)
        uy_s[:, cols] = up
        x = jnp.dot(up, f_ref[k], preferred_element_type=F32)
        for q in range(4):
            st_s[q * npair + k] = x[:, q * lane_w:(q + 1) * lane_w]

    def lanes_of(ref, q, k):
        return ref[q, :, k * lane_w:(k + 1) * lane_w]

    dec = [[dec_ref[q:q + 1, k * lane_w:(k + 1) * lane_w] for k in range(npair)] for q in range(4)]

    def step(i, carry):
        rows_f = pl.ds(pl.multiple_of(i * nseq, nseq), nseq)
        rows_b = pl.ds(pl.multiple_of((nchunk - 1 - i) * nseq, nseq), nseq)
        new = []
        for k in range(npair):
            fr, fi, br, bi = (carry[q * npair + k] for q in range(4))
            xr, xi = st_s[0 * npair + k, rows_f, :], st_s[1 * npair + k, rows_f, :]
            yr, yi = st_s[2 * npair + k, rows_b, :], st_s[3 * npair + k, rows_b, :]
            st_s[0 * npair + k, rows_f, :] = fr
            st_s[1 * npair + k, rows_f, :] = fi
            st_s[2 * npair + k, rows_b, :] = br
            st_s[3 * npair + k, rows_b, :] = bi
            new.append((dec[0][k] * fr - dec[1][k] * fi + xr, dec[0][k] * fi + dec[1][k] * fr + xi,
                        dec[2][k] * br - dec[3][k] * bi + yr, dec[2][k] * bi + dec[3][k] * br + yi))
        return tuple(new[k][q] for q in range(4) for k in range(npair))

    init = tuple(lanes_of(h0_ref, q, k) for q in range(4) for k in range(npair))
    final = lax.fori_loop(0, nchunk, step, init)
    for q in range(4):
        for k in range(npair):
            hf_ref[q, :, k * lane_w:(k + 1) * lane_w] = final[q * npair + k]

    for k in range(npair):
        h = jnp.concatenate([st_s[q * npair + k] for q in range(4)], axis=1)
        ye = lax.dot_general(h.astype(BF16), et_ref[k], nt_dims, preferred_element_type=F32)
        for g in range(2):
            cols = slice(k * pair_w + g * S5_TILE, k * pair_w + (g + 1) * S5_TILE)
            y = jnp.dot(uy_s[:, cols], m_ref[2 * k + g], preferred_element_type=F32)
            uy_s[:, cols] = (y + ye[:, g * S5_TILE:(g + 1) * S5_TILE]).astype(BF16)
    slab_w = S5_SLAB * S5_GROUP
    steps = pair_w // slab_w
    for t0 in range(0, S5_CHUNK, steps):
        yp = lax.dot_general(uy_s[...], q_ref[t0 * slab_w:(t0 + steps) * slab_w, :], nt_dims,
                             preferred_element_type=F32).astype(y_ref.dtype)
        for t in range(steps):
            y_ref[t0 + t] = yp[:, t * slab_w:(t + 1) * slab_w]


def s5_scan(u_s, perm, m, f, e, dec, h0, nchunk, nseq, nb):
    rows = nchunk * nb
    nslab = S5_GROUPS // S5_SLAB
    slab_w = S5_SLAB * S5_GROUP
    state_w = S5_SLAB * S5_STATE
    width = S5_CHUNK * slab_w
    body = functools.partial(_s5_scan_body, nchunk=nchunk, nseq=nb)
    hspec = pl.BlockSpec((None, None, 4, nb, state_w), lambda s, b: (s, b, 0, 0, 0))
    state = pltpu.VMEM((2 * S5_SLAB, rows, 2 * S5_STATE), F32)
    once = pl.Buffered(1)
    return pl.pallas_call(
        body,
        grid=(nslab, nseq // nb),
        in_specs=[pl.BlockSpec((S5_CHUNK, rows, slab_w), lambda s, b: (0, b, s)),
                  pl.BlockSpec((width, width), lambda s, b: (0, 0), pipeline_mode=once),
                  pl.BlockSpec((S5_SLAB, S5_TILE, S5_TILE), lambda s, b: (s, 0, 0)),
                  pl.BlockSpec((S5_SLAB // 2, 2 * S5_TILE, 2 * S5_TILE), lambda s, b: (s, 0, 0)),
                  pl.BlockSpec((S5_SLAB // 2, 2 * S5_TILE, 2 * S5_TILE), lambda s, b: (s, 0, 0)),
                  pl.BlockSpec((None, 4, state_w), lambda s, b: (s, 0, 0)),
                  hspec],
        out_specs=[pl.BlockSpec((S5_CHUNK, rows, slab_w), lambda s, b: (0, b, s)), hspec],
        out_shape=[jax.ShapeDtypeStruct((S5_CHUNK, nseq * nchunk, D_MODEL), BF16),
                   jax.ShapeDtypeStruct((nslab, nseq // nb, 4, nb, state_w), F32)],
        scratch_shapes=[pltpu.VMEM((rows, width), BF16), state],
        compiler_params=_params("parallel", "parallel", vmem=VMEM_LIMIT_BYTES),
        name="s5_scan",
    )(u_s, perm, m, f, e, dec, h0)


def _s5_half_tile(nb, rows):
    by_chunk = rows >= 2 * S5_CHUNK
    return (nb, rows // 2, True) if by_chunk else (nb // 2, rows, False)


def s5_row_permutation(nb):
    nbh, rows_h, _ = _s5_half_tile(nb, S5_ROWS // nb)
    cph = rows_h // S5_CHUNK
    n = S5_ROWS // 2
    i = jnp.arange(n)
    step, chunk, seq = i // (cph * nbh), (i // nbh) % cph, i % nbh
    src = seq * (cph * S5_CHUNK) + chunk * S5_CHUNK + step
    p = (src[:, None] == jnp.arange(n)[None, :]).astype(BF16)
    return p, jnp.transpose(p)


def _s5_half(ref, h):
    nb, rows, _ = ref.shape
    nbh, rows_h, by_chunk = _s5_half_tile(nb, rows)
    if by_chunk:
        return (slice(None), slice(h * rows_h, (h + 1) * rows_h), slice(None))
    return (slice(h * nbh, (h + 1) * nbh), slice(None), slice(None))


def _s5_tile_specs(x3, mod, nb):
    nseq, seq_len, _ = x3.shape
    cpt = S5_ROWS // (S5_CHUNK * nb)
    ctiles = seq_len // (cpt * S5_CHUNK)
    grid = (nseq // nb, ctiles)
    xspec = pl.BlockSpec((nb, cpt * S5_CHUNK, D_MODEL), lambda b, c: (b, c, 0))
    if mod.shape[0] == 1:
        mspec = pl.BlockSpec((1, 3, D_MODEL), lambda b, c: (0, 0, 0))
    else:
        mspec = pl.BlockSpec((nb, 3, D_MODEL), lambda b, c: (b, 0, 0))

    def stepped(width, col):
        return pl.BlockSpec((S5_CHUNK, S5_ROWS // S5_CHUNK, width), lambda b, c: (0, b * ctiles + c, col))

    return grid, xspec, mspec, stepped


def _inproj_steps_body(x_ref, g_ref, mod_ref, p_ref, w_ref, o_ref):
    x = x_ref[...]
    y = x * lax.rsqrt(jnp.mean(x * x, axis=-1, keepdims=True) + EPS) * g_ref[...]
    hmod = (y * (1.0 + mod_ref[:, 1:2, :]) + mod_ref[:, 0:1, :]).astype(BF16)
    half = S5_ROWS // 2
    parts = [jnp.dot(p_ref[...], hmod[_s5_half(x_ref, h)].reshape(half, D_MODEL),
                     preferred_element_type=F32).astype(BF16) for h in range(2)]
    out = jnp.dot(jnp.concatenate(parts, axis=0), w_ref[...], preferred_element_type=F32).astype(o_ref.dtype)
    hr = o_ref.shape[1] // 2
    for h in range(2):
        o_ref[:, h * hr:(h + 1) * hr, :] = out[h * half:(h + 1) * half].reshape(S5_CHUNK, hr, o_ref.shape[2])


def in_proj_steps(x3, g, mod, p, w, nb):
    nseq, seq_len, _ = x3.shape
    nout = w.shape[1]
    grid, xspec, mspec, stepped = _s5_tile_specs(x3, mod, nb)
    return pl.pallas_call(
        _inproj_steps_body,
        grid=grid,
        in_specs=[xspec,
                  pl.BlockSpec((1, D_MODEL), lambda b, c: (0, 0)),
                  mspec,
                  pl.BlockSpec((S5_ROWS // 2, S5_ROWS // 2), lambda b, c: (0, 0)),
                  pl.BlockSpec((D_MODEL, nout), lambda b, c: (0, 0))],
        out_specs=stepped(nout, 0),
        out_shape=jax.ShapeDtypeStruct((S5_CHUNK, nseq * seq_len // S5_CHUNK, nout), BF16),
        compiler_params=_params("parallel", "parallel", vmem=VMEM_LIMIT_BYTES),
        name="in_proj_steps",
    )(x3, g.reshape(1, D_MODEL), mod, p, w)


def _s5_out_body(y_ref, gate_ref, wg_ref, bg_ref, wo_ref, pt_ref, x_ref, mod_ref, *rest):
    o_ref = rest[-1]
    half = S5_ROWS // 2
    hr = y_ref.shape[1] // 2
    g = jax.nn.gelu(y_ref[...].astype(F32).reshape(S5_ROWS, D_MODEL))
    t = jnp.dot(g.astype(BF16), wg_ref[...], preferred_element_type=F32) + bg_ref[...]
    z = g * jax.nn.sigmoid(t) * _silu(gate_ref[...].astype(F32).reshape(S5_ROWS, D_MODEL))
    z = z.astype(BF16).reshape(S5_CHUNK, 2 * hr, D_MODEL)
    parts = [jnp.dot(pt_ref[...], z[:, h * hr:(h + 1) * hr, :].reshape(half, D_MODEL),
                     preferred_element_type=F32).astype(BF16) for h in range(2)]
    out = jnp.dot(jnp.concatenate(parts, axis=0), wo_ref[...], preferred_element_type=F32)
    nb = x_ref.shape[0]
    nbh, _, by_chunk = _s5_half_tile(nb, x_ref.shape[1])
    for h in range(2):
        idx = _s5_half(x_ref, h)
        per_seq_half = not by_chunk and mod_ref.shape[0] == nb
        gate_mod = mod_ref[h * nbh:(h + 1) * nbh, 2:3, :] if per_seq_half else mod_ref[:, 2:3, :]
        xh = x_ref[idx]
        x = xh + gate_mod * out[h * half:(h + 1) * half].reshape(xh.shape)
        if len(rest) == 2:
            x = x * lax.rsqrt(jnp.mean(x * x, axis=-1, keepdims=True) + EPS) * rest[0][...]
        o_ref[idx] = x


def s5_out(y_s, u_s, w_glu, b_glu, w_out, pt, x3, mod, nb, final_g=None):
    grid, xspec, mspec, stepped = _s5_tile_specs(x3, mod, nb)
    wspec = pl.BlockSpec((D_MODEL, D_MODEL), lambda b, c: (0, 0))
    vspec = pl.BlockSpec((1, D_MODEL), lambda b, c: (0, 0))
    in_specs = [stepped(D_MODEL, 0), stepped(D_MODEL, 1), wspec, vspec, wspec,
                pl.BlockSpec((S5_ROWS // 2, S5_ROWS // 2), lambda b, c: (0, 0)), xspec, mspec]
    args = [y_s, u_s, w_glu, b_glu.reshape(1, D_MODEL), w_out, pt, x3, mod]
    if final_g is not None:
        in_specs.append(vspec)
        args.append(final_g.reshape(1, D_MODEL))
    return pl.pallas_call(
        _s5_out_body,
        grid=grid,
        in_specs=in_specs,
        out_specs=xspec,
        out_shape=jax.ShapeDtypeStruct(x3.shape, F32),
        compiler_params=_params("parallel", "parallel", vmem=VMEM_LIMIT_BYTES),
        name="s5_out",
    )(*args)


def s5_layer(x, g, mod, nseq, seq_len, h0, w_in, tables, perm, w_glu, b_glu, w_out, nb, final_g=None):
    nchunk = seq_len // S5_CHUNK
    nslab = S5_GROUPS // S5_SLAB
    state_w = S5_SLAB * S5_STATE
    row_p, row_pt = s5_row_permutation(nb)
    x3 = x.reshape(nseq, seq_len, D_MODEL)
    u_s = in_proj_steps(x3, g, mod, row_p, w_in, nb)
    if h0 is None:
        h0_p = jnp.zeros((nslab, nseq // nb, 4, nb, state_w), F32)
    else:
        h0_p = jnp.transpose(h0.reshape(nseq // nb, nb, 4, nslab, state_w), (3, 0, 2, 1, 4))
    m, f, e, dec = tables
    y_s, hf = s5_scan(u_s, perm, m, f, e, dec, h0_p, nchunk, nseq, nb)
    x_new = s5_out(y_s, u_s, w_glu, b_glu, w_out, row_pt, x3, mod, nb, final_g).reshape(nseq * seq_len, D_MODEL)
    states = jnp.transpose(hf, (1, 3, 2, 0, 4)).reshape(nseq, 2, 2, S5_GROUPS, S5_STATE)
    return x_new, states


def _ret_body(*refs, seq_len, rope, has_init, want_state):
    refs = list(refs)
    q_ref, k_ref, v_ref, gate_ref, lg_ref = refs[:5]
    pos = 5
    if rope:
        cos_ref, sin_ref = refs[pos:pos + 2]
        pos += 2
    if has_init:
        s0_ref = refs[pos]
        pos += 1
    o_ref = refs[pos]
    pos += 1
    if want_state:
        sfin_ref = refs[pos]
        pos += 1
    sf_scr = refs[pos]

    c_len = RET_BLOCK
    nblk = seq_len // c_len
    ii = lax.broadcasted_iota(jnp.int32, (c_len, c_len), 0)
    jj = lax.broadcasted_iota(jnp.int32, (c_len, c_len), 1)
    diff = (ii - jj).astype(F32)
    ic = lax.broadcasted_iota(jnp.int32, (c_len, 1), 0).astype(F32)
    for hh in range(lg_ref.shape[0]):
        _ret_head(hh, q_ref, k_ref, v_ref, gate_ref, lg_ref,
                  (cos_ref, sin_ref) if rope else None, s0_ref if has_init else None,
                  o_ref, sfin_ref if want_state else None, sf_scr, diff, ic, seq_len)


def _ret_head(hh, q_ref, k_ref, v_ref, gate_ref, lg_ref, rope_refs, s0_ref, o_ref, sfin_ref, sf_scr,
              diff, ic, seq_len):
    rope, has_init, want_state = rope_refs is not None, s0_ref is not None, sfin_ref is not None
    c_len = RET_BLOCK
    nblk = seq_len // c_len
    qk_cols = slice(hh * RET_DK, (hh + 1) * RET_DK)
    v_cols = slice(hh * RET_DV, (hh + 1) * RET_DV)
    lgf = lg_ref[hh, 0:1, 0:1]
    lgb = lg_ref[hh, 1:2, 0:1]
    decay = jnp.exp(jnp.abs(diff) * jnp.where(diff >= 0, lgf, lgb))
    q_dec_f = jnp.exp((ic + 1.0) * lgf)
    q_dec_b = jnp.exp((c_len - ic) * lgb)
    k_dec_f = jnp.exp((c_len - 1.0 - ic) * lgf)
    k_dec_b = jnp.exp(ic * lgb)
    c_dec_f = jnp.exp(c_len * lgf)
    c_dec_b = jnp.exp(c_len * lgb)

    q = q_ref[:, qk_cols].astype(F32)
    k = k_ref[:, qk_cols].astype(F32)
    if rope:
        cos_ref, sin_ref = rope_refs
        lane = lax.broadcasted_iota(jnp.int32, (seq_len, RET_DK), 1)
        first = (lane % (RET_DK // 2)) < (RET_DK // 4)
        cos = cos_ref[...]
        sin = sin_ref[...]

        def rot(x):
            swapped = jnp.where(first, pltpu.roll(x, RET_DK - RET_DK // 4, 1), pltpu.roll(x, RET_DK // 4, 1))
            return x * cos + swapped * sin

        q = rot(q)
        k = rot(k)

    def blk(a, c):
        return a[c * c_len:(c + 1) * c_len]

    def kv_state(kd, c):
        kt = jnp.transpose(blk(k, c) * kd).astype(BF16)
        return jnp.dot(kt, v_ref[c * c_len:(c + 1) * c_len, v_cols], preferred_element_type=F32)

    use_state = has_init or nblk > 1
    s_f = s0_ref[0, hh] if has_init else jnp.zeros((RET_DK, RET_DV), F32)
    for c in range(nblk):
        if use_state:
            sf_scr[c] = s_f
        if want_state or c < nblk - 1:
            s_f = c_dec_f * s_f + kv_state(k_dec_f, c)
    if want_state:
        sfin_ref[0, hh] = s_f

    s_b = s0_ref[1, hh] if has_init else jnp.zeros((RET_DK, RET_DV), F32)
    for c in range(nblk - 1, -1, -1):
        qc = blk(q, c)
        kc = blk(k, c)
        vc = v_ref[c * c_len:(c + 1) * c_len, v_cols]
        s = lax.dot_general(qc.astype(BF16), kc.astype(BF16), (((1,), (1,)), ((), ())),
                            preferred_element_type=F32)
        o = jnp.dot((s * decay).astype(BF16), vc, preferred_element_type=F32)
        if use_state:
            o = o + jnp.dot((qc * q_dec_f).astype(BF16), sf_scr[c].astype(BF16), preferred_element_type=F32)
            o = o + jnp.dot((qc * q_dec_b).astype(BF16), s_b.astype(BF16), preferred_element_type=F32)
        o = o * lax.rsqrt(jnp.mean(o * o, axis=-1, keepdims=True) + EPS)
        gate = gate_ref[c * c_len:(c + 1) * c_len, v_cols].astype(F32)
        o_ref[c * c_len:(c + 1) * c_len, v_cols] = (o * _silu(gate)).astype(o_ref.dtype)
        if want_state or c > 0:
            s_b = c_dec_b * s_b + kv_state(k_dec_b, c)
    if want_state:
        sfin_ref[1, hh] = s_b


def retention_core(proj3, lg, rope_tabs, s0, want_state, hp):
    nseq, seq_len, _ = proj3.shape
    nblk = seq_len // RET_BLOCK
    rope = rope_tabs is not None
    has_init = s0 is not None
    qw, vw = hp * RET_DK, hp * RET_DV
    in_specs = [pl.BlockSpec((None, seq_len, qw), lambda b, h: (b, 0, h)),
                pl.BlockSpec((None, seq_len, qw), lambda b, h: (b, 0, RET_QK // qw + h)),
                pl.BlockSpec((None, seq_len, vw), lambda b, h: (b, 0, 2 * RET_QK // vw + h)),
                pl.BlockSpec((None, seq_len, vw), lambda b, h: (b, 0, (2 * RET_QK + RET_V) // vw + h)),
                pl.BlockSpec((hp, 2, 128), lambda b, h: (h, 0, 0))]
    args = [proj3, proj3, proj3, proj3, lg]
    if rope:
        tab = pl.BlockSpec((seq_len, RET_DK), lambda b, h: (0, 0))
        in_specs += [tab, tab]
        args += list(rope_tabs)
    sspec = pl.BlockSpec((None, 2, hp, RET_DK, RET_DV), lambda b, h: (b, 0, h, 0, 0))
    if has_init:
        in_specs.append(sspec)
        args.append(s0)
    out_specs = [pl.BlockSpec((None, seq_len, vw), lambda b, h: (b, 0, h))]
    out_shape = [jax.ShapeDtypeStruct((nseq, seq_len, RET_V), BF16)]
    if want_state:
        out_specs.append(sspec)
        out_shape.append(jax.ShapeDtypeStruct((nseq, 2, RET_HEADS, RET_DK, RET_DV), F32))
    body = functools.partial(_ret_body, seq_len=seq_len, rope=rope, has_init=has_init, want_state=want_state)
    res = pl.pallas_call(
        body,
        grid=(nseq, RET_HEADS // hp),
        in_specs=in_specs,
        out_specs=out_specs,
        out_shape=out_shape,
        scratch_shapes=[pltpu.VMEM((nblk, RET_DK, RET_DV), F32)],
        compiler_params=_params("parallel", "parallel", vmem=VMEM_LIMIT_BYTES),
        name="retention",
    )(*args)
    return res[0], (res[1] if want_state else None)


def rope_tables(seq_len):
    t = jnp.arange(seq_len)
    half = RET_DK // 2
    nfreq = half // 2
    inv = ROPE_BASE ** (-jnp.arange(nfreq, dtype=F32) / nfreq)
    parts_c, parts_s = [], []
    for p in (t // GRID_W, t % GRID_W):
        ang = p.astype(F32)[:, None] * inv[None, :]
        parts_c += [jnp.cos(ang), jnp.cos(ang)]
        parts_s += [-jnp.sin(ang), jnp.sin(ang)]
    return jnp.concatenate(parts_c, axis=-1), jnp.concatenate(parts_s, axis=-1)


def retention_layer(x, g, mod, rows_per_mod, nseq, seq_len, s0, grid_pos, w_in, lg, w_out, tm):
    proj = in_proj(x, g, mod, w_in, rows_per_mod, tm, PROJ_COLS)
    proj3 = proj.reshape(nseq, seq_len, 2 * RET_QK + 2 * RET_V)
    tabs = rope_tables(seq_len) if grid_pos else None
    hp = RET_HEADS if seq_len <= RET_BLOCK else 1
    y, states = retention_core(proj3, lg, tabs, s0, want_state=s0 is None, hp=hp)
    x_new = out_proj(y.reshape(nseq * seq_len, RET_V), w_out, x, mod, rows_per_mod, min(tm, OUT_ROWS))
    return x_new, states


def _filter_dft_body(ae_ref, ao_ref, b_ref, norm_ref, o_ref, *, seq_len, tm):
    half = seq_len // 2
    row = pl.program_id(0) * tm + lax.broadcasted_iota(jnp.int32, (tm, 1), 0)
    wf = jnp.where(row % half == 0, 1.0, 2.0) * (1.0 / (2.0 * seq_len))
    acc = (jnp.dot(ae_ref[...], b_ref[0:half, :], preferred_element_type=F32)
           + jnp.dot(ao_ref[...], b_ref[half:, :], preferred_element_type=F32))
    o_ref[...] = acc * wf / (norm_ref[...] + EPS)


def filter_dft(a_even, a_odd, b, norm, tm, tn):
    half = a_even.shape[0]
    n = b.shape[2]
    nh = half // tm
    body = functools.partial(_filter_dft_body, seq_len=2 * half, tm=tm)
    aspec = pl.BlockSpec((tm, half), lambda i, j: (i % nh, 0))
    return pl.pallas_call(
        body,
        grid=(2 * nh, n // tn),
        in_specs=[aspec, aspec,
                  pl.BlockSpec((None, 2 * half, tn), lambda i, j: (i // nh, 0, j)),
                  pl.BlockSpec((1, tn), lambda i, j: (0, j))],
        out_specs=pl.BlockSpec((tm, tn), lambda i, j: (i, j)),
        out_shape=jax.ShapeDtypeStruct((2 * half, n), F32),
        compiler_params=_params("parallel", "parallel", vmem=VMEM_LIMIT_BYTES),
        name="filter_dft",
    )(a_even, a_odd, b, norm)


def _filter_gen_body(feat_ref, w1_ref, b1_ref, w2_ref, b2_ref, w3f_ref, w3b_ref, dl_ref,
                     fs_ref, fd_ref, norm_ref, mre_ref, mim_ref, *, seq_len, tr):
    r = pl.program_id(1)
    hp = lax.Precision.HIGHEST
    z = jnp.sin(jnp.dot(feat_ref[...], w1_ref[...], precision=hp, preferred_element_type=F32) + b1_ref[...])
    z = jnp.sin(jnp.dot(z, w2_ref[...], precision=hp, preferred_element_type=F32) + b2_ref[...]).astype(BF16)
    row = r * tr + lax.broadcasted_iota(jnp.int32, (tr, 1), 0)
    lag = 2 * (row % (seq_len // 2)) + row // (seq_len // 2)
    t = lag.astype(F32) * (1.0 / (seq_len - 1))
    win = jnp.exp(-t * jnp.abs(dl_ref[...]))
    ff = jnp.dot(z, w3f_ref[...], preferred_element_type=F32) * win
    fb = jnp.dot(z, w3b_ref[...], preferred_element_type=F32) * win
    part = jnp.sum(jnp.abs(ff) + jnp.abs(fb), axis=0, keepdims=True)
    fb = jnp.where(lag == 0, 0.0, fb)
    fs = ff + fb
    fd = ff - fb
    quarter = jnp.where(lag % 4 < 2, 1.0, -1.0)
    cos_q = jnp.where(lag % 2 == 0, quarter, 0.0)
    sin_q = jnp.where(lag % 2 == 1, quarter, 0.0)

    @pl.when(r == 0)
    def _():
        norm_ref[...] = jnp.zeros_like(norm_ref)
        mre_ref[...] = jnp.zeros_like(mre_ref)
        mim_ref[...] = jnp.zeros_like(mim_ref)

    norm_ref[...] += part
    mre_ref[...] += jnp.sum(cos_q * fs, axis=0, keepdims=True)
    mim_ref[...] -= jnp.sum(sin_q * fd, axis=0, keepdims=True)
    alt = jnp.where(lag % 2 == 0, 1.0, -1.0)
    fs_ref[0] = fs.astype(fs_ref.dtype)
    fs_ref[1] = (fs * alt).astype(fs_ref.dtype)
    fd_ref[0] = fd.astype(fd_ref.dtype)
    fd_ref[1] = (-(fd * alt)).astype(fd_ref.dtype)


def filter_gen(feat, w1, b1, w2, b2, w3, deltas, tr, tn):
    seq_len, kf = feat.shape
    hid = w1.shape[1]
    ncol = HY_ORDER * HY_WIDTH
    nj = ncol // tn
    body = functools.partial(_filter_gen_body, seq_len=seq_len, tr=tr)
    full = lambda shape: pl.BlockSpec(shape, lambda j, r: (0, 0))
    row = pl.BlockSpec((1, tn), lambda j, r: (0, j))
    return pl.pallas_call(
        body,
        grid=(nj, seq_len // tr),
        in_specs=[pl.BlockSpec((tr, kf), lambda j, r: (r, 0)),
                  full((kf, hid)), full((1, hid)), full((hid, hid)), full((1, hid)),
                  pl.BlockSpec((hid, tn), lambda j, r: (0, j)),
                  pl.BlockSpec((hid, tn), lambda j, r: (0, nj + j)),
                  pl.BlockSpec((1, tn), lambda j, r: (0, j % (HY_WIDTH // tn)))],
        out_specs=[pl.BlockSpec((2, tr, tn), lambda j, r: (0, r, j)),
                   pl.BlockSpec((2, tr, tn), lambda j, r: (0, r, j)), row, row, row],
        out_shape=[jax.ShapeDtypeStruct((2, seq_len, ncol), BF16), jax.ShapeDtypeStruct((2, seq_len, ncol), BF16)]
        + [jax.ShapeDtypeStruct((1, ncol), F32)] * 3,
        compiler_params=_params("parallel", "arbitrary", vmem=VMEM_LIMIT_BYTES),
        name="filter_gen",
    )(feat, w1, b1, w2, b2, w3, w3, deltas)


def _cos_sin(freq, time, seq_len):
    prod = (freq[:, None] * time[None, :]) % (2 * seq_len)
    ang = prod.astype(F32) * (math.pi / seq_len)
    return jnp.cos(ang).astype(BF16), (-jnp.sin(ang)).astype(BF16)


def dft_matrices(seq_len):
    half = seq_len // 2
    lo = jnp.arange(half, dtype=jnp.int32)
    c_e, s_e = _cos_sin(lo, 2 * lo, seq_len)
    c_o, s_o = _cos_sin(lo, 2 * lo + 1, seq_len)
    c_ot, s_ot = _cos_sin(2 * lo + 1, lo, seq_len)
    return c_e, c_o, s_e, s_o, c_ot, s_ot


def hyena_filter_spectrum(seq_len, mats, w1, b1, w2, b2, w3):
    c_e, c_o, s_e, s_o = mats[:4]
    lags = jnp.concatenate([jnp.arange(0, seq_len, 2), jnp.arange(1, seq_len, 2)]).astype(F32)[:, None]
    t = lags / (seq_len - 1.0)
    w = 2.0 * math.pi * lags / seq_len
    f = jnp.linspace(1e-4, HY_BANDS - 1.0, HY_BANDS, dtype=F32)[None, :]
    feat = jnp.concatenate([t, jnp.cos(f * w), -jnp.sin(f * w)], axis=-1)
    emb, hid = w1.shape
    pe, ph = HY_LANES - emb, HY_LANES - hid
    feat = jnp.pad(feat, ((0, 0), (0, pe)))
    w1p = jnp.pad(w1, ((0, pe), (0, ph)))
    w2p = jnp.pad(w2, ((0, ph), (0, ph)))
    w3p = jnp.pad(w3, ((0, ph), (0, 0))).astype(BF16)
    b1p = jnp.pad(b1, (0, ph)).reshape(1, HY_LANES)
    b2p = jnp.pad(b2, (0, ph)).reshape(1, HY_LANES)
    max_decay = math.log(HY_DECAY_TARGET) / HY_SHORT_DECAY_PCT
    min_decay = math.log(HY_DECAY_TARGET) / HY_LONG_DECAY_PCT
    deltas = jnp.linspace(min_decay, max_decay, HY_WIDTH, dtype=F32).reshape(1, HY_WIDTH)
    tile = min(seq_len, 512)
    fs, fd, norm, mid_re, mid_im = filter_gen(feat, w1p, b1p, w2p, b2p, w3p, deltas, tile, 1024)
    kr = filter_dft(c_e, c_o, fs, norm, min(seq_len // 2, tile), 512)
    ki = filter_dft(s_e, s_o, fd, norm, min(seq_len // 2, tile), 512)
    kmid = jnp.stack([mid_re, mid_im], axis=0) / (norm + EPS) / seq_len
    kmid = jnp.transpose(kmid.reshape(2, HY_ORDER, HY_WIDTH), (1, 0, 2)).reshape(2 * HY_ORDER, HY_WIDTH)
    return kr, ki, kmid


def _hyena_body(v_ref, x1_ref, x2_ref, gate_ref, cw_ref, cb_ref, kr0_ref, ki0_ref, kr1_ref, ki1_ref,
                km_ref, sk_ref, ce_ref, co_ref, se_ref, so_ref, cot_ref, sot_ref, o_ref,
                z_scr, xg_scr, p_scr, *, seq_len):
    half = seq_len // 2
    cb = o_ref.shape[-1]
    row = lax.broadcasted_iota(jnp.int32, (half, 1), 0)
    alt = jnp.where(row % 2 == 0, 1.0, -1.0).astype(F32)

    def halves(ref):
        return (ref[:, 0].astype(F32).reshape(half, cb), ref[:, 1].astype(F32).reshape(half, cb))

    def short_conv(x_ref, which, dst):
        xe, xo = halves(x_ref)
        w0, w1, w2 = (cw_ref[j, which:which + 1, :] for j in range(HY_SHORT))
        bias = cb_ref[which:which + 1, :]
        xo_prev = jnp.where(row == 0, 0.0, pltpu.roll(xo, 1, 0))
        xe_next = jnp.where(row == half - 1, 0.0, pltpu.roll(xe, half - 1, 0))
        dst[0] = bias + xo_prev * w0 + xe * w1 + xo * w2
        dst[1] = bias + xe * w0 + xo * w1 + xe_next * w2

    rb = min(half, HY_ROW_BLOCK)
    short_conv(v_ref, 0, z_scr)
    for o, (xg_ref, kr_ref, ki_ref) in enumerate(((x1_ref, kr0_ref, ki0_ref), (x2_ref, kr1_ref, ki1_ref))):
        ze, zo = z_scr[0], z_scr[1]
        zeb, zob = ze.astype(BF16), zo.astype(BF16)
        zm_re = jnp.sum(ze * alt, axis=0, keepdims=True)
        zm_im = -jnp.sum(zo * alt, axis=0, keepdims=True)
        km_re, km_im = km_ref[2 * o:2 * o + 1, :], km_ref[2 * o + 1:2 * o + 2, :]
        pm_re = zm_re * km_re - zm_im * km_im
        pm_im = zm_re * km_im + zm_im * km_re
        for r in range(0, half, rb):
            a_e = jnp.dot(ce_ref[r:r + rb, :], zeb, preferred_element_type=F32)
            a_o = jnp.dot(co_ref[r:r + rb, :], zob, preferred_element_type=F32)
            b_e = jnp.dot(se_ref[r:r + rb, :], zeb, preferred_element_type=F32)
            b_o = jnp.dot(so_ref[r:r + rb, :], zob, preferred_element_type=F32)
            lo_re, lo_im, hi_re, hi_im = a_e + a_o, b_e + b_o, a_e - a_o, b_o - b_e
            kl_re, kl_im = kr_ref[r:r + rb, :], ki_ref[r:r + rb, :]
            kh_re, kh_im = kr_ref[half + r:half + r + rb, :], ki_ref[half + r:half + r + rb, :]
            pl_re, pl_im = lo_re * kl_re - lo_im * kl_im, lo_re * kl_im + lo_im * kl_re
            ph_re, ph_im = hi_re * kh_re - hi_im * kh_im, hi_re * kh_im + hi_im * kh_re
            p_scr[0, r:r + rb, :] = (pl_re + ph_re).astype(BF16)
            p_scr[1, r:r + rb, :] = (pl_im - ph_im).astype(BF16)
            p_scr[2, r:r + rb, :] = (pl_re - ph_re).astype(BF16)
            p_scr[3, r:r + rb, :] = (pl_im + ph_im).astype(BF16)
        short_conv(xg_ref, o + 1, xg_scr)
        sk = sk_ref[o:o + 1, :]
        for r in range(0, half, rb):
            y_e = (jnp.dot(ce_ref[r:r + rb, :], p_scr[0], preferred_element_type=F32)
                   + jnp.dot(se_ref[r:r + rb, :], p_scr[1], preferred_element_type=F32)
                   + alt[r:r + rb] * pm_re)
            y_o = (jnp.dot(cot_ref[r:r + rb, :], p_scr[2], preferred_element_type=F32)
                   + jnp.dot(sot_ref[r:r + rb, :], p_scr[3], preferred_element_type=F32)
                   - alt[r:r + rb] * pm_im)
            z_scr[0, r:r + rb, :] = xg_scr[0, r:r + rb, :] * (y_e + z_scr[0, r:r + rb, :] * sk)
            z_scr[1, r:r + rb, :] = xg_scr[1, r:r + rb, :] * (y_o + z_scr[1, r:r + rb, :] * sk)
    ge, go = halves(gate_ref)
    tiles = o_ref.shape[0]
    o_ref[:, 0] = (z_scr[0] * _silu(ge)).astype(o_ref.dtype).reshape(tiles, HY_PARITY_ROWS // 2, cb)
    o_ref[:, 1] = (z_scr[1] * _silu(go)).astype(o_ref.dtype).reshape(tiles, HY_PARITY_ROWS // 2, cb)


def hyena_core(proj5, conv_w, conv_b, kr, ki, kmid, skip, mats, cb):
    nseq, tiles, _, prow, _ = proj5.shape
    seq_len = tiles * 2 * prow
    half = seq_len // 2
    nj = HY_WIDTH // cb
    once = pl.Buffered(1)

    def act(k):
        return pl.BlockSpec((None, tiles, 2, prow, cb), lambda j, b: (b, 0, 0, 0, k * nj + j))

    def spec(k):
        return pl.BlockSpec((seq_len, cb), lambda j, b: (0, k * nj + j), pipeline_mode=once)

    dft = pl.BlockSpec((half, half), lambda j, b: (0, 0), pipeline_mode=once)
    body = functools.partial(_hyena_body, seq_len=seq_len)
    return pl.pallas_call(
        body,
        grid=(nj, nseq),
        in_specs=[act(0), act(1), act(2), act(3),
                  pl.BlockSpec((HY_SHORT, 3, cb), lambda j, b: (0, 0, j)),
                  pl.BlockSpec((3, cb), lambda j, b: (0, j)),
                  spec(0), spec(0), spec(1), spec(1),
                  pl.BlockSpec((2 * HY_ORDER, cb), lambda j, b: (0, j)),
                  pl.BlockSpec((HY_ORDER, cb), lambda j, b: (0, j))] + [dft] * 6,
        out_specs=pl.BlockSpec((None, tiles, 2, prow, cb), lambda j, b: (b, 0, 0, 0, j)),
        out_shape=jax.ShapeDtypeStruct((nseq, tiles, 2, prow, HY_WIDTH), BF16),
        scratch_shapes=[pltpu.VMEM((2, half, cb), F32), pltpu.VMEM((2, half, cb), F32),
                        pltpu.VMEM((4, half, cb), BF16)],
        compiler_params=_params("parallel", "arbitrary", vmem=VMEM_LIMIT_BYTES),
        name="hyena",
    )(proj5, proj5, proj5, proj5,
      conv_w.reshape(HY_SHORT, 3, HY_WIDTH), conv_b.reshape(3, HY_WIDTH),
      kr, ki, kr, ki, kmid, skip, *mats)


def parity_permutation():
    i = jnp.arange(HY_PARITY_ROWS)
    h = HY_PARITY_ROWS // 2
    src = jnp.where(i < h, 2 * i, 2 * (i - h) + 1)
    p = (src[:, None] == jnp.arange(HY_PARITY_ROWS)[None, :]).astype(BF16)
    return p, jnp.transpose(p)


def _inproj_parity_body(x_ref, g_ref, mod_ref, p_ref, w_ref, o_ref, h_scr):
    @pl.when(pl.program_id(1) == 0)
    def _():
        x = x_ref[...]
        y = x * lax.rsqrt(jnp.mean(x * x, axis=-1, keepdims=True) + EPS) * g_ref[...]
        h = (y * (1.0 + mod_ref[1:2, :]) + mod_ref[0:1, :]).astype(BF16)
        for r in range(0, x.shape[0], HY_PARITY_ROWS):
            h_scr[r:r + HY_PARITY_ROWS, :] = jnp.dot(p_ref[...], h[r:r + HY_PARITY_ROWS],
                                                     preferred_element_type=F32).astype(BF16)

    out = jnp.dot(h_scr[...], w_ref[...], preferred_element_type=F32).astype(o_ref.dtype)
    o_ref[...] = out.reshape(o_ref.shape)


def in_proj_parity(x, g, mod, p, w, rows_per_mod, tm, tn):
    n, nout = x.shape[0], w.shape[1]
    pr = HY_PARITY_ROWS
    return pl.pallas_call(
        _inproj_parity_body,
        grid=(n // tm, nout // tn),
        in_specs=[pl.BlockSpec((tm, D_MODEL), lambda i, j: (i, 0)),
                  pl.BlockSpec((1, D_MODEL), lambda i, j: (0, 0)),
                  pl.BlockSpec((None, 3, D_MODEL), lambda i, j: ((i * tm) // rows_per_mod, 0, 0)),
                  pl.BlockSpec((pr, pr), lambda i, j: (0, 0)),
                  pl.BlockSpec((D_MODEL, tn), lambda i, j: (0, j))],
        out_specs=pl.BlockSpec((tm // pr, 2, pr // 2, tn), lambda i, j: (i, 0, 0, j)),
        out_shape=jax.ShapeDtypeStruct((n // pr, 2, pr // 2, nout), BF16),
        scratch_shapes=[pltpu.VMEM((tm, D_MODEL), BF16)],
        compiler_params=_params("parallel", "arbitrary", vmem=VMEM_LIMIT_BYTES),
        name="in_proj_parity",
    )(x, g.reshape(1, D_MODEL), mod, p, w)


def _outproj_parity_body(y_ref, pt_ref, w_ref, x_ref, mod_ref, o_ref):
    tm = x_ref.shape[0]
    y = y_ref[...].reshape(tm, y_ref.shape[-1])
    parts = [jnp.dot(pt_ref[...], y[r:r + HY_PARITY_ROWS], preferred_element_type=F32).astype(BF16)
             for r in range(0, tm, HY_PARITY_ROWS)]
    out = jnp.dot(jnp.concatenate(parts, axis=0), w_ref[...], preferred_element_type=F32)
    o_ref[...] = x_ref[...] + mod_ref[2:3, :] * out


def out_proj_parity(y, pt, w, x, mod, rows_per_mod, tm):
    n = x.shape[0]
    width = y.shape[-1]
    pr = HY_PARITY_ROWS
    return pl.pallas_call(
        _outproj_parity_body,
        grid=(n // tm,),
        in_specs=[pl.BlockSpec((tm // pr, 2, pr // 2, width), lambda i: (i, 0, 0, 0)),
                  pl.BlockSpec((pr, pr), lambda i: (0, 0)),
                  pl.BlockSpec((width, D_MODEL), lambda i: (0, 0)),
                  pl.BlockSpec((tm, D_MODEL), lambda i: (i, 0)),
                  pl.BlockSpec((None, 3, D_MODEL), lambda i: ((i * tm) // rows_per_mod, 0, 0))],
        out_specs=pl.BlockSpec((tm, D_MODEL), lambda i: (i, 0)),
        out_shape=jax.ShapeDtypeStruct((n, D_MODEL), F32),
        compiler_params=_params("parallel", vmem=VMEM_LIMIT_BYTES),
        name="out_proj_parity",
    )(y, pt, w, x, mod)


def hyena_layer(x, g, mod, rows_per_mod, nseq, seq_len, w_in, conv_w, conv_b, filt_w, skip, w_out, perms, tm, cb):
    par_p, par_pt = perms
    pr = HY_PARITY_ROWS
    proj = in_proj_parity(x, g, mod, par_p, w_in, rows_per_mod, tm, PROJ_COLS)
    proj5 = proj.reshape(nseq, seq_len // pr, 2, pr // 2, 4 * HY_WIDTH)
    mats = dft_matrices(seq_len)
    kr, ki, kmid = hyena_filter_spectrum(seq_len, mats, *filt_w)
    y = hyena_core(proj5, conv_w, conv_b, kr, ki, kmid, skip, mats, cb)
    y = y.reshape(nseq * seq_len // pr, 2, pr // 2, HY_WIDTH)
    return out_proj_parity(y, par_pt, w_out, x, mod, rows_per_mod, min(tm, OUT_ROWS))


def kernel(x_prompt, x_sample, c, state_s5, state_ret, c_ctx, norm_g, mod_w, mod_b, s5_w_in, s5_lam_re, s5_lam_im, s5_log_step, s5_b_re, s5_b_im, s5_c_re, s5_c_im, s5_d, s5_w_glu, s5_b_glu, s5_w_out, ret_w_in, ret_decay_logit, ret_w_out, hy_w_in, hy_conv_w, hy_conv_b, hy_f_w1, hy_f_b1, hy_f_w2, hy_f_b2, hy_f_w3, hy_skip, hy_w_out, final_g):
    n_ctx, l_ctx, _ = x_prompt.shape
    n_dec, l_dec, _ = x_sample.shape
    xc = x_prompt.reshape(n_ctx * l_ctx, D_MODEL)
    xl = x_sample.reshape(n_dec * l_dec, D_MODEL)

    pad = (-(1 + n_dec)) % 8
    cvecs = jnp.concatenate([c_ctx[None, :], c, jnp.zeros((pad, D_MODEL), F32)], axis=0)
    mods = ada_mod_all(cvecs, mod_w, mod_b).reshape(DEPTH, -1, 3, D_MODEL)

    tm = 1024
    perm = s5_lane_permutation()
    last_is_s5 = (DEPTH - 1) % N_MIXERS == 0
    s5_new, ret_new = [], []
    for i in range(DEPTH):
        kind, j = i % N_MIXERS, i // N_MIXERS
        mod_c = mods[i, 0:1]
        mod_l = mods[i, 1:1 + n_dec]
        g = norm_g[i]
        if kind == 0:
            tables = s5_tables(s5_lam_re[j], s5_lam_im[j], s5_log_step[j], s5_b_re[j], s5_b_im[j],
                               s5_c_re[j], s5_c_im[j], s5_d[j])
            w = (s5_w_in[j].astype(BF16), tables, perm, s5_w_glu[j].astype(BF16), s5_b_glu[j],
                 s5_w_out[j].astype(BF16))
            fg = final_g if last_is_s5 and i == DEPTH - 1 else None
            xc, st = s5_layer(xc, g, mod_c, n_ctx, l_ctx, None, *w, nb=n_ctx, final_g=fg)
            xl, _ = s5_layer(xl, g, mod_l, n_dec, l_dec, state_s5[:, j], *w, nb=S5_SEQS_PER_STEP, final_g=fg)
            s5_new.append(st)
        elif kind == 1:
            w_in = ret_w_in[j]
            kscale = jnp.concatenate([jnp.ones((RET_QK,), F32), jnp.full((RET_QK,), RET_DK ** -0.5, F32),
                                      jnp.ones((2 * RET_V,), F32)])
            w_in = (w_in * kscale[None, :]).astype(BF16)
            lg = jax.nn.log_sigmoid(ret_decay_logit[j])
            lg = jnp.broadcast_to(jnp.transpose(lg)[:, :, None], (RET_HEADS, 2, 128))
            w = (w_in, lg, ret_w_out[j].astype(BF16), tm)
            xc, st = retention_layer(xc, g, mod_c, n_ctx * l_ctx, n_ctx, l_ctx, None, False, *w)
            xl, _ = retention_layer(xl, g, mod_l, l_dec, n_dec, l_dec, state_ret[:, j], True, *w)
            ret_new.append(st)
        else:
            filt_w = (hy_f_w1[j], hy_f_b1[j], hy_f_w2[j], hy_f_b2[j], hy_f_w3[j])
            w = (hy_w_in[j].astype(BF16), hy_conv_w[j], hy_conv_b[j], filt_w, hy_skip[j], hy_w_out[j].astype(BF16),
                 parity_permutation(), tm)
            xc = hyena_layer(xc, g, mod_c, n_ctx * l_ctx, n_ctx, l_ctx, *w, cb=1024)
            xl = hyena_layer(xl, g, mod_l, l_dec, n_dec, l_dec, *w, cb=256)

    if not last_is_s5:
        xc, xl = final_norm(xc, final_g, tm), final_norm(xl, final_g, tm)
    y_prompt = xc.reshape(n_ctx, l_ctx, D_MODEL)
    y_sample = xl.reshape(n_dec, l_dec, D_MODEL)
    new_state_s5 = jnp.stack(s5_new, axis=1)
    new_state_ret = jnp.stack(ret_new, axis=1)
    return (y_prompt, y_sample, new_state_s5, new_state_ret)
```

```python
import functools
import math

import jax
import jax.numpy as jnp
from jax import lax
from jax.experimental import pallas as pl
from jax.experimental.pallas import tpu as pltpu

F32 = jnp.float32
BF16 = jnp.bfloat16

D_MODEL = 1024
DEPTH = 4
N_MIXERS = 3
EPS = 1e-6
GRID_W = 64

S5_GROUP = 16
S5_GROUPS = D_MODEL // S5_GROUP
S5_STATE = 64
S5_CHUNK = 16
S5_TILE = S5_CHUNK * S5_GROUP
S5_SLAB = 128 // S5_GROUP
S5_SEQS_PER_STEP = 8
S5_ROWS = 512

RET_HEADS = 8
RET_QK = D_MODEL
RET_V = 2 * D_MODEL
RET_DK = RET_QK // RET_HEADS
RET_DV = RET_V // RET_HEADS
RET_BLOCK = 256
ROPE_BASE = 10000.0

HY_WIDTH = 2 * D_MODEL
HY_ORDER = 2
HY_SHORT = 3
HY_BANDS = 16
HY_SHORT_DECAY_PCT = 0.3
HY_LONG_DECAY_PCT = 1.5
HY_DECAY_TARGET = 1e-2
HY_LANES = 128
HY_PARITY_ROWS = 256
HY_ROW_BLOCK = 512

VMEM_LIMIT_BYTES = 56 * 1024 * 1024
PROJ_COLS = 2048
OUT_ROWS = 1024


def _params(*sem, vmem=None):
    return pltpu.CompilerParams(dimension_semantics=sem, vmem_limit_bytes=vmem)


def _silu(x):
    return x * jax.nn.sigmoid(x)


def _mod_body(c_ref, w_ref, b_ref, o_ref):
    a = _silu(c_ref[...]).astype(BF16)
    o_ref[0] = jnp.dot(a, w_ref[0].astype(BF16), preferred_element_type=F32) + b_ref[0]


def ada_mod_all(cvecs, mod_w, mod_b):
    r = cvecs.shape[0]
    tn = D_MODEL
    return pl.pallas_call(
        _mod_body,
        grid=(DEPTH, 3 * D_MODEL // tn),
        in_specs=[pl.BlockSpec((r, D_MODEL), lambda i, j: (0, 0)),
                  pl.BlockSpec((1, D_MODEL, tn), lambda i, j: (i, 0, j)),
                  pl.BlockSpec((1, 1, tn), lambda i, j: (i, 0, j))],
        out_specs=pl.BlockSpec((1, r, tn), lambda i, j: (i, 0, j)),
        out_shape=jax.ShapeDtypeStruct((DEPTH, r, 3 * D_MODEL), F32),
        compiler_params=_params("parallel", "parallel"),
        name="ada_mod",
    )(cvecs, mod_w, mod_b.reshape(DEPTH, 1, 3 * D_MODEL))


def _inproj_body(x_ref, g_ref, mod_ref, w_ref, o_ref, h_scr):
    @pl.when(pl.program_id(1) == 0)
    def _():
        x = x_ref[...]
        y = x * lax.rsqrt(jnp.mean(x * x, axis=-1, keepdims=True) + EPS) * g_ref[...]
        h_scr[...] = (y * (1.0 + mod_ref[1:2, :]) + mod_ref[0:1, :]).astype(BF16)

    o_ref[...] = jnp.dot(h_scr[...], w_ref[...], preferred_element_type=F32).astype(o_ref.dtype)


def in_proj(x, g, mod, w, rows_per_mod, tm, tn):
    n, nout = x.shape[0], w.shape[1]
    return pl.pallas_call(
        _inproj_body,
        grid=(n // tm, nout // tn),
        in_specs=[pl.BlockSpec((tm, D_MODEL), lambda i, j: (i, 0)),
                  pl.BlockSpec((1, D_MODEL), lambda i, j: (0, 0)),
                  pl.BlockSpec((None, 3, D_MODEL), lambda i, j: ((i * tm) // rows_per_mod, 0, 0)),
                  pl.BlockSpec((D_MODEL, tn), lambda i, j: (0, j))],
        out_specs=pl.BlockSpec((tm, tn), lambda i, j: (i, j)),
        out_shape=jax.ShapeDtypeStruct((n, nout), BF16),
        scratch_shapes=[pltpu.VMEM((tm, D_MODEL), BF16)],
        compiler_params=_params("parallel", "arbitrary", vmem=VMEM_LIMIT_BYTES),
        name="in_proj",
    )(x, g.reshape(1, D_MODEL), mod, w)


def _outproj_body(y_ref, w_ref, x_ref, mod_ref, o_ref):
    out = jnp.dot(y_ref[...], w_ref[...], preferred_element_type=F32)
    o_ref[...] = x_ref[...] + mod_ref[2:3, :] * out


def out_proj(y, w, x, mod, rows_per_mod, tm):
    n, width = y.shape
    return pl.pallas_call(
        _outproj_body,
        grid=(n // tm,),
        in_specs=[pl.BlockSpec((tm, width), lambda i: (i, 0)),
                  pl.BlockSpec((width, D_MODEL), lambda i: (0, 0)),
                  pl.BlockSpec((tm, D_MODEL), lambda i: (i, 0)),
                  pl.BlockSpec((None, 3, D_MODEL), lambda i: ((i * tm) // rows_per_mod, 0, 0))],
        out_specs=pl.BlockSpec((tm, D_MODEL), lambda i: (i, 0)),
        out_shape=jax.ShapeDtypeStruct((n, D_MODEL), F32),
        compiler_params=_params("parallel", vmem=VMEM_LIMIT_BYTES),
        name="out_proj",
    )(y, w, x, mod)


def _final_norm_body(x_ref, g_ref, o_ref):
    x = x_ref[...]
    o_ref[...] = x * lax.rsqrt(jnp.mean(x * x, axis=-1, keepdims=True) + EPS) * g_ref[...]


def final_norm(x, g, tm):
    n = x.shape[0]
    return pl.pallas_call(
        _final_norm_body,
        grid=(n // tm,),
        in_specs=[pl.BlockSpec((tm, D_MODEL), lambda i: (i, 0)),
                  pl.BlockSpec((1, D_MODEL), lambda i: (0, 0))],
        out_specs=pl.BlockSpec((tm, D_MODEL), lambda i: (i, 0)),
        out_shape=jax.ShapeDtypeStruct((n, D_MODEL), F32),
        compiler_params=_params("parallel"),
        name="final_norm",
    )(x, g.reshape(1, D_MODEL))


def s5_tables(lam_re, lam_im, log_step, b_re, b_im, c_re, c_im, d_skip):
    t_len = S5_CHUNK
    dt = jnp.exp(log_step)[..., None]
    ab_re = jnp.exp(lam_re * dt) * jnp.cos(lam_im * dt)
    ab_im = jnp.exp(lam_re * dt) * jnp.sin(lam_im * dt)
    den = lam_re * lam_re + lam_im * lam_im
    nr, ni = ab_re - 1.0, ab_im
    f_re = (nr * lam_re + ni * lam_im) / den
    f_im = (ni * lam_re - nr * lam_im) / den
    bb_re = f_re[..., None] * b_re - f_im[..., None] * b_im
    bb_im = f_re[..., None] * b_im + f_im[..., None] * b_re
    ks = jnp.arange(t_len + 1, dtype=F32)[:, None, None, None]
    pw_mag = jnp.exp(ks * (lam_re * dt)[None])
    pw_re = pw_mag * jnp.cos(ks * (lam_im * dt)[None])
    pw_im = pw_mag * jnp.sin(ks * (lam_im * dt)[None])

    npair = S5_GROUPS // 2
    pair_eye = jnp.eye(2, dtype=F32)

    def pair_pack(a):
        a = a.reshape(2, npair, 2, S5_GROUP, S5_STATE)
        a = a[:, :, :, :, None, :] * pair_eye[None, None, :, None, :, None]
        return jnp.transpose(a.reshape(2, npair, 2, S5_GROUP, 2 * S5_STATE), (1, 0, 2, 3, 4))

    bbt = jnp.stack([pair_pack(jnp.swapaxes(bb_re, 2, 3)), pair_pack(jnp.swapaxes(bb_im, 2, 3))], axis=2)
    cpk = jnp.stack([pair_pack(c_re), pair_pack(c_im)], axis=2)
    pw = jnp.stack([pw_re, pw_im], axis=0).reshape(2, t_len + 1, 2, npair, 2 * S5_STATE)
    pw = jnp.transpose(pw, (3, 2, 0, 1, 4))
    dmat = jnp.eye(S5_GROUP, dtype=F32) * d_skip.reshape(S5_GROUPS, 1, S5_GROUP)
    dmat = jnp.pad(dmat, ((0, 0), (0, 0), (0, S5_TILE - S5_GROUP))).reshape(npair, 2, S5_GROUP, S5_TILE)
    m, f, et = s5_table_kernel(bbt, cpk, pw, dmat)

    dec = jnp.stack([pw_re[t_len, 0], pw_im[t_len, 0], pw_re[t_len, 1], pw_im[t_len, 1]], axis=0)
    dec = jnp.transpose(dec.reshape(4, S5_GROUPS // S5_SLAB, S5_SLAB * S5_STATE), (1, 0, 2))
    return m.reshape(S5_GROUPS, S5_TILE, S5_TILE), f, et, dec


def _s5_table_body(bb_ref, c_ref, pw_ref, d_ref, m_ref, f_ref, et_ref):
    t_len = S5_CHUNK
    nt_dims = (((1,), (1,)), ((), ()))
    hp = lax.Precision.HIGHEST

    def times_power(ref, d, g, k):
        pr, pi = pw_ref[d, 0, k:k + 1, :], pw_ref[d, 1, k:k + 1, :]
        ar, ai = ref[d, 0, g], ref[d, 1, g]
        return ar * pr - ai * pi, ar * pi + ai * pr

    def stack(parts):
        return jnp.concatenate(parts, axis=0)

    for d in range(2):
        f_pow = [t_len - 1 - s for s in range(t_len)] if d == 0 else list(range(t_len))
        e_pow = [s + 1 for s in range(t_len)] if d == 0 else [t_len - s for s in range(t_len)]
        fr, fi, er, ei = [], [], [], []
        for g in range(2):
            for s in range(t_len):
                a, b = times_power(bb_ref, d, g, f_pow[s])
                fr.append(a)
                fi.append(b)
                a, b = times_power(c_ref, d, g, e_pow[s])
                er.append(a)
                ei.append(-b)
        lanes = 2 * S5_STATE
        f_ref[:, (2 * d) * lanes:(2 * d + 1) * lanes] = stack(fr).astype(f_ref.dtype)
        f_ref[:, (2 * d + 1) * lanes:(2 * d + 2) * lanes] = stack(fi).astype(f_ref.dtype)
        et_ref[:, (2 * d) * lanes:(2 * d + 1) * lanes] = stack(er).astype(et_ref.dtype)
        et_ref[:, (2 * d + 1) * lanes:(2 * d + 2) * lanes] = stack(ei).astype(et_ref.dtype)

    zeros = jnp.zeros((S5_GROUP, S5_TILE), F32)
    for g in range(2):
        def lag_kernels(d, powers):
            car, cai = zip(*(times_power(c_ref, d, g, k) for k in powers))
            return (lax.dot_general(bb_ref[d, 0, g], stack(car), nt_dims, precision=hp, preferred_element_type=F32)
                    - lax.dot_general(bb_ref[d, 1, g], stack(cai), nt_dims, precision=hp,
                                      preferred_element_type=F32))
        v_f = jnp.concatenate([zeros, lag_kernels(0, range(t_len)) + d_ref[g]], axis=1)
        v_b = jnp.concatenate([lag_kernels(1, range(t_len - 1, -1, -1)), zeros], axis=1)
        width = 2 * S5_TILE
        for s in range(t_len):
            blk_f = pltpu.roll(v_f, (width - (t_len - s) * S5_GROUP) % width, 1)[:, :S5_TILE]
            blk_b = pltpu.roll(v_b, (width - (t_len - 1 - s) * S5_GROUP) % width, 1)[:, :S5_TILE]
            m_ref[g, s * S5_GROUP:(s + 1) * S5_GROUP, :] = (blk_f + blk_b).astype(m_ref.dtype)


def s5_table_kernel(bbt, cpk, pw, dmat):
    npair = S5_GROUPS // 2
    pspec = pl.BlockSpec((None, 2, 2, 2, S5_GROUP, 2 * S5_STATE), lambda k: (k, 0, 0, 0, 0, 0))
    wide = pl.BlockSpec((None, 2 * S5_TILE, 2 * S5_TILE), lambda k: (k, 0, 0))
    return pl.pallas_call(
        _s5_table_body,
        grid=(npair,),
        in_specs=[pspec, pspec,
                  pl.BlockSpec((None, 2, 2, S5_CHUNK + 1, 2 * S5_STATE), lambda k: (k, 0, 0, 0, 0)),
                  pl.BlockSpec((None, 2, S5_GROUP, S5_TILE), lambda k: (k, 0, 0, 0))],
        out_specs=[pl.BlockSpec((None, 2, S5_TILE, S5_TILE), lambda k: (k, 0, 0, 0)), wide, wide],
        out_shape=[jax.ShapeDtypeStruct((npair, 2, S5_TILE, S5_TILE), BF16),
                   jax.ShapeDtypeStruct((npair, 2 * S5_TILE, 2 * S5_TILE), BF16),
                   jax.ShapeDtypeStruct((npair, 2 * S5_TILE, 2 * S5_TILE), BF16)],
        compiler_params=_params("parallel"),
        name="s5_tables",
    )(bbt, cpk, pw, dmat)


def s5_lane_permutation():
    width = S5_CHUNK * S5_SLAB * S5_GROUP
    i = jnp.arange(width)
    step, grp, ch = i // (S5_SLAB * S5_GROUP), (i // S5_GROUP) % S5_SLAB, i % S5_GROUP
    dst = grp * S5_TILE + step * S5_GROUP + ch
    return (dst[:, None] == jnp.arange(width)[None, :]).astype(BF16)


def _s5_scan_body(u_ref, q_ref, m_ref, f_ref, et_ref, dec_ref, h0_ref, y_ref, hf_ref,
                  uy_s, st_s, *, nchunk, nseq):
    npair = S5_SLAB // 2
    pair_w = 2 * S5_TILE
    lane_w = 2 * S5_STATE
    nt_dims = (((1,), (1,)), ((), ()))
    xcat = jnp.concatenate([u_ref[s] for s in range(S5_CHUNK)], axis=1)
    for k in range(npair):
        cols = slice(k * pair_w, (k + 1) * pair_w)
        up = jnp.dot(xcat, q_ref[:, cols], preferred_element_type=F32).astype(BF16)
        uy_s[:, cols] = up
        x = jnp.dot(up, f_ref[k], preferred_element_type=F32)
        for q in range(4):
            st_s[q * npair + k] = x[:, q * lane_w:(q + 1) * lane_w]

    def lanes_of(ref, q, k):
        return ref[q, :, k * lane_w:(k + 1) * lane_w]

    dec = [[dec_ref[q:q + 1, k * lane_w:(k + 1) * lane_w] for k in range(npair)] for q in range(4)]

    def step(i, carry):
        rows_f = pl.ds(pl.multiple_of(i * nseq, nseq), nseq)
        rows_b = pl.ds(pl.multiple_of((nchunk - 1 - i) * nseq, nseq), nseq)
        new = []
        for k in range(npair):
            fr, fi, br, bi = (carry[q * npair + k] for q in range(4))
            xr, xi = st_s[0 * npair + k, rows_f, :], st_s[1 * npair + k, rows_f, :]
            yr, yi = st_s[2 * npair + k, rows_b, :], st_s[3 * npair + k, rows_b, :]
            st_s[0 * npair + k, rows_f, :] = fr
            st_s[1 * npair + k, rows_f, :] = fi
            st_s[2 * npair + k, rows_b, :] = br
            st_s[3 * npair + k, rows_b, :] = bi
            new.append((dec[0][k] * fr - dec[1][k] * fi + xr, dec[0][k] * fi + dec[1][k] * fr + xi,
                        dec[2][k] * br - dec[3][k] * bi + yr, dec[2][k] * bi + dec[3][k] * br + yi))
        return tuple(new[k][q] for q in range(4) for k in range(npair))

    init = tuple(lanes_of(h0_ref, q, k) for q in range(4) for k in range(npair))
    final = lax.fori_loop(0, nchunk, step, init)
    for q in range(4):
        for k in range(npair):
            hf_ref[q, :, k * lane_w:(k + 1) * lane_w] = final[q * npair + k]

    for k in range(npair):
        h = jnp.concatenate([st_s[q * npair + k] for q in range(4)], axis=1)
        ye = lax.dot_general(h.astype(BF16), et_ref[k], nt_dims, preferred_element_type=F32)
        for g in range(2):
            cols = slice(k * pair_w + g * S5_TILE, k * pair_w + (g + 1) * S5_TILE)
            y = jnp.dot(uy_s[:, cols], m_ref[2 * k + g], preferred_element_type=F32)
            uy_s[:, cols] = (y + ye[:, g * S5_TILE:(g + 1) * S5_TILE]).astype(BF16)
    slab_w = S5_SLAB * S5_GROUP
    steps = pair_w // slab_w
    for t0 in range(0, S5_CHUNK, steps):
        yp = lax.dot_general(uy_s[...], q_ref[t0 * slab_w:(t0 + steps) * slab_w, :], nt_dims,
                             preferred_element_type=F32).astype(y_ref.dtype)
        for t in range(steps):
            y_ref[t0 + t] = yp[:, t * slab_w:(t + 1) * slab_w]


def s5_scan(u_s, perm, m, f, e, dec, h0, nchunk, nseq, nb):
    rows = nchunk * nb
    nslab = S5_GROUPS // S5_SLAB
    slab_w = S5_SLAB * S5_GROUP
    state_w = S5_SLAB * S5_STATE
    width = S5_CHUNK * slab_w
    body = functools.partial(_s5_scan_body, nchunk=nchunk, nseq=nb)
    hspec = pl.BlockSpec((None, None, 4, nb, state_w), lambda s, b: (s, b, 0, 0, 0))
    state = pltpu.VMEM((2 * S5_SLAB, rows, 2 * S5_STATE), F32)
    once = pl.Buffered(1)
    return pl.pallas_call(
        body,
        grid=(nslab, nseq // nb),
        in_specs=[pl.BlockSpec((S5_CHUNK, rows, slab_w), lambda s, b: (0, b, s)),
                  pl.BlockSpec((width, width), lambda s, b: (0, 0), pipeline_mode=once),
                  pl.BlockSpec((S5_SLAB, S5_TILE, S5_TILE), lambda s, b: (s, 0, 0)),
                  pl.BlockSpec((S5_SLAB // 2, 2 * S5_TILE, 2 * S5_TILE), lambda s, b: (s, 0, 0)),
                  pl.BlockSpec((S5_SLAB // 2, 2 * S5_TILE, 2 * S5_TILE), lambda s, b: (s, 0, 0)),
                  pl.BlockSpec((None, 4, state_w), lambda s, b: (s, 0, 0)),
                  hspec],
        out_specs=[pl.BlockSpec((S5_CHUNK, rows, slab_w), lambda s, b: (0, b, s)), hspec],
        out_shape=[jax.ShapeDtypeStruct((S5_CHUNK, nseq * nchunk, D_MODEL), BF16),
                   jax.ShapeDtypeStruct((nslab, nseq // nb, 4, nb, state_w), F32)],
        scratch_shapes=[pltpu.VMEM((rows, width), BF16), state],
        compiler_params=_params("parallel", "parallel", vmem=VMEM_LIMIT_BYTES),
        name="s5_scan",
    )(u_s, perm, m, f, e, dec, h0)


def _s5_half_tile(nb, rows):
    by_chunk = rows >= 2 * S5_CHUNK
    return (nb, rows // 2, True) if by_chunk else (nb // 2, rows, False)


def s5_row_permutation(nb):
    nbh, rows_h, _ = _s5_half_tile(nb, S5_ROWS // nb)
    cph = rows_h // S5_CHUNK
    n = S5_ROWS // 2
    i = jnp.arange(n)
    step, chunk, seq = i // (cph * nbh), (i // nbh) % cph, i % nbh
    src = seq * (cph * S5_CHUNK) + chunk * S5_CHUNK + step
    p = (src[:, None] == jnp.arange(n)[None, :]).astype(BF16)
    return p, jnp.transpose(p)


def _s5_half(ref, h):
    nb, rows, _ = ref.shape
    nbh, rows_h, by_chunk = _s5_half_tile(nb, rows)
    if by_chunk:
        return (slice(None), slice(h * rows_h, (h + 1) * rows_h), slice(None))
    return (slice(h * nbh, (h + 1) * nbh), slice(None), slice(None))


def _s5_tile_specs(x3, mod, nb):
    nseq, seq_len, _ = x3.shape
    cpt = S5_ROWS // (S5_CHUNK * nb)
    ctiles = seq_len // (cpt * S5_CHUNK)
    grid = (nseq // nb, ctiles)
    xspec = pl.BlockSpec((nb, cpt * S5_CHUNK, D_MODEL), lambda b, c: (b, c, 0))
    if mod.shape[0] == 1:
        mspec = pl.BlockSpec((1, 3, D_MODEL), lambda b, c: (0, 0, 0))
    else:
        mspec = pl.BlockSpec((nb, 3, D_MODEL), lambda b, c: (b, 0, 0))

    def stepped(width, col):
        return pl.BlockSpec((S5_CHUNK, S5_ROWS // S5_CHUNK, width), lambda b, c: (0, b * ctiles + c, col))

    return grid, xspec, mspec, stepped


def _inproj_steps_body(x_ref, g_ref, mod_ref, p_ref, w_ref, o_ref):
    x = x_ref[...]
    y = x * lax.rsqrt(jnp.mean(x * x, axis=-1, keepdims=True) + EPS) * g_ref[...]
    hmod = (y * (1.0 + mod_ref[:, 1:2, :]) + mod_ref[:, 0:1, :]).astype(BF16)
    half = S5_ROWS // 2
    parts = [jnp.dot(p_ref[...], hmod[_s5_half(x_ref, h)].reshape(half, D_MODEL),
                     preferred_element_type=F32).astype(BF16) for h in range(2)]
    out = jnp.dot(jnp.concatenate(parts, axis=0), w_ref[...], preferred_element_type=F32).astype(o_ref.dtype)
    hr = o_ref.shape[1] // 2
    for h in range(2):
        o_ref[:, h * hr:(h + 1) * hr, :] = out[h * half:(h + 1) * half].reshape(S5_CHUNK, hr, o_ref.shape[2])


def in_proj_steps(x3, g, mod, p, w, nb):
    nseq, seq_len, _ = x3.shape
    nout = w.shape[1]
    grid, xspec, mspec, stepped = _s5_tile_specs(x3, mod, nb)
    return pl.pallas_call(
        _inproj_steps_body,
        grid=grid,
        in_specs=[xspec,
                  pl.BlockSpec((1, D_MODEL), lambda b, c: (0, 0)),
                  mspec,
                  pl.BlockSpec((S5_ROWS // 2, S5_ROWS // 2), lambda b, c: (0, 0)),
                  pl.BlockSpec((D_MODEL, nout), lambda b, c: (0, 0))],
        out_specs=stepped(nout, 0),
        out_shape=jax.ShapeDtypeStruct((S5_CHUNK, nseq * seq_len // S5_CHUNK, nout), BF16),
        compiler_params=_params("parallel", "parallel", vmem=VMEM_LIMIT_BYTES),
        name="in_proj_steps",
    )(x3, g.reshape(1, D_MODEL), mod, p, w)


def _s5_out_body(y_ref, gate_ref, wg_ref, bg_ref, wo_ref, pt_ref, x_ref, mod_ref, *rest):
    o_ref = rest[-1]
    half = S5_ROWS // 2
    hr = y_ref.shape[1] // 2
    g = jax.nn.gelu(y_ref[...].astype(F32).reshape(S5_ROWS, D_MODEL))
    t = jnp.dot(g.astype(BF16), wg_ref[...], preferred_element_type=F32) + bg_ref[...]
    z = g * jax.nn.sigmoid(t) * _silu(gate_ref[...].astype(F32).reshape(S5_ROWS, D_MODEL))
    z = z.astype(BF16).reshape(S5_CHUNK, 2 * hr, D_MODEL)
    parts = [jnp.dot(pt_ref[...], z[:, h * hr:(h + 1) * hr, :].reshape(half, D_MODEL),
                     preferred_element_type=F32).astype(BF16) for h in range(2)]
    out = jnp.dot(jnp.concatenate(parts, axis=0), wo_ref[...], preferred_element_type=F32)
    nb = x_ref.shape[0]
    nbh, _, by_chunk = _s5_half_tile(nb, x_ref.shape[1])
    for h in range(2):
        idx = _s5_half(x_ref, h)
        per_seq_half = not by_chunk and mod_ref.shape[0] == nb
        gate_mod = mod_ref[h * nbh:(h + 1) * nbh, 2:3, :] if per_seq_half else mod_ref[:, 2:3, :]
        xh = x_ref[idx]
        x = xh + gate_mod * out[h * half:(h + 1) * half].reshape(xh.shape)
        if len(rest) == 2:
            x = x * lax.rsqrt(jnp.mean(x * x, axis=-1, keepdims=True) + EPS) * rest[0][...]
        o_ref[idx] = x


def s5_out(y_s, u_s, w_glu, b_glu, w_out, pt, x3, mod, nb, final_g=None):
    grid, xspec, mspec, stepped = _s5_tile_specs(x3, mod, nb)
    wspec = pl.BlockSpec((D_MODEL, D_MODEL), lambda b, c: (0, 0))
    vspec = pl.BlockSpec((1, D_MODEL), lambda b, c: (0, 0))
    in_specs = [stepped(D_MODEL, 0), stepped(D_MODEL, 1), wspec, vspec, wspec,
                pl.BlockSpec((S5_ROWS // 2, S5_ROWS // 2), lambda b, c: (0, 0)), xspec, mspec]
    args = [y_s, u_s, w_glu, b_glu.reshape(1, D_MODEL), w_out, pt, x3, mod]
    if final_g is not None:
        in_specs.append(vspec)
        args.append(final_g.reshape(1, D_MODEL))
    return pl.pallas_call(
        _s5_out_body,
        grid=grid,
        in_specs=in_specs,
        out_specs=xspec,
        out_shape=jax.ShapeDtypeStruct(x3.shape, F32),
        compiler_params=_params("parallel", "parallel", vmem=VMEM_LIMIT_BYTES),
        name="s5_out",
    )(*args)


def s5_layer(x, g, mod, nseq, seq_len, h0, w_in, tables, perm, w_glu, b_glu, w_out, nb, final_g=None):
    nchunk = seq_len // S5_CHUNK
    nslab = S5_GROUPS // S5_SLAB
    state_w = S5_SLAB * S5_STATE
    row_p, row_pt = s5_row_permutation(nb)
    x3 = x.reshape(nseq, seq_len, D_MODEL)
    u_s = in_proj_steps(x3, g, mod, row_p, w_in, nb)
    if h0 is None:
        h0_p = jnp.zeros((nslab, nseq // nb, 4, nb, state_w), F32)
    else:
        h0_p = jnp.transpose(h0.reshape(nseq // nb, nb, 4, nslab, state_w), (3, 0, 2, 1, 4))
    m, f, e, dec = tables
    y_s, hf = s5_scan(u_s, perm, m, f, e, dec, h0_p, nchunk, nseq, nb)
    x_new = s5_out(y_s, u_s, w_glu, b_glu, w_out, row_pt, x3, mod, nb, final_g).reshape(nseq * seq_len, D_MODEL)
    states = jnp.transpose(hf, (1, 3, 2, 0, 4)).reshape(nseq, 2, 2, S5_GROUPS, S5_STATE)
    return x_new, states


def _ret_body(*refs, seq_len, rope, has_init, want_state):
    refs = list(refs)
    q_ref, k_ref, v_ref, gate_ref, lg_ref = refs[:5]
    pos = 5
    if rope:
        cos_ref, sin_ref = refs[pos:pos + 2]
        pos += 2
    if has_init:
        s0_ref = refs[pos]
        pos += 1
    o_ref = refs[pos]
    pos += 1
    if want_state:
        sfin_ref = refs[pos]
        pos += 1
    sf_scr = refs[pos]

    c_len = RET_BLOCK
    nblk = seq_len // c_len
    ii = lax.broadcasted_iota(jnp.int32, (c_len, c_len), 0)
    jj = lax.broadcasted_iota(jnp.int32, (c_len, c_len), 1)
    diff = (ii - jj).astype(F32)
    ic = lax.broadcasted_iota(jnp.int32, (c_len, 1), 0).astype(F32)
    for hh in range(lg_ref.shape[0]):
        _ret_head(hh, q_ref, k_ref, v_ref, gate_ref, lg_ref,
                  (cos_ref, sin_ref) if rope else None, s0_ref if has_init else None,
                  o_ref, sfin_ref if want_state else None, sf_scr, diff, ic, seq_len)


def _ret_head(hh, q_ref, k_ref, v_ref, gate_ref, lg_ref, rope_refs, s0_ref, o_ref, sfin_ref, sf_scr,
              diff, ic, seq_len):
    rope, has_init, want_state = rope_refs is not None, s0_ref is not None, sfin_ref is not None
    c_len = RET_BLOCK
    nblk = seq_len // c_len
    qk_cols = slice(hh * RET_DK, (hh + 1) * RET_DK)
    v_cols = slice(hh * RET_DV, (hh + 1) * RET_DV)
    lgf = lg_ref[hh, 0:1, 0:1]
    lgb = lg_ref[hh, 1:2, 0:1]
    decay = jnp.exp(jnp.abs(diff) * jnp.where(diff >= 0, lgf, lgb))
    q_dec_f = jnp.exp((ic + 1.0) * lgf)
    q_dec_b = jnp.exp((c_len - ic) * lgb)
    k_dec_f = jnp.exp((c_len - 1.0 - ic) * lgf)
    k_dec_b = jnp.exp(ic * lgb)
    c_dec_f = jnp.exp(c_len * lgf)
    c_dec_b = jnp.exp(c_len * lgb)

    q = q_ref[:, qk_cols].astype(F32)
    k = k_ref[:, qk_cols].astype(F32)
    if rope:
        cos_ref, sin_ref = rope_refs
        lane = lax.broadcasted_iota(jnp.int32, (seq_len, RET_DK), 1)
        first = (lane % (RET_DK // 2)) < (RET_DK // 4)
        cos = cos_ref[...]
        sin = sin_ref[...]

        def rot(x):
            swapped = jnp.where(first, pltpu.roll(x, RET_DK - RET_DK // 4, 1), pltpu.roll(x, RET_DK // 4, 1))
            return x * cos + swapped * sin

        q = rot(q)
        k = rot(k)

    def blk(a, c):
        return a[c * c_len:(c + 1) * c_len]

    def kv_state(kd, c):
        kt = jnp.transpose(blk(k, c) * kd).astype(BF16)
        return jnp.dot(kt, v_ref[c * c_len:(c + 1) * c_len, v_cols], preferred_element_type=F32)

    use_state = has_init or nblk > 1
    s_f = s0_ref[0, hh] if has_init else jnp.zeros((RET_DK, RET_DV), F32)
    for c in range(nblk):
        if use_state:
            sf_scr[c] = s_f
        if want_state or c < nblk - 1:
            s_f = c_dec_f * s_f + kv_state(k_dec_f, c)
    if want_state:
        sfin_ref[0, hh] = s_f

    s_b = s0_ref[1, hh] if has_init else jnp.zeros((RET_DK, RET_DV), F32)
    for c in range(nblk - 1, -1, -1):
        qc = blk(q, c)
        kc = blk(k, c)
        vc = v_ref[c * c_len:(c + 1) * c_len, v_cols]
        s = lax.dot_general(qc.astype(BF16), kc.astype(BF16), (((1,), (1,)), ((), ())),
                            preferred_element_type=F32)
        o = jnp.dot((s * decay).astype(BF16), vc, preferred_element_type=F32)
        if use_state:
            o = o + jnp.dot((qc * q_dec_f).astype(BF16), sf_scr[c].astype(BF16), preferred_element_type=F32)
            o = o + jnp.dot((qc * q_dec_b).astype(BF16), s_b.astype(BF16), preferred_element_type=F32)
        o = o * lax.rsqrt(jnp.mean(o * o, axis=-1, keepdims=True) + EPS)
        gate = gate_ref[c * c_len:(c + 1) * c_len, v_cols].astype(F32)
        o_ref[c * c_len:(c + 1) * c_len, v_cols] = (o * _silu(gate)).astype(o_ref.dtype)
        if want_state or c > 0:
            s_b = c_dec_b * s_b + kv_state(k_dec_b, c)
    if want_state:
        sfin_ref[1, hh] = s_b


def retention_core(proj3, lg, rope_tabs, s0, want_state, hp):
    nseq, seq_len, _ = proj3.shape
    nblk = seq_len // RET_BLOCK
    rope = rope_tabs is not None
    has_init = s0 is not None
    qw, vw = hp * RET_DK, hp * RET_DV
    in_specs = [pl.BlockSpec((None, seq_len, qw), lambda b, h: (b, 0, h)),
                pl.BlockSpec((None, seq_len, qw), lambda b, h: (b, 0, RET_QK // qw + h)),
                pl.BlockSpec((None, seq_len, vw), lambda b, h: (b, 0, 2 * RET_QK // vw + h)),
                pl.BlockSpec((None, seq_len, vw), lambda b, h: (b, 0, (2 * RET_QK + RET_V) // vw + h)),
                pl.BlockSpec((hp, 2, 128), lambda b, h: (h, 0, 0))]
    args = [proj3, proj3, proj3, proj3, lg]
    if rope:
        tab = pl.BlockSpec((seq_len, RET_DK), lambda b, h: (0, 0))
        in_specs += [tab, tab]
        args += list(rope_tabs)
    sspec = pl.BlockSpec((None, 2, hp, RET_DK, RET_DV), lambda b, h: (b, 0, h, 0, 0))
    if has_init:
        in_specs.append(sspec)
        args.append(s0)
    out_specs = [pl.BlockSpec((None, seq_len, vw), lambda b, h: (b, 0, h))]
    out_shape = [jax.ShapeDtypeStruct((nseq, seq_len, RET_V), BF16)]
    if want_state:
        out_specs.append(sspec)
        out_shape.append(jax.ShapeDtypeStruct((nseq, 2, RET_HEADS, RET_DK, RET_DV), F32))
    body = functools.partial(_ret_body, seq_len=seq_len, rope=rope, has_init=has_init, want_state=want_state)
    res = pl.pallas_call(
        body,
        grid=(nseq, RET_HEADS // hp),
        in_specs=in_specs,
        out_specs=out_specs,
        out_shape=out_shape,
        scratch_shapes=[pltpu.VMEM((nblk, RET_DK, RET_DV), F32)],
        compiler_params=_params("parallel", "parallel", vmem=VMEM_LIMIT_BYTES),
        name="retention",
    )(*args)
    return res[0], (res[1] if want_state else None)


def rope_tables(seq_len):
    t = jnp.arange(seq_len)
    half = RET_DK // 2
    nfreq = half // 2
    inv = ROPE_BASE ** (-jnp.arange(nfreq, dtype=F32) / nfreq)
    parts_c, parts_s = [], []
    for p in (t // GRID_W, t % GRID_W):
        ang = p.astype(F32)[:, None] * inv[None, :]
        parts_c += [jnp.cos(ang), jnp.cos(ang)]
        parts_s += [-jnp.sin(ang), jnp.sin(ang)]
    return jnp.concatenate(parts_c, axis=-1), jnp.concatenate(parts_s, axis=-1)


def retention_layer(x, g, mod, rows_per_mod, nseq, seq_len, s0, grid_pos, w_in, lg, w_out, tm):
    proj = in_proj(x, g, mod, w_in, rows_per_mod, tm, PROJ_COLS)
    proj3 = proj.reshape(nseq, seq_len, 2 * RET_QK + 2 * RET_V)
    tabs = rope_tables(seq_len) if grid_pos else None
    hp = RET_HEADS if seq_len <= RET_BLOCK else 1
    y, states = retention_core(proj3, lg, tabs, s0, want_state=s0 is None, hp=hp)
    x_new = out_proj(y.reshape(nseq * seq_len, RET_V), w_out, x, mod, rows_per_mod, min(tm, OUT_ROWS))
    return x_new, states


def _filter_dft_body(ae_ref, ao_ref, b_ref, norm_ref, o_ref, *, seq_len, tm):
    half = seq_len // 2
    row = pl.program_id(0) * tm + lax.broadcasted_iota(jnp.int32, (tm, 1), 0)
    wf = jnp.where(row % half == 0, 1.0, 2.0) * (1.0 / (2.0 * seq_len))
    acc = (jnp.dot(ae_ref[...], b_ref[0:half, :], preferred_element_type=F32)
           + jnp.dot(ao_ref[...], b_ref[half:, :], preferred_element_type=F32))
    o_ref[...] = acc * wf / (norm_ref[...] + EPS)


def filter_dft(a_even, a_odd, b, norm, tm, tn):
    half = a_even.shape[0]
    n = b.shape[2]
    nh = half // tm
    body = functools.partial(_filter_dft_body, seq_len=2 * half, tm=tm)
    aspec = pl.BlockSpec((tm, half), lambda i, j: (i % nh, 0))
    return pl.pallas_call(
        body,
        grid=(2 * nh, n // tn),
        in_specs=[aspec, aspec,
                  pl.BlockSpec((None, 2 * half, tn), lambda i, j: (i // nh, 0, j)),
                  pl.BlockSpec((1, tn), lambda i, j: (0, j))],
        out_specs=pl.BlockSpec((tm, tn), lambda i, j: (i, j)),
        out_shape=jax.ShapeDtypeStruct((2 * half, n), F32),
        compiler_params=_params("parallel", "parallel", vmem=VMEM_LIMIT_BYTES),
        name="filter_dft",
    )(a_even, a_odd, b, norm)


def _filter_gen_body(feat_ref, w1_ref, b1_ref, w2_ref, b2_ref, w3f_ref, w3b_ref, dl_ref,
                     fs_ref, fd_ref, norm_ref, mre_ref, mim_ref, *, seq_len, tr):
    r = pl.program_id(1)
    hp = lax.Precision.HIGHEST
    z = jnp.sin(jnp.dot(feat_ref[...], w1_ref[...], precision=hp, preferred_element_type=F32) + b1_ref[...])
    z = jnp.sin(jnp.dot(z, w2_ref[...], precision=hp, preferred_element_type=F32) + b2_ref[...]).astype(BF16)
    row = r * tr + lax.broadcasted_iota(jnp.int32, (tr, 1), 0)
    lag = 2 * (row % (seq_len // 2)) + row // (seq_len // 2)
    t = lag.astype(F32) * (1.0 / (seq_len - 1))
    win = jnp.exp(-t * jnp.abs(dl_ref[...]))
    ff = jnp.dot(z, w3f_ref[...], preferred_element_type=F32) * win
    fb = jnp.dot(z, w3b_ref[...], preferred_element_type=F32) * win
    part = jnp.sum(jnp.abs(ff) + jnp.abs(fb), axis=0, keepdims=True)
    fb = jnp.where(lag == 0, 0.0, fb)
    fs = ff + fb
    fd = ff - fb
    quarter = jnp.where(lag % 4 < 2, 1.0, -1.0)
    cos_q = jnp.where(lag % 2 == 0, quarter, 0.0)
    sin_q = jnp.where(lag % 2 == 1, quarter, 0.0)

    @pl.when(r == 0)
    def _():
        norm_ref[...] = jnp.zeros_like(norm_ref)
        mre_ref[...] = jnp.zeros_like(mre_ref)
        mim_ref[...] = jnp.zeros_like(mim_ref)

    norm_ref[...] += part
    mre_ref[...] += jnp.sum(cos_q * fs, axis=0, keepdims=True)
    mim_ref[...] -= jnp.sum(sin_q * fd, axis=0, keepdims=True)
    alt = jnp.where(lag % 2 == 0, 1.0, -1.0)
    fs_ref[0] = fs.astype(fs_ref.dtype)
    fs_ref[1] = (fs * alt).astype(fs_ref.dtype)
    fd_ref[0] = fd.astype(fd_ref.dtype)
    fd_ref[1] = (-(fd * alt)).astype(fd_ref.dtype)


def filter_gen(feat, w1, b1, w2, b2, w3, deltas, tr, tn):
    seq_len, kf = feat.shape
    hid = w1.shape[1]
    ncol = HY_ORDER * HY_WIDTH
    nj = ncol // tn
    body = functools.partial(_filter_gen_body, seq_len=seq_len, tr=tr)
    full = lambda shape: pl.BlockSpec(shape, lambda j, r: (0, 0))
    row = pl.BlockSpec((1, tn), lambda j, r: (0, j))
    return pl.pallas_call(
        body,
        grid=(nj, seq_len // tr),
        in_specs=[pl.BlockSpec((tr, kf), lambda j, r: (r, 0)),
                  full((kf, hid)), full((1, hid)), full((hid, hid)), full((1, hid)),
                  pl.BlockSpec((hid, tn), lambda j, r: (0, j)),
                  pl.BlockSpec((hid, tn), lambda j, r: (0, nj + j)),
                  pl.BlockSpec((1, tn), lambda j, r: (0, j % (HY_WIDTH // tn)))],
        out_specs=[pl.BlockSpec((2, tr, tn), lambda j, r: (0, r, j)),
                   pl.BlockSpec((2, tr, tn), lambda j, r: (0, r, j)), row, row, row],
        out_shape=[jax.ShapeDtypeStruct((2, seq_len, ncol), BF16), jax.ShapeDtypeStruct((2, seq_len, ncol), BF16)]
        + [jax.ShapeDtypeStruct((1, ncol), F32)] * 3,
        compiler_params=_params("parallel", "arbitrary", vmem=VMEM_LIMIT_BYTES),
        name="filter_gen",
    )(feat, w1, b1, w2, b2, w3, w3, deltas)


def _cos_sin(freq, time, seq_len):
    prod = (freq[:, None] * time[None, :]) % (2 * seq_len)
    ang = prod.astype(F32) * (math.pi / seq_len)
    return jnp.cos(ang).astype(BF16), (-jnp.sin(ang)).astype(BF16)


def dft_matrices(seq_len):
    half = seq_len // 2
    lo = jnp.arange(half, dtype=jnp.int32)
    c_e, s_e = _cos_sin(lo, 2 * lo, seq_len)
    c_o, s_o = _cos_sin(lo, 2 * lo + 1, seq_len)
    c_ot, s_ot = _cos_sin(2 * lo + 1, lo, seq_len)
    return c_e, c_o, s_e, s_o, c_ot, s_ot


def hyena_filter_spectrum(seq_len, mats, w1, b1, w2, b2, w3):
    c_e, c_o, s_e, s_o = mats[:4]
    lags = jnp.concatenate([jnp.arange(0, seq_len, 2), jnp.arange(1, seq_len, 2)]).astype(F32)[:, None]
    t = lags / (seq_len - 1.0)
    w = 2.0 * math.pi * lags / seq_len
    f = jnp.linspace(1e-4, HY_BANDS - 1.0, HY_BANDS, dtype=F32)[None, :]
    feat = jnp.concatenate([t, jnp.cos(f * w), -jnp.sin(f * w)], axis=-1)
    emb, hid = w1.shape
    pe, ph = HY_LANES - emb, HY_LANES - hid
    feat = jnp.pad(feat, ((0, 0), (0, pe)))
    w1p = jnp.pad(w1, ((0, pe), (0, ph)))
    w2p = jnp.pad(w2, ((0, ph), (0, ph)))
    w3p = jnp.pad(w3, ((0, ph), (0, 0))).astype(BF16)
    b1p = jnp.pad(b1, (0, ph)).reshape(1, HY_LANES)
    b2p = jnp.pad(b2, (0, ph)).reshape(1, HY_LANES)
    max_decay = math.log(HY_DECAY_TARGET) / HY_SHORT_DECAY_PCT
    min_decay = math.log(HY_DECAY_TARGET) / HY_LONG_DECAY_PCT
    deltas = jnp.linspace(min_decay, max_decay, HY_WIDTH, dtype=F32).reshape(1, HY_WIDTH)
    tile = min(seq_len, 512)
    fs, fd, norm, mid_re, mid_im = filter_gen(feat, w1p, b1p, w2p, b2p, w3p, deltas, tile, 1024)
    kr = filter_dft(c_e, c_o, fs, norm, min(seq_len // 2, tile), 512)
    ki = filter_dft(s_e, s_o, fd, norm, min(seq_len // 2, tile), 512)
    kmid = jnp.stack([mid_re, mid_im], axis=0) / (norm + EPS) / seq_len
    kmid = jnp.transpose(kmid.reshape(2, HY_ORDER, HY_WIDTH), (1, 0, 2)).reshape(2 * HY_ORDER, HY_WIDTH)
    return kr, ki, kmid


def _hyena_body(v_ref, x1_ref, x2_ref, gate_ref, cw_ref, cb_ref, kr0_ref, ki0_ref, kr1_ref, ki1_ref,
                km_ref, sk_ref, ce_ref, co_ref, se_ref, so_ref, cot_ref, sot_ref, o_ref,
                z_scr, xg_scr, p_scr, *, seq_len):
    half = seq_len // 2
    cb = o_ref.shape[-1]
    row = lax.broadcasted_iota(jnp.int32, (half, 1), 0)
    alt = jnp.where(row % 2 == 0, 1.0, -1.0).astype(F32)

    def halves(ref):
        return (ref[:, 0].astype(F32).reshape(half, cb), ref[:, 1].astype(F32).reshape(half, cb))

    def short_conv(x_ref, which, dst):
        xe, xo = halves(x_ref)
        w0, w1, w2 = (cw_ref[j, which:which + 1, :] for j in range(HY_SHORT))
        bias = cb_ref[which:which + 1, :]
        xo_prev = jnp.where(row == 0, 0.0, pltpu.roll(xo, 1, 0))
        xe_next = jnp.where(row == half - 1, 0.0, pltpu.roll(xe, half - 1, 0))
        dst[0] = bias + xo_prev * w0 + xe * w1 + xo * w2
        dst[1] = bias + xe * w0 + xo * w1 + xe_next * w2

    rb = min(half, HY_ROW_BLOCK)
    short_conv(v_ref, 0, z_scr)
    for o, (xg_ref, kr_ref, ki_ref) in enumerate(((x1_ref, kr0_ref, ki0_ref), (x2_ref, kr1_ref, ki1_ref))):
        ze, zo = z_scr[0], z_scr[1]
        zeb, zob = ze.astype(BF16), zo.astype(BF16)
        zm_re = jnp.sum(ze * alt, axis=0, keepdims=True)
        zm_im = -jnp.sum(zo * alt, axis=0, keepdims=True)
        km_re, km_im = km_ref[2 * o:2 * o + 1, :], km_ref[2 * o + 1:2 * o + 2, :]
        pm_re = zm_re * km_re - zm_im * km_im
        pm_im = zm_re * km_im + zm_im * km_re
        for r in range(0, half, rb):
            a_e = jnp.dot(ce_ref[r:r + rb, :], zeb, preferred_element_type=F32)
            a_o = jnp.dot(co_ref[r:r + rb, :], zob, preferred_element_type=F32)
            b_e = jnp.dot(se_ref[r:r + rb, :], zeb, preferred_element_type=F32)
            b_o = jnp.dot(so_ref[r:r + rb, :], zob, preferred_element_type=F32)
            lo_re, lo_im, hi_re, hi_im = a_e + a_o, b_e + b_o, a_e - a_o, b_o - b_e
            kl_re, kl_im = kr_ref[r:r + rb, :], ki_ref[r:r + rb, :]
            kh_re, kh_im = kr_ref[half + r:half + r + rb, :], ki_ref[half + r:half + r + rb, :]
            pl_re, pl_im = lo_re * kl_re - lo_im * kl_im, lo_re * kl_im + lo_im * kl_re
            ph_re, ph_im = hi_re * kh_re - hi_im * kh_im, hi_re * kh_im + hi_im * kh_re
            p_scr[0, r:r + rb, :] = (pl_re + ph_re).astype(BF16)
            p_scr[1, r:r + rb, :] = (pl_im - ph_im).astype(BF16)
            p_scr[2, r:r + rb, :] = (pl_re - ph_re).astype(BF16)
            p_scr[3, r:r + rb, :] = (pl_im + ph_im).astype(BF16)
        short_conv(xg_ref, o + 1, xg_scr)
        sk = sk_ref[o:o + 1, :]
        for r in range(0, half, rb):
            y_e = (jnp.dot(ce_ref[r:r + rb, :], p_scr[0], preferred_element_type=F32)
                   + jnp.dot(se_ref[r:r + rb, :], p_scr[1], preferred_element_type=F32)
                   + alt[r:r + rb] * pm_re)
            y_o = (jnp.dot(cot_ref[r:r + rb, :], p_scr[2], preferred_element_type=F32)
                   + jnp.dot(sot_ref[r:r + rb, :], p_scr[3], preferred_element_type=F32)
                   - alt[r:r + rb] * pm_im)
            z_scr[0, r:r + rb, :] = xg_scr[0, r:r + rb, :] * (y_e + z_scr[0, r:r + rb, :] * sk)
            z_scr[1, r:r + rb, :] = xg_scr[1, r:r + rb, :] * (y_o + z_scr[1, r:r + rb, :] * sk)
    ge, go = halves(gate_ref)
    tiles = o_ref.shape[0]
    o_ref[:, 0] = (z_scr[0] * _silu(ge)).astype(o_ref.dtype).reshape(tiles, HY_PARITY_ROWS // 2, cb)
    o_ref[:, 1] = (z_scr[1] * _silu(go)).astype(o_ref.dtype).reshape(tiles, HY_PARITY_ROWS // 2, cb)


def hyena_core(proj5, conv_w, conv_b, kr, ki, kmid, skip, mats, cb):
    nseq, tiles, _, prow, _ = proj5.shape
    seq_len = tiles * 2 * prow
    half = seq_len // 2
    nj = HY_WIDTH // cb
    once = pl.Buffered(1)

    def act(k):
        return pl.BlockSpec((None, tiles, 2, prow, cb), lambda j, b: (b, 0, 0, 0, k * nj + j))

    def spec(k):
        return pl.BlockSpec((seq_len, cb), lambda j, b: (0, k * nj + j), pipeline_mode=once)

    dft = pl.BlockSpec((half, half), lambda j, b: (0, 0), pipeline_mode=once)
    body = functools.partial(_hyena_body, seq_len=seq_len)
    return pl.pallas_call(
        body,
        grid=(nj, nseq),
        in_specs=[act(0), act(1), act(2), act(3),
                  pl.BlockSpec((HY_SHORT, 3, cb), lambda j, b: (0, 0, j)),
                  pl.BlockSpec((3, cb), lambda j, b: (0, j)),
                  spec(0), spec(0), spec(1), spec(1),
                  pl.BlockSpec((2 * HY_ORDER, cb), lambda j, b: (0, j)),
                  pl.BlockSpec((HY_ORDER, cb), lambda j, b: (0, j))] + [dft] * 6,
        out_specs=pl.BlockSpec((None, tiles, 2, prow, cb), lambda j, b: (b, 0, 0, 0, j)),
        out_shape=jax.ShapeDtypeStruct((nseq, tiles, 2, prow, HY_WIDTH), BF16),
        scratch_shapes=[pltpu.VMEM((2, half, cb), F32), pltpu.VMEM((2, half, cb), F32),
                        pltpu.VMEM((4, half, cb), BF16)],
        compiler_params=_params("parallel", "arbitrary", vmem=VMEM_LIMIT_BYTES),
        name="hyena",
    )(proj5, proj5, proj5, proj5,
      conv_w.reshape(HY_SHORT, 3, HY_WIDTH), conv_b.reshape(3, HY_WIDTH),
      kr, ki, kr, ki, kmid, skip, *mats)


def parity_permutation():
    i = jnp.arange(HY_PARITY_ROWS)
    h = HY_PARITY_ROWS // 2
    src = jnp.where(i < h, 2 * i, 2 * (i - h) + 1)
    p = (src[:, None] == jnp.arange(HY_PARITY_ROWS)[None, :]).astype(BF16)
    return p, jnp.transpose(p)


def _inproj_parity_body(x_ref, g_ref, mod_ref, p_ref, w_ref, o_ref, h_scr):
    @pl.when(pl.program_id(1) == 0)
    def _():
        x = x_ref[...]
        y = x * lax.rsqrt(jnp.mean(x * x, axis=-1, keepdims=True) + EPS) * g_ref[...]
        h = (y * (1.0 + mod_ref[1:2, :]) + mod_ref[0:1, :]).astype(BF16)
        for r in range(0, x.shape[0], HY_PARITY_ROWS):
            h_scr[r:r + HY_PARITY_ROWS, :] = jnp.dot(p_ref[...], h[r:r + HY_PARITY_ROWS],
                                                     preferred_element_type=F32).astype(BF16)

    out = jnp.dot(h_scr[...], w_ref[...], preferred_element_type=F32).astype(o_ref.dtype)
    o_ref[...] = out.reshape(o_ref.shape)


def in_proj_parity(x, g, mod, p, w, rows_per_mod, tm, tn):
    n, nout = x.shape[0], w.shape[1]
    pr = HY_PARITY_ROWS
    return pl.pallas_call(
        _inproj_parity_body,
        grid=(n // tm, nout // tn),
        in_specs=[pl.BlockSpec((tm, D_MODEL), lambda i, j: (i, 0)),
                  pl.BlockSpec((1, D_MODEL), lambda i, j: (0, 0)),
                  pl.BlockSpec((None, 3, D_MODEL), lambda i, j: ((i * tm) // rows_per_mod, 0, 0)),
                  pl.BlockSpec((pr, pr), lambda i, j: (0, 0)),
                  pl.BlockSpec((D_MODEL, tn), lambda i, j: (0, j))],
        out_specs=pl.BlockSpec((tm // pr, 2, pr // 2, tn), lambda i, j: (i, 0, 0, j)),
        out_shape=jax.ShapeDtypeStruct((n // pr, 2, pr // 2, nout), BF16),
        scratch_shapes=[pltpu.VMEM((tm, D_MODEL), BF16)],
        compiler_params=_params("parallel", "arbitrary", vmem=VMEM_LIMIT_BYTES),
        name="in_proj_parity",
    )(x, g.reshape(1, D_MODEL), mod, p, w)


def _outproj_parity_body(y_ref, pt_ref, w_ref, x_ref, mod_ref, o_ref):
    tm = x_ref.shape[0]
    y = y_ref[...].reshape(tm, y_ref.shape[-1])
    parts = [jnp.dot(pt_ref[...], y[r:r + HY_PARITY_ROWS], preferred_element_type=F32).astype(BF16)
             for r in range(0, tm, HY_PARITY_ROWS)]
    out = jnp.dot(jnp.concatenate(parts, axis=0), w_ref[...], preferred_element_type=F32)
    o_ref[...] = x_ref[...] + mod_ref[2:3, :] * out


def out_proj_parity(y, pt, w, x, mod, rows_per_mod, tm):
    n = x.shape[0]
    width = y.shape[-1]
    pr = HY_PARITY_ROWS
    return pl.pallas_call(
        _outproj_parity_body,
        grid=(n // tm,),
        in_specs=[pl.BlockSpec((tm // pr, 2, pr // 2, width), lambda i: (i, 0, 0, 0)),
                  pl.BlockSpec((pr, pr), lambda i: (0, 0)),
                  pl.BlockSpec((width, D_MODEL), lambda i: (0, 0)),
                  pl.BlockSpec((tm, D_MODEL), lambda i: (i, 0)),
                  pl.BlockSpec((None, 3, D_MODEL), lambda i: ((i * tm) // rows_per_mod, 0, 0))],
        out_specs=pl.BlockSpec((tm, D_MODEL), lambda i: (i, 0)),
        out_shape=jax.ShapeDtypeStruct((n, D_MODEL), F32),
        compiler_params=_params("parallel", vmem=VMEM_LIMIT_BYTES),
        name="out_proj_parity",
    )(y, pt, w, x, mod)


def hyena_layer(x, g, mod, rows_per_mod, nseq, seq_len, w_in, conv_w, conv_b, filt_w, skip, w_out, perms, tm, cb):
    par_p, par_pt = perms
    pr = HY_PARITY_ROWS
    proj = in_proj_parity(x, g, mod, par_p, w_in, rows_per_mod, tm, PROJ_COLS)
    proj5 = proj.reshape(nseq, seq_len // pr, 2, pr // 2, 4 * HY_WIDTH)
    mats = dft_matrices(seq_len)
    kr, ki, kmid = hyena_filter_spectrum(seq_len, mats, *filt_w)
    y = hyena_core(proj5, conv_w, conv_b, kr, ki, kmid, skip, mats, cb)
    y = y.reshape(nseq * seq_len // pr, 2, pr // 2, HY_WIDTH)
    return out_proj_parity(y, par_pt, w_out, x, mod, rows_per_mod, min(tm, OUT_ROWS))


def kernel(x_prompt, x_sample, c, state_s5, state_ret, c_ctx, norm_g, mod_w, mod_b, s5_w_in, s5_lam_re, s5_lam_im, s5_log_step, s5_b_re, s5_b_im, s5_c_re, s5_c_im, s5_d, s5_w_glu, s5_b_glu, s5_w_out, ret_w_in, ret_decay_logit, ret_w_out, hy_w_in, hy_conv_w, hy_conv_b, hy_f_w1, hy_f_b1, hy_f_w2, hy_f_b2, hy_f_w3, hy_skip, hy_w_out, final_g):
    n_ctx, l_ctx, _ = x_prompt.shape
    n_dec, l_dec, _ = x_sample.shape
    xc = x_prompt.reshape(n_ctx * l_ctx, D_MODEL)
    xl = x_sample.reshape(n_dec * l_dec, D_MODEL)

    pad = (-(1 + n_dec)) % 8
    cvecs = jnp.concatenate([c_ctx[None, :], c, jnp.zeros((pad, D_MODEL), F32)], axis=0)
    mods = ada_mod_all(cvecs, mod_w, mod_b).reshape(DEPTH, -1, 3, D_MODEL)

    tm = 2048
    perm = s5_lane_permutation()
    last_is_s5 = (DEPTH - 1) % N_MIXERS == 0
    s5_new, ret_new = [], []
    for i in range(DEPTH):
        kind, j = i % N_MIXERS, i // N_MIXERS
        mod_c = mods[i, 0:1]
        mod_l = mods[i, 1:1 + n_dec]
        g = norm_g[i]
        if kind == 0:
            tables = s5_tables(s5_lam_re[j], s5_lam_im[j], s5_log_step[j], s5_b_re[j], s5_b_im[j],
                               s5_c_re[j], s5_c_im[j], s5_d[j])
            w = (s5_w_in[j].astype(BF16), tables, perm, s5_w_glu[j].astype(BF16), s5_b_glu[j],
                 s5_w_out[j].astype(BF16))
            fg = final_g if last_is_s5 and i == DEPTH - 1 else None
            xc, st = s5_layer(xc, g, mod_c, n_ctx, l_ctx, None, *w, nb=n_ctx, final_g=fg)
            xl, _ = s5_layer(xl, g, mod_l, n_dec, l_dec, state_s5[:, j], *w, nb=S5_SEQS_PER_STEP, final_g=fg)
            s5_new.append(st)
        elif kind == 1:
            w_in = ret_w_in[j]
            kscale = jnp.concatenate([jnp.ones((RET_QK,), F32), jnp.full((RET_QK,), RET_DK ** -0.5, F32),
                                      jnp.ones((2 * RET_V,), F32)])
            w_in = (w_in * kscale[None, :]).astype(BF16)
            lg = jax.nn.log_sigmoid(ret_decay_logit[j])
            lg = jnp.broadcast_to(jnp.transpose(lg)[:, :, None], (RET_HEADS, 2, 128))
            w = (w_in, lg, ret_w_out[j].astype(BF16), tm)
            xc, st = retention_layer(xc, g, mod_c, n_ctx * l_ctx, n_ctx, l_ctx, None, False, *w)
            xl, _ = retention_layer(xl, g, mod_l, l_dec, n_dec, l_dec, state_ret[:, j], True, *w)
            ret_new.append(st)
        else:
            filt_w = (hy_f_w1[j], hy_f_b1[j], hy_f_w2[j], hy_f_b2[j], hy_f_w3[j])
            w = (hy_w_in[j].astype(BF16), hy_conv_w[j], hy_conv_b[j], filt_w, hy_skip[j], hy_w_out[j].astype(BF16),
                 parity_permutation(), tm)
            xc = hyena_layer(xc, g, mod_c, n_ctx * l_ctx, n_ctx, l_ctx, *w, cb=1024)
            xl = hyena_layer(xl, g, mod_l, l_dec, n_dec, l_dec, *w, cb=256)

    if not last_is_s5:
        xc, xl = final_norm(xc, final_g, tm), final_norm(xl, final_g, tm)
    y_prompt = xc.reshape(n_ctx, l_ctx, D_MODEL)
    y_sample = xl.reshape(n_dec, l_dec, D_MODEL)
    new_state_s5 = jnp.stack(s5_new, axis=1)
    new_state_ret = jnp.stack(ret_new, axis=1)
    return (y_prompt, y_sample, new_state_s5, new_state_ret)
```
